```python
import math
import jax, jax.numpy as jnp
from jax import lax
import numpy as np


D_MODEL = 2048
BATCH = 1
SEQ = 16384
DEPTH = 4
DEC_BATCH = 16
DEC_SEQ = 16
PAST_LEN = 4096

CHUNK = 64
N_MIXERS = 3
N_HGRN = len(range(0, DEPTH, N_MIXERS))
N_POOL = len(range(1, DEPTH, N_MIXERS))
N_ATTN = len(range(2, DEPTH, N_MIXERS))
HG_EXPAND = 128
HG_HEADS = D_MODEL // HG_EXPAND
HG_DK = HG_EXPAND
HG_FD = HG_HEADS * HG_DK
HG_DV = D_MODEL // HG_HEADS
F_MIN = 1e-6
POOL_WINDOWS = (2, 4, 8, 16)
POOL_GROUPS = len(POOL_WINDOWS)
POOL_GC = D_MODEL // POOL_GROUPS
POOL_HIST = max(POOL_WINDOWS) - 1
DA_HEADS = 8
DA_DH = D_MODEL // (2 * DA_HEADS)
Q_BLOCK = 128
REL_BUCKETS = 32
REL_MAX_DIST = 128
MASK_VALUE = -1e30
D_FF = ((8 * D_MODEL // 3 + 255) // 256) * 256
EPS = 1e-6

kernel_name = 'hybrid_stream_hgrn2_pool_diffattn'


def rmsnorm(x, g):
    xf = x.astype(jnp.float32)
    y = xf * lax.rsqrt(jnp.mean(xf * xf, axis=-1, keepdims=True) + EPS) * g.astype(jnp.float32)
    return y.astype(x.dtype)


def swiglu(x, w_gate, w_up, w_down):
    return (jax.nn.silu(x @ w_gate) * (x @ w_up)) @ w_down


def hgrn2_recurrence(q, k, v, logf, S0):
    B, T, H, DK = q.shape
    DV = v.shape[-1]
    C = min(CHUNK, T)
    n = T // C
    def to_chunks(a):
        return a.astype(jnp.float32).reshape(B, n, C, H, a.shape[-1]).transpose(1, 0, 3, 2, 4)
    qc, kc, vc, gc = to_chunks(q), to_chunks(k), to_chunks(v), to_chunks(logf)
    causal = jnp.tril(jnp.ones((C, C), dtype=bool))[:, :, None]
    causal_f = causal.astype(jnp.float32)
    def step(S, inp):
        qi, ki, vi, gi = inp
        G = jnp.cumsum(gi, axis=2)
        inter = jnp.einsum('bhtk,bhkv->bhtv', qi * jnp.exp(G), S)
        diff = G[:, :, :, None, :] - G[:, :, None, :, :]
        decay = jnp.exp(jnp.where(causal, diff, 0.0)) * causal_f
        scores = jnp.einsum('bhtk,bhsk,bhtsk->bhts', qi, ki, decay)
        intra = jnp.einsum('bhts,bhsv->bhtv', scores, vi)
        G_last = G[:, :, -1]
        k_dec = ki * jnp.exp(G_last[:, :, None, :] - G)
        S_new = jnp.exp(G_last)[..., None] * S + jnp.einsum('bhsk,bhsv->bhkv', k_dec, vi)
        return S_new, inter + intra
    S_T, o = lax.scan(step, S0.astype(jnp.float32), (qc, kc, vc, gc))
    o = o.transpose(1, 0, 3, 2, 4).reshape(B, T, H, DV)
    return o, S_T


def hgrn2_mixer(xn, S0, w_q, w_f, w_i, w_g, w_o, lb, gain):
    B, T, D = xn.shape
    q = (xn @ w_q).reshape(B, T, HG_HEADS, HG_DK)
    f_pre = (xn @ w_f).reshape(B, T, HG_HEADS, HG_DK).astype(jnp.float32)
    lbh = lb.reshape(HG_HEADS, HG_DK)
    f = lbh + (1.0 - lbh) * jax.nn.sigmoid(f_pre)
    logf = jnp.log(jnp.maximum(f, F_MIN))
    k = 1.0 - f
    v = (xn @ w_i).reshape(B, T, HG_HEADS, HG_DV)
    o, S_T = hgrn2_recurrence(q, k, v, logf, S0)
    o = rmsnorm(o.astype(xn.dtype), gain).reshape(B, T, D) * jax.nn.silu(xn @ w_g)
    return o @ w_o, S_T


def pool_mixer(xn, hist, pos0, w_grp, scale):
    B, T, D = xn.shape
    P = POOL_HIST
    xa = jnp.concatenate([hist.astype(xn.dtype), xn], axis=1)
    xf = xa.astype(jnp.float32)
    cs = jnp.concatenate([jnp.zeros((B, 1, D), jnp.float32), jnp.cumsum(xf, axis=1)], axis=1)
    pos = pos0 + jnp.arange(T)
    xt = xf[:, P:]
    pooled = []
    for g, w in enumerate(POOL_WINDOWS):
        sl = slice(g * POOL_GC, (g + 1) * POOL_GC)
        wsum = cs[:, P + 1:P + 1 + T, sl] - cs[:, P + 1 - w:P + 1 - w + T, sl]
        cnt = jnp.minimum(w, pos + 1).astype(jnp.float32)[None, :, None]
        pooled.append(wsum / cnt - xt[:, :, sl])
    pooled = jnp.stack(pooled, axis=2).astype(xn.dtype)
    out = jnp.einsum('btgc,gce->btge', pooled, w_grp).reshape(B, T, D) * scale
    return out, xa[:, -P:]


def rel_bucket(rel):
    nb = REL_BUCKETS // 2
    max_exact = nb // 2
    n = jnp.abs(rel)
    large = max_exact + (jnp.log(jnp.maximum(n, max_exact).astype(jnp.float32) / max_exact)
                         / math.log(REL_MAX_DIST / max_exact) * (nb - max_exact)).astype(jnp.int32)
    large = jnp.minimum(large, nb - 1)
    return jnp.where(rel > 0, nb, 0) + jnp.where(n < max_exact, n, large)


def diff_attention_core(q, k, v, q_pos, lam, rel_table):
    B, Tq, H, _, dh = q.shape
    Tk = k.shape[1]
    QB = min(Q_BLOCK, Tq)
    nb = Tq // QB
    k_pos = jnp.arange(Tk)
    qb = q.reshape(B, nb, QB, H, 2, dh).transpose(1, 0, 2, 3, 4, 5)
    pb = q_pos.reshape(nb, QB)
    scale = dh ** -0.5
    def block(args):
        qi, pi = args
        s = jnp.einsum('bqhcd,bkhcd->bhcqk', qi, k).astype(jnp.float32) * scale
        bias = rel_table.astype(jnp.float32)[rel_bucket(k_pos[None, :] - pi[:, None])]
        s = s + bias.transpose(2, 0, 1)[None, :, None]
        mask = (k_pos[None, :] // CHUNK) <= (pi[:, None] // CHUNK)
        s = jnp.where(mask, s, MASK_VALUE)
        p = jax.nn.softmax(s, axis=-1)
        a = p[:, :, 0] - lam * p[:, :, 1]
        return jnp.einsum('bhqk,bkhe->bqhe', a.astype(v.dtype), v)
    o = lax.map(block, (qb, pb))
    return o.transpose(1, 0, 2, 3, 4).reshape(B, Tq, H, 2 * dh)


def diff_attn_mixer(xn, k_past, v_past, q_pos, w_q, w_k, w_v, w_o, lq1, lk1, lq2, lk2,
                    sub_gain, rel_table, lambda_init):
    B, T, D = xn.shape
    q = (xn @ w_q).reshape(B, T, DA_HEADS, 2, DA_DH)
    k_new = (xn @ w_k).reshape(B, T, 2 * DA_HEADS, DA_DH)
    v_new = (xn @ w_v).reshape(B, T, DA_HEADS, 2 * DA_DH)
    if k_past is None:
        k_all, v_all = k_new, v_new
    else:
        k_all = jnp.concatenate([k_past.astype(k_new.dtype), k_new], axis=1)
        v_all = jnp.concatenate([v_past.astype(v_new.dtype), v_new], axis=1)
    Tk = k_all.shape[1]
    lam = (jnp.exp(jnp.sum(lq1.astype(jnp.float32) * lk1.astype(jnp.float32)))
           - jnp.exp(jnp.sum(lq2.astype(jnp.float32) * lk2.astype(jnp.float32))) + lambda_init)
    o = diff_attention_core(q, k_all.reshape(B, Tk, DA_HEADS, 2, DA_DH), v_all, q_pos, lam, rel_table)
    o = rmsnorm(o, sub_gain) * (1.0 - lambda_init)
    return o.reshape(B, T, D) @ w_o, k_new, v_new


def setup_inputs(seed: int = 0) -> dict:
    key = jax.random.key(seed)
    ks = iter(jax.random.split(key, 40))
    def nrm(shape, s=1.0):
        return s * jax.random.normal(next(ks), shape, jnp.float32)
    D = D_MODEL
    return {
        'x_prompt': nrm((BATCH, SEQ, D)),
        'x_sample': nrm((DEC_BATCH, DEC_SEQ, D)),
        'state_hgrn': nrm((N_HGRN, DEC_BATCH, HG_HEADS, HG_DK, HG_DV), 0.5),
        'state_pool': nrm((N_POOL, DEC_BATCH, POOL_HIST, D)),
        'cache_k': nrm((N_ATTN, DEC_BATCH, PAST_LEN, 2 * DA_HEADS, DA_DH)),
        'cache_v': nrm((N_ATTN, DEC_BATCH, PAST_LEN, DA_HEADS, 2 * DA_DH)),
        'norm_mix': 1.0 + nrm((DEPTH, D), 0.02),
        'norm_ffn': 1.0 + nrm((DEPTH, D), 0.02),
        'norm_final': 1.0 + nrm((D,), 0.02),
        'hgrn_w_q': nrm((N_HGRN, D, HG_FD), D ** -0.5),
        'hgrn_w_f': nrm((N_HGRN, D, HG_FD), D ** -0.5),
        'hgrn_w_i': nrm((N_HGRN, D, D), D ** -0.5),
        'hgrn_w_g': nrm((N_HGRN, D, D), D ** -0.5),
        'hgrn_w_o': nrm((N_HGRN, D, D), D ** -0.5),
        'hgrn_lb_logits': nrm((N_HGRN, HG_FD), 0.5),
        'hgrn_norm_gain': 1.0 + nrm((N_HGRN, HG_DV), 0.02),
        'pool_w': nrm((N_POOL, POOL_GROUPS, POOL_GC, POOL_GC), POOL_GC ** -0.5),
        'pool_scale': 1.0 + nrm((N_POOL, D), 0.1),
        'attn_w_q': nrm((N_ATTN, D, 2 * DA_HEADS * DA_DH), D ** -0.5),
        'attn_w_k': nrm((N_ATTN, D, 2 * DA_HEADS * DA_DH), D ** -0.5),
        'attn_w_v': nrm((N_ATTN, D, 2 * DA_HEADS * DA_DH), D ** -0.5),
        'attn_w_o': nrm((N_ATTN, 2 * DA_HEADS * DA_DH, D), D ** -0.5),
        'attn_lambda_q1': nrm((N_ATTN, DA_DH), 0.1),
        'attn_lambda_k1': nrm((N_ATTN, DA_DH), 0.1),
        'attn_lambda_q2': nrm((N_ATTN, DA_DH), 0.1),
        'attn_lambda_k2': nrm((N_ATTN, DA_DH), 0.1),
        'attn_subln_gain': 1.0 + nrm((N_ATTN, 2 * DA_DH), 0.02),
        'rel_bias_table': nrm((REL_BUCKETS, DA_HEADS), 0.5),
        'ffn_w_gate': nrm((DEPTH, D, D_FF), D ** -0.5),
        'ffn_w_up': nrm((DEPTH, D, D_FF), D ** -0.5),
        'ffn_w_down': nrm((DEPTH, D_FF, D), D_FF ** -0.5),
    }


def reference(x_prompt, x_sample, state_hgrn, state_pool, cache_k, cache_v,
              norm_mix, norm_ffn, norm_final,
              hgrn_w_q, hgrn_w_f, hgrn_w_i, hgrn_w_g, hgrn_w_o, hgrn_lb_logits, hgrn_norm_gain,
              pool_w, pool_scale,
              attn_w_q, attn_w_k, attn_w_v, attn_w_o,
              attn_lambda_q1, attn_lambda_k1, attn_lambda_q2, attn_lambda_k2, attn_subln_gain,
              rel_bias_table, ffn_w_gate, ffn_w_up, ffn_w_down):
    B, T = x_prompt.shape[0], x_prompt.shape[1]
    Ts = x_sample.shape[1]
    p_lb = jax.nn.softmax(hgrn_lb_logits.astype(jnp.float32), axis=0)
    lower_bounds = jnp.maximum(jnp.cumsum(p_lb, axis=0) - p_lb[0], 0.0)
    pos_p = jnp.arange(T)
    pos_s = PAST_LEN + jnp.arange(Ts)
    x_p, x_s = x_prompt, x_sample
    hg_p, hg_s, pl_p, pl_s, k_p, v_p, k_s, v_s = [], [], [], [], [], [], [], []
    for i in range(DEPTH):
        m, j = i % N_MIXERS, i // N_MIXERS
        xn_p = rmsnorm(x_p, norm_mix[i])
        xn_s = rmsnorm(x_s, norm_mix[i])
        if m == 0:
            w = (hgrn_w_q[j], hgrn_w_f[j], hgrn_w_i[j], hgrn_w_g[j], hgrn_w_o[j],
                 lower_bounds[j], hgrn_norm_gain[j])
            o_p, S_p = hgrn2_mixer(xn_p, jnp.zeros((B, HG_HEADS, HG_DK, HG_DV), jnp.float32), *w)
            o_s, S_s = hgrn2_mixer(xn_s, state_hgrn[j], *w)
            hg_p.append(S_p.astype(x_prompt.dtype))
            hg_s.append(S_s.astype(state_hgrn.dtype))
        elif m == 1:
            o_p, P_p = pool_mixer(xn_p, jnp.zeros((B, POOL_HIST, D_MODEL), xn_p.dtype), 0,
                                  pool_w[j], pool_scale[j])
            o_s, P_s = pool_mixer(xn_s, state_pool[j], PAST_LEN, pool_w[j], pool_scale[j])
            pl_p.append(P_p)
            pl_s.append(P_s)
        else:
            lambda_init = 0.8 - 0.6 * math.exp(-0.3 * i)
            w = (attn_w_q[j], attn_w_k[j], attn_w_v[j], attn_w_o[j],
                 attn_lambda_q1[j], attn_lambda_k1[j], attn_lambda_q2[j], attn_lambda_k2[j],
                 attn_subln_gain[j], rel_bias_table, lambda_init)
            o_p, kn_p, vn_p = diff_attn_mixer(xn_p, None, None, pos_p, *w)
            o_s, kn_s, vn_s = diff_attn_mixer(xn_s, cache_k[j], cache_v[j], pos_s, *w)
            k_p.append(kn_p)
            v_p.append(vn_p)
            k_s.append(kn_s)
            v_s.append(vn_s)
        x_p = x_p + o_p
        x_s = x_s + o_s
        x_p = x_p + swiglu(rmsnorm(x_p, norm_ffn[i]), ffn_w_gate[i], ffn_w_up[i], ffn_w_down[i])
        x_s = x_s + swiglu(rmsnorm(x_s, norm_ffn[i]), ffn_w_gate[i], ffn_w_up[i], ffn_w_down[i])
    y_prompt = rmsnorm(x_p, norm_final)
    y_sample = rmsnorm(x_s, norm_final)
    new_hgrn_prompt = jnp.stack(hg_p)
    new_hgrn_sample = jnp.stack(hg_s)
    new_pool_prompt = jnp.stack(pl_p)
    new_pool_sample = jnp.stack(pl_s)
    new_k_prompt = jnp.stack(k_p)
    new_v_prompt = jnp.stack(v_p)
    new_k_sample = jnp.stack(k_s)
    new_v_sample = jnp.stack(v_s)
    return (y_prompt, y_sample, new_hgrn_prompt, new_hgrn_sample, new_pool_prompt, new_pool_sample,
            new_k_prompt, new_v_prompt, new_k_sample, new_v_sample)
```

```python
import functools
import math

import numpy as np
import jax
import jax.numpy as jnp
from jax import lax
from jax.experimental import pallas as pl
from jax.experimental.pallas import tpu as pltpu

F32 = jnp.float32
BF16 = jnp.bfloat16

EPS = 1e-6
F_MIN = 1e-6
MASK_VALUE = -1e30
HG_HEADS = 16
HG_DK = 128
POOL_WINDOWS = (2, 4, 8, 16)
POOL_HIST = 15
DA_HEADS = 8
DA_DH = 128
ATT_CHUNK = 64
REL_BUCKETS = 32
REL_MAX_DIST = 128
N_MIXERS = 3

CFG = dict(
    mm_tm=512, mm_tn=512,
    ffn_tm=512, ffn_tf=512,
    pool_tm=512,
    hg_chunk=64, hg_rows=512, hg_heads=2,
    fa_tq=512, fa_tk=512,
    sa_tk=512,
    vmem=56 * 1024 * 1024,
)


def _cp(sem):
    return pltpu.CompilerParams(dimension_semantics=sem, vmem_limit_bytes=CFG["vmem"])


def _rms(x, g):
    return x * lax.rsqrt(jnp.mean(x * x, axis=-1, keepdims=True) + EPS) * g


def _dot(a, b):
    return jnp.dot(a, b, preferred_element_type=F32)


def _dot_nt(a, b):
    return lax.dot_general(a, b, (((1,), (1,)), ((), ())), preferred_element_type=F32)


def _norm_mm_kernel(x_ref, g_ref, w_ref, *rest):
    outs, xn_ref = rest[:-1], rest[-1]

    @pl.when(pl.program_id(1) == 0)
    def _():
        xn_ref[...] = _rms(x_ref[...], g_ref[...]).astype(BF16)

    y = _dot(xn_ref[...], w_ref[...])
    for o in outs:
        o[...] = y.astype(o.dtype)


def norm_mm(x, g, w, out_dtypes):
    m, d = x.shape
    n = w.shape[1]
    tm, tn = min(CFG["mm_tm"], m), min(CFG["mm_tn"], n)
    res = pl.pallas_call(
        _norm_mm_kernel,
        grid=(m // tm, n // tn),
        in_specs=[pl.BlockSpec((tm, d), lambda i, j: (i, 0)),
                  pl.BlockSpec((1, d), lambda i, j: (0, 0)),
                  pl.BlockSpec((d, tn), lambda i, j: (0, j))],
        out_specs=[pl.BlockSpec((tm, tn), lambda i, j: (i, j)) for _ in out_dtypes],
        out_shape=[jax.ShapeDtypeStruct((m, n), dt) for dt in out_dtypes],
        scratch_shapes=[pltpu.VMEM((tm, d), BF16)],
        compiler_params=_cp(("parallel", "arbitrary")),
        name="norm_mm",
    )(x, g.reshape(1, d), w)
    return res


def _mm_res_kernel(a_ref, w_ref, r_ref, o_ref):
    o_ref[...] = r_ref[...] + _dot(a_ref[...], w_ref[...])


def mm_res(a, w, r):
    m, k = a.shape
    n = w.shape[1]
    tm, tn = min(CFG["mm_tm"], m), min(CFG["mm_tn"], n)
    return pl.pallas_call(
        _mm_res_kernel,
        grid=(m // tm, n // tn),
        in_specs=[pl.BlockSpec((tm, k), lambda i, j: (i, 0)),
                  pl.BlockSpec((k, tn), lambda i, j: (0, j)),
                  pl.BlockSpec((tm, tn), lambda i, j: (i, j))],
        out_specs=pl.BlockSpec((tm, tn), lambda i, j: (i, j)),
        out_shape=jax.ShapeDtypeStruct((m, n), F32),
        compiler_params=_cp(("parallel", "arbitrary")),
        name="mm_res",
    )(a, w, r)


def _ffn_kernel(x_ref, g_ref, wg_ref, wu_ref, wd_ref, gf_ref, o_ref, xn_ref, *, final_norm):
    j = pl.program_id(1)

    @pl.when(j == 0)
    def _():
        x = x_ref[...]
        xn_ref[...] = _rms(x, g_ref[...]).astype(BF16)
        o_ref[...] = x

    xn = xn_ref[...]
    a = _dot(xn, wg_ref[...])
    b = _dot(xn, wu_ref[...])
    h = (a * jax.nn.sigmoid(a) * b).astype(BF16)
    o_ref[...] += _dot(h, wd_ref[...])

    if final_norm:
        @pl.when(j == pl.num_programs(1) - 1)
        def _():
            o_ref[...] = _rms(o_ref[...], gf_ref[...])


def ffn(x, g, wg, wu, wd, g_final=None):
    m, d = x.shape
    f = wg.shape[1]
    tm, tf = min(CFG["ffn_tm"], m), min(CFG["ffn_tf"], f)
    final_norm = g_final is not None
    gf = (g_final if final_norm else g).reshape(1, d)
    return pl.pallas_call(
        functools.partial(_ffn_kernel, final_norm=final_norm),
        grid=(m // tm, f // tf),
        in_specs=[pl.BlockSpec((tm, d), lambda i, j: (i, 0)),
                  pl.BlockSpec((1, d), lambda i, j: (0, 0)),
                  pl.BlockSpec((d, tf), lambda i, j: (0, j)),
                  pl.BlockSpec((d, tf), lambda i, j: (0, j)),
                  pl.BlockSpec((tf, d), lambda i, j: (j, 0)),
                  pl.BlockSpec((1, d), lambda i, j: (0, 0))],
        out_specs=pl.BlockSpec((tm, d), lambda i, j: (i, 0)),
        out_shape=jax.ShapeDtypeStruct((m, d), F32),
        scratch_shapes=[pltpu.VMEM((tm, d), BF16)],
        compiler_params=_cp(("parallel", "arbitrary")),
        name="ffn",
    )(x, g.reshape(1, d), wg, wu, wd, gf)


def _hgrn_levels(c):
    lv, h = [], c // 2
    while h >= 8:
        lv.append(h)
        h //= 2
    return lv


def _bcast_rows(g, rows, rep):
    return jnp.concatenate([jnp.broadcast_to(g[r:r + 1, :], (rep, g.shape[1])) for r in rows], axis=0)


def _hgrn_kernel(q_ref, f_ref, v_ref, gt_ref, lbl_ref, gain_ref, s0_ref, l_ref, og_ref, st_ref, stt_ref,
                 *, c, n_sub, hb, layer):
    cb = pl.program_id(2)
    dk = HG_DK

    @pl.when(cb == 0)
    def _():
        for h in range(hb):
            stt_ref[h] = s0_ref[0, h].T

    lg = lbl_ref[...]
    e = jnp.exp(lg - jnp.max(lg, axis=0, keepdims=True))
    p = e / jnp.sum(e, axis=0, keepdims=True)
    lb = p[0:1, :]
    for i in range(1, layer + 1):
        lb = lb + p[i:i + 1, :]
    lb = jnp.maximum(lb - p[0:1, :], 0.0)

    ti = lax.broadcasted_iota(jnp.int32, (c, c), 0)
    si = lax.broadcasted_iota(jnp.int32, (c, c), 1)
    levels = _hgrn_levels(c)
    masks = []
    for hh in levels:
        sh = int(math.log2(2 * hh))
        masks.append(((ti >> sh) == (si >> sh)) & ((ti & (2 * hh - 1)) >= hh) & ((si & (2 * hh - 1)) < hh))
    mask_loc = ((ti >> 3) == (si >> 3)) & (si <= ti)
    lmat = l_ref[...]
    gain = gain_ref[...]

    def chunk(ci, carry):
        r0 = pl.multiple_of(ci * c, c)
        fp = f_ref[pl.ds(r0, c), :]
        f = lb + (1.0 - lb) * jax.nn.sigmoid(fp)
        g = jnp.log(jnp.maximum(f, F_MIN))
        kk = 1.0 - f
        g_hi = g.astype(BF16)
        r1 = g - g_hi.astype(F32)
        g_mid = r1.astype(BF16)
        g_lo = (r1 - g_mid.astype(F32)).astype(BF16)
        gc = _dot(lmat, g_hi) + _dot(lmat, g_mid) + _dot(lmat, g_lo)
        qa = q_ref[pl.ds(r0, c), :]
        va = v_ref[pl.ds(r0, c), :]
        ga = gt_ref[pl.ds(r0, c), :]
        for h in range(hb):
            sl = slice(h * dk, (h + 1) * dk)
            gh, qh, kh = gc[:, sl], qa[:, sl], kk[:, sl]
            vh = va[:, sl].astype(BF16)
            stt = stt_ref[h]
            inter = _dot_nt((qh * jnp.exp(gh)).astype(BF16), stt.astype(BF16))
            g_last = gh[c - 1:c, :]
            k_dec = (kh * jnp.exp(g_last - gh)).astype(BF16)
            stt_ref[h] = stt * jnp.exp(g_last) + lax.dot_general(
                vh, k_dec, (((0,), (0,)), ((), ())), preferred_element_type=F32)
            g_loc = _bcast_rows(gh, [8 * b + 3 for b in range(c // 8)], 8)
            sc = jnp.where(mask_loc,
                           _dot_nt((qh * jnp.exp(gh - g_loc)).astype(BF16),
                                   (kh * jnp.exp(g_loc - gh)).astype(BF16)), 0.0)
            for hh, mk in zip(levels, masks):
                g_mid_rows = _bcast_rows(gh, [b * 2 * hh + hh - 1 for b in range(c // (2 * hh))], 2 * hh)
                fac = jnp.exp(-jnp.abs(gh - g_mid_rows))
                sc = jnp.where(mk, _dot_nt((qh * fac).astype(BF16), (kh * fac).astype(BF16)), sc)
            o = inter + _dot(sc.astype(BF16), vh)
            on = _rms(o, gain)
            gate = ga[:, sl]
            og_ref[pl.ds(r0, c), sl] = (on * (gate * jax.nn.sigmoid(gate))).astype(BF16)
        return carry

    lax.fori_loop(0, n_sub, chunk, 0)

    @pl.when(cb == pl.num_programs(2) - 1)
    def _():
        for h in range(hb):
            st_ref[0, h] = stt_ref[h].T


def hgrn_recurrence(proj, lb_logits, gain, s0, batch, t, layer):
    d = HG_HEADS * HG_DK
    c = min(CFG["hg_chunk"], t)
    rows = min(CFG["hg_rows"], t)
    hb = CFG["hg_heads"]
    w = hb * HG_DK
    ncb = t // rows
    n_layers = lb_logits.shape[0]
    lmat = jnp.asarray(np.tril(np.ones((c, c), np.float32)), BF16)

    def col(sec):
        return lambda b, hg, cb: (b * ncb + cb, sec * (d // w) + hg)

    og, st = pl.pallas_call(
        functools.partial(_hgrn_kernel, c=c, n_sub=rows // c, hb=hb, layer=layer),
        grid=(batch, HG_HEADS // hb, ncb),
        in_specs=[pl.BlockSpec((rows, w), col(0)),
                  pl.BlockSpec((rows, w), col(1)),
                  pl.BlockSpec((rows, w), col(2)),
                  pl.BlockSpec((rows, w), col(3)),
                  pl.BlockSpec((n_layers, w), lambda b, hg, cb: (0, hg)),
                  pl.BlockSpec((1, HG_DK), lambda b, hg, cb: (0, 0)),
                  pl.BlockSpec((1, hb, HG_DK, HG_DK), lambda b, hg, cb: (b, hg, 0, 0)),
                  pl.BlockSpec((c, c), lambda b, hg, cb: (0, 0))],
        out_specs=[pl.BlockSpec((rows, w), lambda b, hg, cb: (b * ncb + cb, hg)),
                   pl.BlockSpec((1, hb, HG_DK, HG_DK), lambda b, hg, cb: (b, hg, 0, 0))],
        out_shape=[jax.ShapeDtypeStruct((batch * t, d), BF16),
                   jax.ShapeDtypeStruct((batch, HG_HEADS, HG_DK, HG_DK), F32)],
        scratch_shapes=[pltpu.VMEM((hb, HG_DK, HG_DK), F32)],
        compiler_params=_cp(("parallel", "parallel", "arbitrary")),
        name="hgrn_recurrence",
    )(proj, proj, proj, proj, lb_logits, gain.reshape(1, HG_DK), s0, lmat)
    return og, st


def _pool_kernel(x_ref, g_ref, hist_ref, w_ref, sc_ref, o_ref, st_ref, buf_ref, *, tm, pos0):
    tb = pl.program_id(1)
    x = x_ref[...]
    xn = _rms(x, g_ref[...])
    hp = POOL_HIST + 1

    @pl.when(tb == 0)
    def _():
        buf_ref[0:hp, :] = hist_ref[0]

    @pl.when(tb > 0)
    def _():
        buf_ref[0:hp, :] = buf_ref[tm:tm + hp, :]

    buf_ref[hp:hp + tm, :] = xn
    st_ref[0] = buf_ref[tm:tm + hp, :]
    pos = pos0 + tb * tm + lax.broadcasted_iota(jnp.int32, (tm, 1), 0)
    gc = x.shape[1] // len(POOL_WINDOWS)
    for gi, win in enumerate(POOL_WINDOWS):
        cs = slice(gi * gc, (gi + 1) * gc)
        ws = xn[:, cs]
        for sft in range(1, win):
            ws = ws + buf_ref[hp - sft:hp - sft + tm, cs]
        cnt = jnp.minimum(win, pos + 1).astype(F32)
        pooled = (ws / cnt - xn[:, cs]).astype(BF16)
        o_ref[:, cs] = x[:, cs] + _dot(pooled, w_ref[gi]) * sc_ref[:, cs]


def pool_mixer(x, g, hist, w, scale, batch, t, pos0):
    d = x.shape[1]
    tm = min(CFG["pool_tm"], t)
    nb = t // tm
    hp = POOL_HIST + 1
    hist16 = jnp.concatenate([jnp.zeros((batch, 1, d), F32), hist], axis=1)
    ng, gc = w.shape[0], w.shape[1]
    return pl.pallas_call(
        functools.partial(_pool_kernel, tm=tm, pos0=pos0),
        grid=(batch, nb),
        in_specs=[pl.BlockSpec((tm, d), lambda b, i: (b * nb + i, 0)),
                  pl.BlockSpec((1, d), lambda b, i: (0, 0)),
                  pl.BlockSpec((1, hp, d), lambda b, i: (b, 0, 0)),
                  pl.BlockSpec((ng, gc, gc), lambda b, i: (0, 0, 0)),
                  pl.BlockSpec((1, d), lambda b, i: (0, 0))],
        out_specs=[pl.BlockSpec((tm, d), lambda b, i: (b * nb + i, 0)),
                   pl.BlockSpec((1, hp, d), lambda b, i: (b, 0, 0))],
        out_shape=[jax.ShapeDtypeStruct(x.shape, F32),
                   jax.ShapeDtypeStruct((batch, hp, d), F32)],
        scratch_shapes=[pltpu.VMEM((tm + hp, d), F32)],
        compiler_params=_cp(("parallel", "arbitrary")),
        name="pool_mixer",
    )(x, g.reshape(1, d), hist16, w, scale.reshape(1, d))


def _rel_bucket_np(rel):
    nb = REL_BUCKETS // 2
    max_exact = nb // 2
    n = np.abs(rel)
    large = max_exact + (np.log(np.maximum(n, max_exact).astype(np.float64) / max_exact)
                         / math.log(REL_MAX_DIST / max_exact) * (nb - max_exact)).astype(np.int64)
    large = np.minimum(large, nb - 1)
    return (np.where(rel > 0, nb, 0) + np.where(n < max_exact, n, large)).astype(np.int32)


def _bucket_tile(q_pos, k_pos):
    q_pos, k_pos = np.asarray(q_pos)[:, None], np.asarray(k_pos)[None, :]
    b = _rel_bucket_np(k_pos - q_pos)
    return np.where((k_pos // ATT_CHUNK) <= (q_pos // ATT_CHUNK), b, -1).astype(np.int32)


def _bias_kernel(tab_ref, bk_ref, o_ref):
    h = pl.program_id(0)
    bk = bk_ref[...]
    bias = jnp.full(bk.shape, MASK_VALUE, F32)
    for b in range(REL_BUCKETS):
        bias = jnp.where(bk == b, tab_ref[b, h], bias)
    o_ref[0] = bias


def bias_tiles(table, bucket):
    r, c = bucket.shape
    nh = table.shape[1]
    return pl.pallas_call(
        _bias_kernel,
        grid=(nh,),
        in_specs=[pl.BlockSpec(memory_space=pltpu.SMEM),
                  pl.BlockSpec((r, c), lambda h: (0, 0))],
        out_specs=pl.BlockSpec((1, r, c), lambda h: (h, 0, 0)),
        out_shape=jax.ShapeDtypeStruct((nh, r, c), F32),
        compiler_params=_cp(("arbitrary",)),
        name="bias_tiles",
    )(table, jnp.asarray(bucket))


def _lambda(lq1, lk1, lq2, lk2, lambda_init):
    return (jnp.exp(jnp.sum(lq1 * lk1, axis=-1, keepdims=True))
            - jnp.exp(jnp.sum(lq2 * lk2, axis=-1, keepdims=True)) + lambda_init)


def _softmax_step(q, k, v, bias, m_ref, l_ref, acc_ref, idx, scale):
    s = _dot_nt(q, k) * scale + bias
    m_prev = m_ref[idx]
    m_new = jnp.maximum(m_prev, jnp.max(s, axis=-1, keepdims=True))
    p = jnp.exp(s - m_new)
    alpha = jnp.exp(m_prev - m_new)
    l_ref[idx] = alpha * l_ref[idx] + jnp.sum(p, axis=-1, keepdims=True)
    acc_ref[idx] = alpha * acc_ref[idx] + _dot(p.astype(BF16), v)
    m_ref[idx] = m_new


def _flash_kernel(tab_ref, q_ref, k_ref, v_ref, bprev_ref, bdiag_ref, lam_ref, gain_ref, o_ref,
                  m_ref, l_ref, acc_ref, *, tq, tk, far_bucket, lambda_init):
    h = pl.program_id(0)
    i = pl.program_id(1)
    dh = DA_DH
    scale = dh ** -0.5
    m_ref[...] = jnp.full(m_ref.shape, -jnp.inf, F32)
    l_ref[...] = jnp.zeros(l_ref.shape, F32)
    acc_ref[...] = jnp.zeros(acc_ref.shape, F32)
    q = q_ref[...]

    def block(j, bias_fn):
        r0 = pl.multiple_of(j * tk, tk)
        kb = k_ref[pl.ds(r0, tk), :]
        vb = v_ref[pl.ds(r0, tk), :]
        for c in range(2):
            _softmax_step(q[:, c * dh:(c + 1) * dh], kb[:, c * dh:(c + 1) * dh], vb, bias_fn(),
                          m_ref, l_ref, acc_ref, c, scale)

    c_far = tab_ref[far_bucket, h]

    def far(j, carry):
        block(j, lambda: c_far)
        return carry

    lax.fori_loop(0, i - 1, far, 0)

    @pl.when(i >= 1)
    def _():
        block(i - 1, lambda: bprev_ref[0])

    block(i, lambda: bdiag_ref[0])

    lam = _lambda(lam_ref[0:1, :], lam_ref[1:2, :], lam_ref[2:3, :], lam_ref[3:4, :], lambda_init)
    o = acc_ref[0] / l_ref[0] - lam * (acc_ref[1] / l_ref[1])
    o_ref[...] = (_rms(o, gain_ref[...]) * (1.0 - lambda_init)).astype(BF16)


def flash_diff_attention(q, k, v, table, lam_params, gain, lambda_init):
    t, d = q.shape
    tq, tk = min(CFG["fa_tq"], t), min(CFG["fa_tk"], t)
    assert tq == tk and tk % ATT_CHUNK == 0
    hw = 2 * DA_DH
    nh = d // hw
    bucket = _bucket_tile(tk + np.arange(tq), np.arange(2 * tk))
    tiles = bias_tiles(table, bucket)
    far = _rel_bucket_np(-np.arange(tk + 1, max(t, tk + 2)))
    far_bucket = int(far[0])
    assert (far == far_bucket).all()
    return pl.pallas_call(
        functools.partial(_flash_kernel, tq=tq, tk=tk, far_bucket=far_bucket, lambda_init=lambda_init),
        grid=(nh, t // tq),
        in_specs=[pl.BlockSpec(memory_space=pltpu.SMEM),
                  pl.BlockSpec((tq, hw), lambda h, i: (i, h)),
                  pl.BlockSpec((t, hw), lambda h, i: (0, h)),
                  pl.BlockSpec((t, hw), lambda h, i: (0, h)),
                  pl.BlockSpec((1, tq, tk), lambda h, i: (h, 0, 0)),
                  pl.BlockSpec((1, tq, tk), lambda h, i: (h, 0, 1)),
                  pl.BlockSpec((4, DA_DH), lambda h, i: (0, 0)),
                  pl.BlockSpec((1, hw), lambda h, i: (0, 0))],
        out_specs=pl.BlockSpec((tq, hw), lambda h, i: (i, h)),
        out_shape=jax.ShapeDtypeStruct((t, d), BF16),
        scratch_shapes=[pltpu.VMEM((2, tq, 1), F32), pltpu.VMEM((2, tq, 1), F32),
                        pltpu.VMEM((2, tq, hw), F32)],
        compiler_params=_cp(("parallel", "arbitrary")),
        name="flash_diff_attention",
    )(table, q, k, v, tiles, tiles, lam_params, gain.reshape(1, hw))


def _sample_attn_kernel(tab_ref, q_ref, ck_ref, cv_ref, kn_ref, vn_ref, blast_ref, bnew_ref, lam_ref, gain_ref,
                        o_ref, m_ref, l_ref, acc_ref, *, far_bucket, lambda_init):
    j = pl.program_id(1)
    nj = pl.num_programs(1)
    dh = DA_DH
    hw = 2 * dh
    nh = q_ref.shape[1] // hw
    scale = dh ** -0.5
    ts = q_ref.shape[0]

    @pl.when(j == 0)
    def _():
        m_ref[...] = jnp.full(m_ref.shape, -jnp.inf, F32)
        l_ref[...] = jnp.zeros(l_ref.shape, F32)
        acc_ref[...] = jnp.zeros(acc_ref.shape, F32)

    q = q_ref[...]

    def block(kb, vb, bias_fn):
        for h in range(nh):
            for c in range(2):
                cs = slice((2 * h + c) * dh, (2 * h + c + 1) * dh)
                _softmax_step(q[:, cs], kb[:, cs], vb[:, h * hw:(h + 1) * hw], bias_fn(h),
                              m_ref, l_ref, acc_ref, 2 * h + c, scale)

    kb = ck_ref[0].astype(BF16)
    vb = cv_ref[0].astype(BF16)

    @pl.when(j < nj - 1)
    def _():
        block(kb, vb, lambda h: tab_ref[far_bucket, h])

    @pl.when(j == nj - 1)
    def _():
        block(kb, vb, lambda h: blast_ref[h])
        block(kn_ref[...], vn_ref[...], lambda h: bnew_ref[h][:, :ts])
        lam = _lambda(lam_ref[0:1, :], lam_ref[1:2, :], lam_ref[2:3, :], lam_ref[3:4, :], lambda_init)
        for h in range(nh):
            o = acc_ref[2 * h] / l_ref[2 * h] - lam * (acc_ref[2 * h + 1] / l_ref[2 * h + 1])
            o_ref[:, h * hw:(h + 1) * hw] = (_rms(o, gain_ref[...]) * (1.0 - lambda_init)).astype(BF16)


def sample_diff_attention(q, k_new, v_new, cache_k, cache_v, table, lam_params, gain, lambda_init):
    batch, past, d = cache_k.shape
    ts = q.shape[0] // batch
    hw = 2 * DA_DH
    nh = d // hw
    tk = min(CFG["sa_tk"], past)
    nkb = past // tk
    q_pos = past + np.arange(ts)
    blast = bias_tiles(table, _bucket_tile(q_pos, past - tk + np.arange(tk)))
    bnew = bias_tiles(table, _bucket_tile(q_pos, past + np.arange(128)))
    if nkb > 1:
        far = _bucket_tile(q_pos, np.arange(past - tk))
        far_bucket = int(far[0, 0])
        assert (far == far_bucket).all()
    else:
        far_bucket = 0
    return pl.pallas_call(
        functools.partial(_sample_attn_kernel, far_bucket=far_bucket, lambda_init=lambda_init),
        grid=(batch, nkb),
        in_specs=[pl.BlockSpec(memory_space=pltpu.SMEM),
                  pl.BlockSpec((ts, d), lambda b, j: (b, 0)),
                  pl.BlockSpec((1, tk, d), lambda b, j: (b, j, 0)),
                  pl.BlockSpec((1, tk, d), lambda b, j: (b, j, 0)),
                  pl.BlockSpec((ts, d), lambda b, j: (b, 0)),
                  pl.BlockSpec((ts, d), lambda b, j: (b, 0)),
                  pl.BlockSpec((nh, ts, tk), lambda b, j: (0, 0, 0)),
                  pl.BlockSpec((nh, ts, 128), lambda b, j: (0, 0, 0)),
                  pl.BlockSpec((4, DA_DH), lambda b, j: (0, 0)),
                  pl.BlockSpec((1, hw), lambda b, j: (0, 0))],
        out_specs=pl.BlockSpec((ts, d), lambda b, j: (b, 0)),
        out_shape=jax.ShapeDtypeStruct((batch * ts, d), BF16),
        scratch_shapes=[pltpu.VMEM((2 * nh, ts, 1), F32), pltpu.VMEM((2 * nh, ts, 1), F32),
                        pltpu.VMEM((2 * nh, ts, hw), F32)],
        compiler_params=_cp(("parallel", "arbitrary")),
        name="sample_diff_attention",
    )(table, q, cache_k, cache_v, k_new, v_new, blast, bnew, lam_params, gain.reshape(1, hw))


def kernel(x_prompt, x_sample, state_hgrn, state_pool, cache_k, cache_v, norm_mix, norm_ffn, norm_final, hgrn_w_q, hgrn_w_f, hgrn_w_i, hgrn_w_g, hgrn_w_o, hgrn_lb_logits, hgrn_norm_gain, pool_w, pool_scale, attn_w_q, attn_w_k, attn_w_v, attn_w_o, attn_lambda_q1, attn_lambda_k1, attn_lambda_q2, attn_lambda_k2, attn_subln_gain, rel_bias_table, ffn_w_gate, ffn_w_up, ffn_w_down):
    bp, tp, d = x_prompt.shape
    bs, ts, _ = x_sample.shape
    past = cache_k.shape[2]
    depth = norm_mix.shape[0]
    assert bp == 1
    bf = lambda a: a.astype(BF16)
    xs = [x_prompt.reshape(bp * tp, d), x_sample.reshape(bs * ts, d)]
    dims = [(bp, tp), (bs, ts)]
    hg, pool_st, k_out, v_out = [[], []], [[], []], [[], []], [[], []]

    for i in range(depth):
        m, j = i % N_MIXERS, i // N_MIXERS
        if m == 0:
            w_cat = bf(jnp.concatenate([hgrn_w_q[j], hgrn_w_f[j], hgrn_w_i[j], hgrn_w_g[j]], axis=1))
            w_o = bf(hgrn_w_o[j])
            s0s = [jnp.zeros((bp, HG_HEADS, HG_DK, HG_DK), F32), state_hgrn[j]]
            for r in range(2):
                b, t = dims[r]
                proj, = norm_mm(xs[r], norm_mix[i], w_cat, [F32])
                og, st = hgrn_recurrence(proj, hgrn_lb_logits, hgrn_norm_gain[j], s0s[r], b, t, j)
                xs[r] = mm_res(og, w_o, xs[r])
                hg[r].append(st)
        elif m == 1:
            w_p = bf(pool_w[j])
            hists = [jnp.zeros((bp, POOL_HIST, d), F32), state_pool[j]]
            for r in range(2):
                b, t = dims[r]
                xs[r], st = pool_mixer(xs[r], norm_mix[i], hists[r], w_p, pool_scale[j], b, t, (0, past)[r])
                pool_st[r].append(st[:, 1:, :])
        else:
            lambda_init = 0.8 - 0.6 * math.exp(-0.3 * i)
            w_q, w_k, w_v, w_o = bf(attn_w_q[j]), bf(attn_w_k[j]), bf(attn_w_v[j]), bf(attn_w_o[j])
            lam_params = jnp.stack([attn_lambda_q1[j], attn_lambda_k1[j], attn_lambda_q2[j], attn_lambda_k2[j]])
            for r in range(2):
                b, t = dims[r]
                q, = norm_mm(xs[r], norm_mix[i], w_q, [BF16])
                k32, k16 = norm_mm(xs[r], norm_mix[i], w_k, [F32, BF16])
                v32, v16 = norm_mm(xs[r], norm_mix[i], w_v, [F32, BF16])
                if r == 0:
                    o = flash_diff_attention(q, k16, v16, rel_bias_table, lam_params, attn_subln_gain[j],
                                             lambda_init)
                else:
                    o = sample_diff_attention(q, k16, v16, cache_k[j].reshape(bs, past, d),
                                              cache_v[j].reshape(bs, past, d), rel_bias_table, lam_params,
                                              attn_subln_gain[j], lambda_init)
                xs[r] = mm_res(o, w_o, xs[r])
                k_out[r].append(k32.reshape(b, t, 2 * DA_HEADS, DA_DH))
                v_out[r].append(v32.reshape(b, t, DA_HEADS, 2 * DA_DH))
        w_g, w_u, w_d = bf(ffn_w_gate[i]), bf(ffn_w_up[i]), bf(ffn_w_down[i])
        g_final = norm_final if i == depth - 1 else None
        for r in range(2):
            xs[r] = ffn(xs[r], norm_ffn[i], w_g, w_u, w_d, g_final)

    return (xs[0].reshape(bp, tp, d), xs[1].reshape(bs, ts, d),
            jnp.stack(hg[0]), jnp.stack(hg[1]), jnp.stack(pool_st[0]), jnp.stack(pool_st[1]),
            jnp.stack(k_out[0]), jnp.stack(v_out[0]), jnp.stack(k_out[1]), jnp.stack(v_out[1]))
```

```python
import functools
import math

import numpy as np
import jax
import jax.numpy as jnp
from jax import lax
from jax.experimental import pallas as pl
from jax.experimental.pallas import tpu as pltpu

F32 = jnp.float32
BF16 = jnp.bfloat16

EPS = 1e-6
LOG2E = math.log2(math.e)
F_MIN = 1e-6
MASK_VALUE = -1e30
HG_HEADS = 16
HG_DK = 128
POOL_WINDOWS = (2, 4, 8, 16)
POOL_HIST = 15
DA_HEADS = 8
DA_DH = 128
ATT_CHUNK = 64
REL_BUCKETS = 32
REL_MAX_DIST = 128
N_MIXERS = 3

CFG = dict(
    mm_tm=1024, mm_tn=1024,
    ffn_tm=512, ffn_tf=512,
    pool_tm=512,
    hg_chunk=256, hg_rows=512, hg_heads=4, hg_unroll=2,
    fa_tq=512, fa_tk=512,
    sa_tk=512,
    vmem=56 * 1024 * 1024,
)


def _cp(sem):
    return pltpu.CompilerParams(dimension_semantics=sem, vmem_limit_bytes=CFG["vmem"])


def _rms(x, g):
    return x * lax.rsqrt(jnp.mean(x * x, axis=-1, keepdims=True) + EPS) * g


def _dot(a, b):
    return jnp.dot(a, b, preferred_element_type=F32)


def _dot_nt(a, b):
    return lax.dot_general(a, b, (((1,), (1,)), ((), ())), preferred_element_type=F32)


def _norm_mm_kernel(x_ref, g_ref, w_ref, *rest, out_scale, transposed):
    outs, xn_ref = rest[:-1], rest[-1]

    @pl.when(pl.program_id(1) == 0)
    def _():
        xn_ref[...] = _rms(x_ref[...], g_ref[...]).astype(BF16)

    y = _dot(xn_ref[...], w_ref[...])
    if out_scale is not None:
        y = y * out_scale
    for o, tr in zip(outs, transposed):
        if tr:
            o[0] = y.T.astype(o.dtype)
        else:
            o[...] = y.astype(o.dtype)


def norm_mm(x, g, w, out_dtypes, out_scale=None, transposed=None, tm=None):
    m, d = x.shape
    n = w.shape[1]
    tm, tn = min(tm or CFG["mm_tm"], m), min(CFG["mm_tn"], n)
    transposed = transposed or [False] * len(out_dtypes)
    out_specs = [pl.BlockSpec((1, tn, tm), lambda i, j: (i, j, 0)) if tr
                 else pl.BlockSpec((tm, tn), lambda i, j: (i, j)) for tr in transposed]
    out_shape = [jax.ShapeDtypeStruct((m // tm, n, tm) if tr else (m, n), dt)
                 for dt, tr in zip(out_dtypes, transposed)]
    return pl.pallas_call(
        functools.partial(_norm_mm_kernel, out_scale=out_scale, transposed=tuple(transposed)),
        grid=(m // tm, n // tn),
        in_specs=[pl.BlockSpec((tm, d), lambda i, j: (i, 0)),
                  pl.BlockSpec((1, d), lambda i, j: (0, 0)),
                  pl.BlockSpec((d, tn), lambda i, j: (0, j))],
        out_specs=out_specs,
        out_shape=out_shape,
        scratch_shapes=[pltpu.VMEM((tm, d), BF16)],
        compiler_params=_cp(("parallel", "arbitrary")),
        name="norm_mm",
    )(x, g.reshape(1, d), w)


def _mm_res_kernel(a_ref, w_ref, r_ref, o_ref):
    o_ref[...] = r_ref[...] + _dot(a_ref[...], w_ref[...])


def mm_res(a, w, r):
    m, k = a.shape
    n = w.shape[1]
    tm, tn = min(CFG["mm_tm"], m), min(CFG["mm_tn"], n)
    return pl.pallas_call(
        _mm_res_kernel,
        grid=(m // tm, n // tn),
        in_specs=[pl.BlockSpec((tm, k), lambda i, j: (i, 0)),
                  pl.BlockSpec((k, tn), lambda i, j: (0, j)),
                  pl.BlockSpec((tm, tn), lambda i, j: (i, j))],
        out_specs=pl.BlockSpec((tm, tn), lambda i, j: (i, j)),
        out_shape=jax.ShapeDtypeStruct((m, n), F32),
        compiler_params=_cp(("parallel", "arbitrary")),
        name="mm_res",
    )(a, w, r)


def _ffn_kernel(x_ref, g_ref, wg_ref, wu_ref, wd_ref, gf_ref, o_ref, xn_ref, *, final_norm):
    j = pl.program_id(1)

    @pl.when(j == 0)
    def _():
        x = x_ref[...]
        xn_ref[...] = _rms(x, g_ref[...]).astype(BF16)
        o_ref[...] = x

    xn = xn_ref[...]
    a = _dot(xn, wg_ref[...])
    b = _dot(xn, wu_ref[...])
    h = (a * jax.nn.sigmoid(a) * b).astype(BF16)
    o_ref[...] += _dot(h, wd_ref[...])

    if final_norm:
        @pl.when(j == pl.num_programs(1) - 1)
        def _():
            o_ref[...] = _rms(o_ref[...], gf_ref[...])


def ffn(x, g, wg, wu, wd, g_final=None):
    m, d = x.shape
    f = wg.shape[1]
    tm, tf = min(CFG["ffn_tm"], m), min(CFG["ffn_tf"], f)
    final_norm = g_final is not None
    gf = (g_final if final_norm else g).reshape(1, d)
    return pl.pallas_call(
        functools.partial(_ffn_kernel, final_norm=final_norm),
        grid=(m // tm, f // tf),
        in_specs=[pl.BlockSpec((tm, d), lambda i, j: (i, 0)),
                  pl.BlockSpec((1, d), lambda i, j: (0, 0)),
                  pl.BlockSpec((d, tf), lambda i, j: (0, j)),
                  pl.BlockSpec((d, tf), lambda i, j: (0, j)),
                  pl.BlockSpec((tf, d), lambda i, j: (j, 0)),
                  pl.BlockSpec((1, d), lambda i, j: (0, 0))],
        out_specs=pl.BlockSpec((tm, d), lambda i, j: (i, 0)),
        out_shape=jax.ShapeDtypeStruct((m, d), F32),
        scratch_shapes=[pltpu.VMEM((tm, d), BF16)],
        compiler_params=_cp(("parallel", "arbitrary")),
        name="ffn",
    )(x, g.reshape(1, d), wg, wu, wd, gf)


def _hgrn_levels(c):
    lv, h = [], c // 2
    while h >= 8:
        lv.append(h)
        h //= 2
    return lv


def _bcast_rows(g, rows, rep):
    return jnp.concatenate([jnp.broadcast_to(g[r:r + 1, :], (rep, g.shape[1])) for r in rows], axis=0)


def _hgrn_kernel(q_ref, f_ref, v_ref, gt_ref, lbl_ref, gain_ref, s0_ref, l_ref, og_ref, st_ref, stt_ref,
                 *, c, n_sub, hb, layer):
    cb = pl.program_id(2)
    dk = HG_DK

    @pl.when(cb == 0)
    def _():
        for h in range(hb):
            stt_ref[h] = s0_ref[0, h].T

    lg = lbl_ref[...]
    e = jnp.exp(lg - jnp.max(lg, axis=0, keepdims=True))
    p = e / jnp.sum(e, axis=0, keepdims=True)
    lb = p[0:1, :]
    for i in range(1, layer + 1):
        lb = lb + p[i:i + 1, :]
    lb = jnp.maximum(lb - p[0:1, :], 0.0)

    ti = lax.broadcasted_iota(jnp.int32, (c, c), 0)
    si = lax.broadcasted_iota(jnp.int32, (c, c), 1)
    levels = _hgrn_levels(c)
    masks = []
    for hh in levels:
        sh = int(math.log2(2 * hh))
        masks.append(((ti >> sh) == (si >> sh)) & ((ti & (2 * hh - 1)) >= hh) & ((si & (2 * hh - 1)) < hh))
    mask_loc = ((ti >> 3) == (si >> 3)) & (si <= ti)
    lmat = l_ref[...]
    gain = gain_ref[...]

    def chunk(ci, carry):
        r0 = pl.multiple_of(ci * c, c)
        fp = f_ref[pl.ds(r0, c), :]
        f = lb + (1.0 - lb) * jax.nn.sigmoid(fp)
        g = jnp.log2(jnp.maximum(f, F_MIN))
        kk = 1.0 - f
        g_hi = g.astype(BF16)
        r1 = g - g_hi.astype(F32)
        g_mid = r1.astype(BF16)
        g_lo = (r1 - g_mid.astype(F32)).astype(BF16)
        gc = _dot(lmat, g_hi) + _dot(lmat, g_mid) + _dot(lmat, g_lo)
        qa = q_ref[pl.ds(r0, c), :]
        va = v_ref[pl.ds(r0, c), :]
        ga = gt_ref[pl.ds(r0, c), :]
        for h in range(hb):
            sl = slice(h * dk, (h + 1) * dk)
            gh, qh, kh = gc[:, sl], qa[:, sl], kk[:, sl]
            vh = va[:, sl].astype(BF16)
            stt = stt_ref[h]
            inter = _dot_nt((qh * jnp.exp2(gh)).astype(BF16), stt.astype(BF16))
            g_last = gh[c - 1:c, :]
            k_dec = (kh * jnp.exp2(g_last - gh)).astype(BF16)
            stt_ref[h] = stt * jnp.exp2(g_last) + lax.dot_general(
                vh, k_dec, (((0,), (0,)), ((), ())), preferred_element_type=F32)
            q16, k16 = qh.astype(BF16), kh.astype(BF16)
            g_loc = _bcast_rows(gh, [8 * b + 3 for b in range(c // 8)], 8)
            sc = jnp.where(mask_loc,
                           _dot_nt(q16 * jnp.exp2(gh - g_loc).astype(BF16),
                                   k16 * jnp.exp2(g_loc - gh).astype(BF16)), 0.0)
            for hh, mk in zip(levels, masks):
                g_mid_rows = _bcast_rows(gh, [b * 2 * hh + hh - 1 for b in range(c // (2 * hh))], 2 * hh)
                fac = jnp.exp2(-jnp.abs(gh - g_mid_rows)).astype(BF16)
                sc = jnp.where(mk, _dot_nt(q16 * fac, k16 * fac), sc)
            o = inter + _dot(sc.astype(BF16), vh)
            on = _rms(o, gain)
            gate = ga[:, sl]
            og_ref[pl.ds(r0, c), sl] = (on * (gate * jax.nn.sigmoid(gate))).astype(BF16)
        return carry

    lax.fori_loop(0, n_sub, chunk, 0, unroll=min(CFG["hg_unroll"], n_sub))

    @pl.when(cb == pl.num_programs(2) - 1)
    def _():
        for h in range(hb):
            st_ref[0, h] = stt_ref[h].T


def hgrn_recurrence(proj, lb_logits, gain, s0, batch, t, layer):
    d = HG_HEADS * HG_DK
    c = min(CFG["hg_chunk"], t)
    rows = min(CFG["hg_rows"], t)
    hb = CFG["hg_heads"]
    w = hb * HG_DK
    ncb = t // rows
    n_layers = lb_logits.shape[0]
    lmat = jnp.asarray(np.tril(np.ones((c, c), np.float32)), BF16)

    def col(sec):
        return lambda b, hg, cb: (b * ncb + cb, sec * (d // w) + hg)

    og, st = pl.pallas_call(
        functools.partial(_hgrn_kernel, c=c, n_sub=rows // c, hb=hb, layer=layer),
        grid=(batch, HG_HEADS // hb, ncb),
        in_specs=[pl.BlockSpec((rows, w), col(0)),
                  pl.BlockSpec((rows, w), col(1)),
                  pl.BlockSpec((rows, w), col(2)),
                  pl.BlockSpec((rows, w), col(3)),
                  pl.BlockSpec((n_layers, w), lambda b, hg, cb: (0, hg)),
                  pl.BlockSpec((1, HG_DK), lambda b, hg, cb: (0, 0)),
                  pl.BlockSpec((1, hb, HG_DK, HG_DK), lambda b, hg, cb: (b, hg, 0, 0)),
                  pl.BlockSpec((c, c), lambda b, hg, cb: (0, 0))],
        out_specs=[pl.BlockSpec((rows, w), lambda b, hg, cb: (b * ncb + cb, hg)),
                   pl.BlockSpec((1, hb, HG_DK, HG_DK), lambda b, hg, cb: (b, hg, 0, 0))],
        out_shape=[jax.ShapeDtypeStruct((batch * t, d), BF16),
                   jax.ShapeDtypeStruct((batch, HG_HEADS, HG_DK, HG_DK), F32)],
        scratch_shapes=[pltpu.VMEM((hb, HG_DK, HG_DK), F32)],
        compiler_params=_cp(("parallel", "parallel", "arbitrary")),
        name="hgrn_recurrence",
    )(proj, proj, proj, proj, lb_logits, gain.reshape(1, HG_DK), s0, lmat)
    return og, st


def _pool_kernel(x_ref, g_ref, hist_ref, w_ref, sc_ref, o_ref, st_ref, buf_ref, *, tm, pos0):
    tb = pl.program_id(1)
    x = x_ref[...]
    xn = _rms(x, g_ref[...])
    hp = POOL_HIST + 1

    @pl.when(tb == 0)
    def _():
        buf_ref[0:hp, :] = hist_ref[0]

    @pl.when(tb > 0)
    def _():
        buf_ref[0:hp, :] = buf_ref[tm:tm + hp, :]

    buf_ref[hp:hp + tm, :] = xn
    st_ref[0] = buf_ref[tm:tm + hp, :]
    pos = pos0 + tb * tm + lax.broadcasted_iota(jnp.int32, (tm, 1), 0)
    gc = x.shape[1] // len(POOL_WINDOWS)
    for gi, win in enumerate(POOL_WINDOWS):
        cs = slice(gi * gc, (gi + 1) * gc)
        ws = xn[:, cs]
        for sft in range(1, win):
            ws = ws + buf_ref[hp - sft:hp - sft + tm, cs]
        cnt = jnp.minimum(win, pos + 1).astype(F32)
        pooled = (ws / cnt - xn[:, cs]).astype(BF16)
        o_ref[:, cs] = x[:, cs] + _dot(pooled, w_ref[gi]) * sc_ref[:, cs]


def pool_mixer(x, g, hist, w, scale, batch, t, pos0):
    d = x.shape[1]
    tm = min(CFG["pool_tm"], t)
    nb = t // tm
    hp = POOL_HIST + 1
    hist16 = jnp.concatenate([jnp.zeros((batch, 1, d), F32), hist], axis=1)
    ng, gc = w.shape[0], w.shape[1]
    return pl.pallas_call(
        functools.partial(_pool_kernel, tm=tm, pos0=pos0),
        grid=(batch, nb),
        in_specs=[pl.BlockSpec((tm, d), lambda b, i: (b * nb + i, 0)),
                  pl.BlockSpec((1, d), lambda b, i: (0, 0)),
                  pl.BlockSpec((1, hp, d), lambda b, i: (b, 0, 0)),
                  pl.BlockSpec((ng, gc, gc), lambda b, i: (0, 0, 0)),
                  pl.BlockSpec((1, d), lambda b, i: (0, 0))],
        out_specs=[pl.BlockSpec((tm, d), lambda b, i: (b * nb + i, 0)),
                   pl.BlockSpec((1, hp, d), lambda b, i: (b, 0, 0))],
        out_shape=[jax.ShapeDtypeStruct(x.shape, F32),
                   jax.ShapeDtypeStruct((batch, hp, d), F32)],
        scratch_shapes=[pltpu.VMEM((tm + hp, d), F32)],
        compiler_params=_cp(("parallel", "arbitrary")),
        name="pool_mixer",
    )(x, g.reshape(1, d), hist16, w, scale.reshape(1, d))


def _rel_bucket_np(rel):
    nb = REL_BUCKETS // 2
    max_exact = nb // 2
    n = np.abs(rel)
    large = max_exact + (np.log(np.maximum(n, max_exact).astype(np.float64) / max_exact)
                         / math.log(REL_MAX_DIST / max_exact) * (nb - max_exact)).astype(np.int64)
    large = np.minimum(large, nb - 1)
    return (np.where(rel > 0, nb, 0) + np.where(n < max_exact, n, large)).astype(np.int32)


def _bucket_tile(q_pos, k_pos):
    q_pos, k_pos = np.asarray(q_pos)[:, None], np.asarray(k_pos)[None, :]
    b = _rel_bucket_np(k_pos - q_pos)
    return np.where((k_pos // ATT_CHUNK) <= (q_pos // ATT_CHUNK), b, -1).astype(np.int32)


def _bias_kernel(tab_ref, bk_ref, o_ref):
    h = pl.program_id(0)
    bk = bk_ref[...]
    bias = jnp.full(bk.shape, MASK_VALUE, F32)
    for b in range(REL_BUCKETS):
        bias = jnp.where(bk == b, tab_ref[b, h] * LOG2E, bias)
    o_ref[0] = bias


def bias_tiles(table, bucket):
    r, c = bucket.shape
    nh = table.shape[1]
    return pl.pallas_call(
        _bias_kernel,
        grid=(nh,),
        in_specs=[pl.BlockSpec(memory_space=pltpu.SMEM),
                  pl.BlockSpec((r, c), lambda h: (0, 0))],
        out_specs=pl.BlockSpec((1, r, c), lambda h: (h, 0, 0)),
        out_shape=jax.ShapeDtypeStruct((nh, r, c), F32),
        compiler_params=_cp(("arbitrary",)),
        name="bias_tiles",
    )(table, jnp.asarray(bucket))


def _lambda(lq1, lk1, lq2, lk2, lambda_init):
    return (jnp.exp(jnp.sum(lq1 * lk1, axis=-1, keepdims=True))
            - jnp.exp(jnp.sum(lq2 * lk2, axis=-1, keepdims=True)) + lambda_init)


def _softmax_step(q, k, v, bias, m_ref, l_ref, acc_ref, idx):
    s = _dot_nt(q, k) + bias
    m_prev = m_ref[idx]
    m_new = jnp.maximum(m_prev, jnp.max(s, axis=-1, keepdims=True))
    p = jnp.exp2(s - m_new)
    alpha = jnp.exp2(m_prev - m_new)
    l_ref[idx] = alpha * l_ref[idx] + jnp.sum(p, axis=-1, keepdims=True)
    acc_ref[idx] = alpha * acc_ref[idx] + _dot(p.astype(BF16), v)
    m_ref[idx] = m_new


def _flash_kernel(q_ref, k_ref, vt_ref, bias_ref, lam_ref, gain_ref, o_ref,
                  m_ref, l_ref, acc_ref, sa_ref, sb_ref, *, tq, tk, lambda_init):
    i = pl.program_id(1)
    dh = DA_DH
    m_ref[...] = jnp.full(m_ref.shape, -jnp.inf, F32)
    l_ref[...] = jnp.zeros(l_ref.shape, F32)
    acc_ref[...] = jnp.zeros(acc_ref.shape, F32)
    q = q_ref[...]

    nkb = vt_ref.shape[0]

    def scores(j, s_ref):
        jc = jnp.minimum(j, nkb - 1)
        kb = k_ref[pl.ds(pl.multiple_of(jc * tk, tk), tk), :]
        for c in range(2):
            s_ref[c] = _dot_nt(kb[:, c * dh:(c + 1) * dh], q[:, c * dh:(c + 1) * dh])

    def update(j, s_ref):
        bias = bias_ref[0, jnp.clip(j - (i - 2), 0, 3)]
        vt = vt_ref[jnp.minimum(j, nkb - 1)]
        for c in range(2):
            s = s_ref[c] + bias
            m_prev = m_ref[c]
            m_new = jnp.maximum(m_prev, jnp.max(s, axis=0, keepdims=True))
            p = jnp.exp2(s - m_new)
            alpha = jnp.exp2(m_prev - m_new)
            l_ref[c] = alpha * l_ref[c] + jnp.sum(p, axis=0, keepdims=True)
            acc_ref[c] = alpha * acc_ref[c] + _dot(vt, p.astype(BF16))
            m_ref[c] = m_new

    scores(0, sa_ref)

    def pair(t, carry):
        j = 2 * t
        scores(j + 1, sb_ref)
        update(j, sa_ref)
        scores(j + 2, sa_ref)
        update(j + 1, sb_ref)
        return carry

    lax.fori_loop(0, (i + 2) // 2, pair, 0)

    lam = _lambda(lam_ref[0:1, :], lam_ref[1:2, :], lam_ref[2:3, :], lam_ref[3:4, :], lambda_init)
    o = acc_ref[0] / l_ref[0] - lam * (acc_ref[1] / l_ref[1])
    on = o * lax.rsqrt(jnp.mean(o * o, axis=0, keepdims=True) + EPS) * (1.0 - lambda_init)
    o_ref[...] = (on.T * gain_ref[...]).astype(BF16)


def flash_diff_attention(q, k, vt, table, lam_params, gain, lambda_init):
    t, d = q.shape
    tq, tk = min(CFG["fa_tq"], t), min(CFG["fa_tk"], t)
    assert tq == tk and tk % ATT_CHUNK == 0 and vt.shape == (t // tk, d, tk)
    hw = 2 * DA_DH
    nh = d // hw
    far = _rel_bucket_np(-np.arange(tk + 1, max(t, tk + 2)))
    far_bucket = int(far[0])
    assert (far == far_bucket).all()
    near = _bucket_tile(tk + np.arange(tq), np.arange(2 * tk)).T
    bucket = np.concatenate([np.full((tk, tq), far_bucket, np.int32), near,
                             np.full((tk, tq), -1, np.int32)], axis=0)
    bank = bias_tiles(table, bucket).reshape(nh, 4, tk, tq)
    once = pl.Buffered(1)
    return pl.pallas_call(
        functools.partial(_flash_kernel, tq=tq, tk=tk, lambda_init=lambda_init),
        grid=(nh, t // tq),
        in_specs=[pl.BlockSpec((tq, hw), lambda h, i: (i, h)),
                  pl.BlockSpec((t, hw), lambda h, i: (0, h), pipeline_mode=once),
                  pl.BlockSpec((t // tk, hw, tk), lambda h, i: (0, h, 0), pipeline_mode=once),
                  pl.BlockSpec((1, 4, tk, tq), lambda h, i: (h, 0, 0, 0), pipeline_mode=once),
                  pl.BlockSpec((4, DA_DH), lambda h, i: (0, 0)),
                  pl.BlockSpec((1, hw), lambda h, i: (0, 0))],
        out_specs=pl.BlockSpec((tq, hw), lambda h, i: (i, h)),
        out_shape=jax.ShapeDtypeStruct((t, d), BF16),
        scratch_shapes=[pltpu.VMEM((2, 1, tq), F32), pltpu.VMEM((2, 1, tq), F32),
                        pltpu.VMEM((2, hw, tq), F32),
                        pltpu.VMEM((2, tk, tq), F32), pltpu.VMEM((2, tk, tq), F32)],
        compiler_params=_cp(("parallel", "arbitrary")),
        name="flash_diff_attention",
    )(q, k, vt, bank, lam_params, gain.reshape(1, hw))


def _sample_attn_kernel(tab_ref, q_ref, ck_ref, cv_ref, kn_ref, vn_ref, blast_ref, bnew_ref, lam_ref, gain_ref,
                        o_ref, m_ref, l_ref, acc_ref, *, far_bucket, lambda_init):
    j = pl.program_id(1)
    nj = pl.num_programs(1)
    dh = DA_DH
    hw = 2 * dh
    nh = q_ref.shape[1] // hw
    ts = q_ref.shape[0]

    @pl.when(j == 0)
    def _():
        m_ref[...] = jnp.full(m_ref.shape, -jnp.inf, F32)
        l_ref[...] = jnp.zeros(l_ref.shape, F32)
        acc_ref[...] = jnp.zeros(acc_ref.shape, F32)

    q = q_ref[...]

    def block(k_of, v_of, bias_fn):
        for h in range(nh):
            vb = v_of(h)
            for c in range(2):
                cs = slice((2 * h + c) * dh, (2 * h + c + 1) * dh)
                _softmax_step(q[:, cs], k_of(2 * h + c), vb, bias_fn(h), m_ref, l_ref, acc_ref, 2 * h + c)

    kb = ck_ref[0].astype(BF16)
    vb = cv_ref[0].astype(BF16)
    cache_k_of = lambda hc: kb[:, hc * dh:(hc + 1) * dh]
    cache_v_of = lambda h: vb[:, h * hw:(h + 1) * hw]

    @pl.when(j < nj - 1)
    def _():
        block(cache_k_of, cache_v_of, lambda h: tab_ref[far_bucket, h] * LOG2E)

    @pl.when(j == nj - 1)
    def _():
        block(cache_k_of, cache_v_of, lambda h: blast_ref[h])
        block(lambda hc: kn_ref[:, hc * dh:(hc + 1) * dh], lambda h: vn_ref[:, h * hw:(h + 1) * hw],
              lambda h: bnew_ref[h][:, :ts])
        lam = _lambda(lam_ref[0:1, :], lam_ref[1:2, :], lam_ref[2:3, :], lam_ref[3:4, :], lambda_init)
        for h in range(nh):
            o = acc_ref[2 * h] / l_ref[2 * h] - lam * (acc_ref[2 * h + 1] / l_ref[2 * h + 1])
            o_ref[:, h * hw:(h + 1) * hw] = (_rms(o, gain_ref[...]) * (1.0 - lambda_init)).astype(BF16)


def sample_diff_attention(q, k_new, v_new, cache_k, cache_v, layer, table, lam_params, gain, lambda_init):
    _, batch, past, nhc, dh = cache_k.shape
    d = nhc * dh
    ts = q.shape[0] // batch
    hw = 2 * DA_DH
    nh = d // hw
    tk = min(CFG["sa_tk"], past)
    nkb = past // tk
    q_pos = past + np.arange(ts)
    blast = bias_tiles(table, _bucket_tile(q_pos, past - tk + np.arange(tk)))
    bnew = bias_tiles(table, _bucket_tile(q_pos, past + np.arange(128)))
    if nkb > 1:
        far = _bucket_tile(q_pos, np.arange(past - tk))
        far_bucket = int(far[0, 0])
        assert (far == far_bucket).all()
    else:
        far_bucket = 0
    return pl.pallas_call(
        functools.partial(_sample_attn_kernel, far_bucket=far_bucket, lambda_init=lambda_init),
        grid=(batch, nkb),
        in_specs=[pl.BlockSpec(memory_space=pltpu.SMEM),
                  pl.BlockSpec((ts, d), lambda b, j: (b, 0)),
                  pl.BlockSpec((1, tk, d), lambda b, j: (b, j, 0)),
                  pl.BlockSpec((1, tk, d), lambda b, j: (b, j, 0)),
                  pl.BlockSpec((ts, d), lambda b, j: (b, 0)),
                  pl.BlockSpec((ts, d), lambda b, j: (b, 0)),
                  pl.BlockSpec((nh, ts, tk), lambda b, j: (0, 0, 0)),
                  pl.BlockSpec((nh, ts, 128), lambda b, j: (0, 0, 0)),
                  pl.BlockSpec((4, DA_DH), lambda b, j: (0, 0)),
                  pl.BlockSpec((1, hw), lambda b, j: (0, 0))],
        out_specs=pl.BlockSpec((ts, d), lambda b, j: (b, 0)),
        out_shape=jax.ShapeDtypeStruct((batch * ts, d), BF16),
        scratch_shapes=[pltpu.VMEM((2 * nh, ts, 1), F32), pltpu.VMEM((2 * nh, ts, 1), F32),
                        pltpu.VMEM((2 * nh, ts, hw), F32)],
        compiler_params=_cp(("parallel", "arbitrary")),
        name="sample_diff_attention",
    )(table, q, cache_k[layer].reshape(batch, past, d), cache_v[layer].reshape(batch, past, d), k_new, v_new,
      blast, bnew, lam_params, gain.reshape(1, hw))


def kernel(x_prompt, x_sample, state_hgrn, state_pool, cache_k, cache_v, norm_mix, norm_ffn, norm_final, hgrn_w_q, hgrn_w_f, hgrn_w_i, hgrn_w_g, hgrn_w_o, hgrn_lb_logits, hgrn_norm_gain, pool_w, pool_scale, attn_w_q, attn_w_k, attn_w_v, attn_w_o, attn_lambda_q1, attn_lambda_k1, attn_lambda_q2, attn_lambda_k2, attn_subln_gain, rel_bias_table, ffn_w_gate, ffn_w_up, ffn_w_down):
    bp, tp, d = x_prompt.shape
    bs, ts, _ = x_sample.shape
    past = cache_k.shape[2]
    depth = norm_mix.shape[0]
    assert bp == 1
    bf = lambda a: a.astype(BF16)
    xs = [x_prompt.reshape(bp * tp, d), x_sample.reshape(bs * ts, d)]
    dims = [(bp, tp), (bs, ts)]
    hg, pool_st, k_out, v_out = [[], []], [[], []], [[], []], [[], []]

    for i in range(depth):
        m, j = i % N_MIXERS, i // N_MIXERS
        if m == 0:
            w_cat = bf(jnp.concatenate([hgrn_w_q[j], hgrn_w_f[j], hgrn_w_i[j], hgrn_w_g[j]], axis=1))
            w_o = bf(hgrn_w_o[j])
            s0s = [jnp.zeros((bp, HG_HEADS, HG_DK, HG_DK), F32), state_hgrn[j]]
            for r in range(2):
                b, t = dims[r]
                proj, = norm_mm(xs[r], norm_mix[i], w_cat, [F32])
                og, st = hgrn_recurrence(proj, hgrn_lb_logits, hgrn_norm_gain[j], s0s[r], b, t, j)
                xs[r] = mm_res(og, w_o, xs[r])
                hg[r].append(st)
        elif m == 1:
            w_p = bf(pool_w[j])
            hists = [jnp.zeros((bp, POOL_HIST, d), F32), state_pool[j]]
            for r in range(2):
                b, t = dims[r]
                xs[r], st = pool_mixer(xs[r], norm_mix[i], hists[r], w_p, pool_scale[j], b, t, (0, past)[r])
                pool_st[r].append(st[:, 1:, :])
        else:
            lambda_init = 0.8 - 0.6 * math.exp(-0.3 * i)
            w_q, w_k, w_v, w_o = bf(attn_w_q[j]), bf(attn_w_k[j]), bf(attn_w_v[j]), bf(attn_w_o[j])
            lam_params = jnp.stack([attn_lambda_q1[j], attn_lambda_k1[j], attn_lambda_q2[j], attn_lambda_k2[j]])
            for r in range(2):
                b, t = dims[r]
                tm = CFG["fa_tk"]
                q, = norm_mm(xs[r], norm_mix[i], w_q, [BF16], out_scale=DA_DH ** -0.5 * LOG2E, tm=tm)
                k32, k16 = norm_mm(xs[r], norm_mix[i], w_k, [F32, BF16], tm=tm)
                v32, v16 = norm_mm(xs[r], norm_mix[i], w_v, [F32, BF16], transposed=[False, r == 0], tm=tm)
                if r == 0:
                    o = flash_diff_attention(q, k16, v16, rel_bias_table, lam_params, attn_subln_gain[j],
                                             lambda_init)
                else:
                    o = sample_diff_attention(q, k16, v16, cache_k, cache_v, j, rel_bias_table, lam_params,
                                              attn_subln_gain[j], lambda_init)
                xs[r] = mm_res(o, w_o, xs[r])
                k_out[r].append(k32.reshape(b, t, 2 * DA_HEADS, DA_DH))
                v_out[r].append(v32.reshape(b, t, DA_HEADS, 2 * DA_DH))
        w_g, w_u, w_d = bf(ffn_w_gate[i]), bf(ffn_w_up[i]), bf(ffn_w_down[i])
        g_final = norm_final if i == depth - 1 else None
        for r in range(2):
            xs[r] = ffn(xs[r], norm_ffn[i], w_g, w_u, w_d, g_final)

    return (xs[0].reshape(bp, tp, d), xs[1].reshape(bs, ts, d),
            jnp.stack(hg[0]), jnp.stack(hg[1]), jnp.stack(pool_st[0]), jnp.stack(pool_st[1]),
            jnp.stack(k_out[0]), jnp.stack(v_out[0]), jnp.stack(k_out[1]), jnp.stack(v_out[1]))
```

```python
import functools
import math

import numpy as np
import jax
import jax.numpy as jnp
from jax import lax
from jax.experimental import pallas as pl
from jax.experimental.pallas import tpu as pltpu

F32 = jnp.float32
BF16 = jnp.bfloat16

EPS = 1e-6
LOG2E = math.log2(math.e)
F_MIN = 1e-6
MASK_VALUE = -1e30
HG_HEADS = 16
HG_DK = 128
POOL_WINDOWS = (2, 4, 8, 16)
POOL_HIST = 15
DA_HEADS = 8
DA_DH = 128
ATT_CHUNK = 64
REL_BUCKETS = 32
REL_MAX_DIST = 128
N_MIXERS = 3

CFG = dict(
    mm_tm=1024, mm_tn=1024,
    ffn_tm=512, ffn_tf=512,
    pool_tm=512,
    hg_chunk=256, hg_rows=512, hg_heads=4, hg_unroll=2,
    fa_tq=512, fa_tk=512,
    sa_tk=1024,
    vmem=56 * 1024 * 1024,
)


def _cp(sem):
    return pltpu.CompilerParams(dimension_semantics=sem, vmem_limit_bytes=CFG["vmem"])


def _rms(x, g):
    return x * lax.rsqrt(jnp.mean(x * x, axis=-1, keepdims=True) + EPS) * g


def _dot(a, b):
    return jnp.dot(a, b, preferred_element_type=F32)


def _dot_nt(a, b):
    return lax.dot_general(a, b, (((1,), (1,)), ((), ())), preferred_element_type=F32)


def _norm_mm_kernel(x_ref, g_ref, w_ref, *rest, out_scale, transposed):
    outs, xn_ref = rest[:-1], rest[-1]

    @pl.when(pl.program_id(1) == 0)
    def _():
        xn_ref[...] = _rms(x_ref[...], g_ref[...]).astype(BF16)

    y = _dot(xn_ref[...], w_ref[...])
    if out_scale is not None:
        y = y * out_scale
    for o, tr in zip(outs, transposed):
        if tr:
            o[0] = y.T.astype(o.dtype)
        else:
            o[...] = y.astype(o.dtype)


def norm_mm(x, g, w, out_dtypes, out_scale=None, transposed=None, tm=None):
    m, d = x.shape
    n = w.shape[1]
    tm, tn = min(tm or CFG["mm_tm"], m), min(CFG["mm_tn"], n)
    transposed = transposed or [False] * len(out_dtypes)
    out_specs = [pl.BlockSpec((1, tn, tm), lambda i, j: (i, j, 0)) if tr
                 else pl.BlockSpec((tm, tn), lambda i, j: (i, j)) for tr in transposed]
    out_shape = [jax.ShapeDtypeStruct((m // tm, n, tm) if tr else (m, n), dt)
                 for dt, tr in zip(out_dtypes, transposed)]
    return pl.pallas_call(
        functools.partial(_norm_mm_kernel, out_scale=out_scale, transposed=tuple(transposed)),
        grid=(m // tm, n // tn),
        in_specs=[pl.BlockSpec((tm, d), lambda i, j: (i, 0)),
                  pl.BlockSpec((1, d), lambda i, j: (0, 0)),
                  pl.BlockSpec((d, tn), lambda i, j: (0, j))],
        out_specs=out_specs,
        out_shape=out_shape,
        scratch_shapes=[pltpu.VMEM((tm, d), BF16)],
        compiler_params=_cp(("parallel", "arbitrary")),
        name="norm_mm",
    )(x, g.reshape(1, d), w)


def _mm_res_kernel(a_ref, w_ref, r_ref, o_ref):
    o_ref[...] = r_ref[...] + _dot(a_ref[...], w_ref[...])


def mm_res(a, w, r):
    m, k = a.shape
    n = w.shape[1]
    tm, tn = min(CFG["mm_tm"], m), min(CFG["mm_tn"], n)
    return pl.pallas_call(
        _mm_res_kernel,
        grid=(m // tm, n // tn),
        in_specs=[pl.BlockSpec((tm, k), lambda i, j: (i, 0)),
                  pl.BlockSpec((k, tn), lambda i, j: (0, j)),
                  pl.BlockSpec((tm, tn), lambda i, j: (i, j))],
        out_specs=pl.BlockSpec((tm, tn), lambda i, j: (i, j)),
        out_shape=jax.ShapeDtypeStruct((m, n), F32),
        compiler_params=_cp(("parallel", "arbitrary")),
        name="mm_res",
    )(a, w, r)


def _ffn_kernel(x_ref, g_ref, wg_ref, wu_ref, wd_ref, gf_ref, o_ref, xn_ref, *, final_norm):
    j = pl.program_id(1)

    @pl.when(j == 0)
    def _():
        x = x_ref[...]
        xn_ref[...] = _rms(x, g_ref[...]).astype(BF16)
        o_ref[...] = x

    xn = xn_ref[...]
    a = _dot(xn, wg_ref[...])
    b = _dot(xn, wu_ref[...])
    h = (a * jax.nn.sigmoid(a) * b).astype(BF16)
    o_ref[...] += _dot(h, wd_ref[...])

    if final_norm:
        @pl.when(j == pl.num_programs(1) - 1)
        def _():
            o_ref[...] = _rms(o_ref[...], gf_ref[...])


def ffn(x, g, wg, wu, wd, g_final=None):
    m, d = x.shape
    f = wg.shape[1]
    tm, tf = min(CFG["ffn_tm"], m), min(CFG["ffn_tf"], f)
    final_norm = g_final is not None
    gf = (g_final if final_norm else g).reshape(1, d)
    return pl.pallas_call(
        functools.partial(_ffn_kernel, final_norm=final_norm),
        grid=(m // tm, f // tf),
        in_specs=[pl.BlockSpec((tm, d), lambda i, j: (i, 0)),
                  pl.BlockSpec((1, d), lambda i, j: (0, 0)),
                  pl.BlockSpec((d, tf), lambda i, j: (0, j)),
                  pl.BlockSpec((d, tf), lambda i, j: (0, j)),
                  pl.BlockSpec((tf, d), lambda i, j: (j, 0)),
                  pl.BlockSpec((1, d), lambda i, j: (0, 0))],
        out_specs=pl.BlockSpec((tm, d), lambda i, j: (i, 0)),
        out_shape=jax.ShapeDtypeStruct((m, d), F32),
        scratch_shapes=[pltpu.VMEM((tm, d), BF16)],
        compiler_params=_cp(("parallel", "arbitrary")),
        name="ffn",
    )(x, g.reshape(1, d), wg, wu, wd, gf)


def _hgrn_levels(c):
    lv, h = [], c // 2
    while h >= 8:
        lv.append(h)
        h //= 2
    return lv


def _bcast_rows(g, rows, rep):
    return jnp.concatenate([jnp.broadcast_to(g[r:r + 1, :], (rep, g.shape[1])) for r in rows], axis=0)


def _hgrn_kernel(q_ref, f_ref, v_ref, gt_ref, lbl_ref, gain_ref, s0_ref, l_ref, og_ref, st_ref, stt_ref,
                 *, c, n_sub, hb, layer):
    cb = pl.program_id(2)
    dk = HG_DK

    @pl.when(cb == 0)
    def _():
        for h in range(hb):
            stt_ref[h] = s0_ref[0, h].T

    lg = lbl_ref[...]
    e = jnp.exp(lg - jnp.max(lg, axis=0, keepdims=True))
    p = e / jnp.sum(e, axis=0, keepdims=True)
    lb = p[0:1, :]
    for i in range(1, layer + 1):
        lb = lb + p[i:i + 1, :]
    lb = jnp.maximum(lb - p[0:1, :], 0.0)

    ti = lax.broadcasted_iota(jnp.int32, (c, c), 0)
    si = lax.broadcasted_iota(jnp.int32, (c, c), 1)
    levels = _hgrn_levels(c)
    masks = []
    for hh in levels:
        sh = int(math.log2(2 * hh))
        masks.append(((ti >> sh) == (si >> sh)) & ((ti & (2 * hh - 1)) >= hh) & ((si & (2 * hh - 1)) < hh))
    mask_loc = ((ti >> 3) == (si >> 3)) & (si <= ti)
    lmat = l_ref[...]
    gain = gain_ref[...]

    def chunk(ci, carry):
        r0 = pl.multiple_of(ci * c, c)
        fp = f_ref[pl.ds(r0, c), :]
        f = lb + (1.0 - lb) * jax.nn.sigmoid(fp)
        g = jnp.log2(jnp.maximum(f, F_MIN))
        kk = 1.0 - f
        g_hi = g.astype(BF16)
        r1 = g - g_hi.astype(F32)
        g_mid = r1.astype(BF16)
        g_lo = (r1 - g_mid.astype(F32)).astype(BF16)
        gc = _dot(lmat, g_hi) + _dot(lmat, g_mid) + _dot(lmat, g_lo)
        qa = q_ref[pl.ds(r0, c), :]
        va = v_ref[pl.ds(r0, c), :]
        ga = gt_ref[pl.ds(r0, c), :]
        for h in range(hb):
            sl = slice(h * dk, (h + 1) * dk)
            gh, qh, kh = gc[:, sl], qa[:, sl], kk[:, sl]
            vh = va[:, sl].astype(BF16)
            stt = stt_ref[h]
            inter = _dot_nt((qh * jnp.exp2(gh)).astype(BF16), stt.astype(BF16))
            g_last = gh[c - 1:c, :]
            k_dec = (kh * jnp.exp2(g_last - gh)).astype(BF16)
            stt_ref[h] = stt * jnp.exp2(g_last) + lax.dot_general(
                vh, k_dec, (((0,), (0,)), ((), ())), preferred_element_type=F32)
            q16, k16 = qh.astype(BF16), kh.astype(BF16)
            g_loc = _bcast_rows(gh, [8 * b + 3 for b in range(c // 8)], 8)
            sc = jnp.where(mask_loc,
                           _dot_nt(q16 * jnp.exp2(gh - g_loc).astype(BF16),
                                   k16 * jnp.exp2(g_loc - gh).astype(BF16)), 0.0)
            for hh, mk in zip(levels, masks):
                g_mid_rows = _bcast_rows(gh, [b * 2 * hh + hh - 1 for b in range(c // (2 * hh))], 2 * hh)
                fac = jnp.exp2(-jnp.abs(gh - g_mid_rows)).astype(BF16)
                sc = jnp.where(mk, _dot_nt(q16 * fac, k16 * fac), sc)
            o = inter + _dot(sc.astype(BF16), vh)
            on = _rms(o, gain)
            gate = ga[:, sl]
            og_ref[pl.ds(r0, c), sl] = (on * (gate * jax.nn.sigmoid(gate))).astype(BF16)
        return carry

    lax.fori_loop(0, n_sub, chunk, 0, unroll=min(CFG["hg_unroll"], n_sub))

    @pl.when(cb == pl.num_programs(2) - 1)
    def _():
        for h in range(hb):
            st_ref[0, h] = stt_ref[h].T


def hgrn_recurrence(proj, lb_logits, gain, s0, batch, t, layer):
    d = HG_HEADS * HG_DK
    c = min(CFG["hg_chunk"], t)
    rows = min(CFG["hg_rows"], t)
    hb = CFG["hg_heads"]
    w = hb * HG_DK
    ncb = t // rows
    n_layers = lb_logits.shape[0]
    lmat = jnp.asarray(np.tril(np.ones((c, c), np.float32)), BF16)

    def col(sec):
        return lambda b, hg, cb: (b * ncb + cb, sec * (d // w) + hg)

    og, st = pl.pallas_call(
        functools.partial(_hgrn_kernel, c=c, n_sub=rows // c, hb=hb, layer=layer),
        grid=(batch, HG_HEADS // hb, ncb),
        in_specs=[pl.BlockSpec((rows, w), col(0)),
                  pl.BlockSpec((rows, w), col(1)),
                  pl.BlockSpec((rows, w), col(2)),
                  pl.BlockSpec((rows, w), col(3)),
                  pl.BlockSpec((n_layers, w), lambda b, hg, cb: (0, hg)),
                  pl.BlockSpec((1, HG_DK), lambda b, hg, cb: (0, 0)),
                  pl.BlockSpec((1, hb, HG_DK, HG_DK), lambda b, hg, cb: (b, hg, 0, 0)),
                  pl.BlockSpec((c, c), lambda b, hg, cb: (0, 0))],
        out_specs=[pl.BlockSpec((rows, w), lambda b, hg, cb: (b * ncb + cb, hg)),
                   pl.BlockSpec((1, hb, HG_DK, HG_DK), lambda b, hg, cb: (b, hg, 0, 0))],
        out_shape=[jax.ShapeDtypeStruct((batch * t, d), BF16),
                   jax.ShapeDtypeStruct((batch, HG_HEADS, HG_DK, HG_DK), F32)],
        scratch_shapes=[pltpu.VMEM((hb, HG_DK, HG_DK), F32)],
        compiler_params=_cp(("parallel", "parallel", "arbitrary")),
        name="hgrn_recurrence",
    )(proj, proj, proj, proj, lb_logits, gain.reshape(1, HG_DK), s0, lmat)
    return og, st


def _pool_kernel(x_ref, g_ref, hist_ref, w_ref, sc_ref, o_ref, st_ref, buf_ref, *, tm, pos0):
    tb = pl.program_id(1)
    x = x_ref[...]
    xn = _rms(x, g_ref[...])
    hp = POOL_HIST + 1

    @pl.when(tb == 0)
    def _():
        buf_ref[0:hp, :] = hist_ref[0]

    @pl.when(tb > 0)
    def _():
        buf_ref[0:hp, :] = buf_ref[tm:tm + hp, :]

    buf_ref[hp:hp + tm, :] = xn
    st_ref[0] = buf_ref[tm:tm + hp, :]
    pos = pos0 + tb * tm + lax.broadcasted_iota(jnp.int32, (tm, 1), 0)
    gc = x.shape[1] // len(POOL_WINDOWS)
    for gi, win in enumerate(POOL_WINDOWS):
        cs = slice(gi * gc, (gi + 1) * gc)
        ws = xn[:, cs]
        for sft in range(1, win):
            ws = ws + buf_ref[hp - sft:hp - sft + tm, cs]
        cnt = jnp.minimum(win, pos + 1).astype(F32)
        pooled = (ws / cnt - xn[:, cs]).astype(BF16)
        o_ref[:, cs] = x[:, cs] + _dot(pooled, w_ref[gi]) * sc_ref[:, cs]


def pool_mixer(x, g, hist, w, scale, batch, t, pos0):
    d = x.shape[1]
    tm = min(CFG["pool_tm"], t)
    nb = t // tm
    hp = POOL_HIST + 1
    hist16 = jnp.concatenate([jnp.zeros((batch, 1, d), F32), hist], axis=1)
    ng, gc = w.shape[0], w.shape[1]
    return pl.pallas_call(
        functools.partial(_pool_kernel, tm=tm, pos0=pos0),
        grid=(batch, nb),
        in_specs=[pl.BlockSpec((tm, d), lambda b, i: (b * nb + i, 0)),
                  pl.BlockSpec((1, d), lambda b, i: (0, 0)),
                  pl.BlockSpec((1, hp, d), lambda b, i: (b, 0, 0)),
                  pl.BlockSpec((ng, gc, gc), lambda b, i: (0, 0, 0)),
                  pl.BlockSpec((1, d), lambda b, i: (0, 0))],
        out_specs=[pl.BlockSpec((tm, d), lambda b, i: (b * nb + i, 0)),
                   pl.BlockSpec((1, hp, d), lambda b, i: (b, 0, 0))],
        out_shape=[jax.ShapeDtypeStruct(x.shape, F32),
                   jax.ShapeDtypeStruct((batch, hp, d), F32)],
        scratch_shapes=[pltpu.VMEM((tm + hp, d), F32)],
        compiler_params=_cp(("parallel", "arbitrary")),
        name="pool_mixer",
    )(x, g.reshape(1, d), hist16, w, scale.reshape(1, d))


def _rel_bucket_np(rel):
    nb = REL_BUCKETS // 2
    max_exact = nb // 2
    n = np.abs(rel)
    large = max_exact + (np.log(np.maximum(n, max_exact).astype(np.float64) / max_exact)
                         / math.log(REL_MAX_DIST / max_exact) * (nb - max_exact)).astype(np.int64)
    large = np.minimum(large, nb - 1)
    return (np.where(rel > 0, nb, 0) + np.where(n < max_exact, n, large)).astype(np.int32)


def _bucket_tile(q_pos, k_pos):
    q_pos, k_pos = np.asarray(q_pos)[:, None], np.asarray(k_pos)[None, :]
    b = _rel_bucket_np(k_pos - q_pos)
    return np.where((k_pos // ATT_CHUNK) <= (q_pos // ATT_CHUNK), b, -1).astype(np.int32)


def _bias_kernel(tab_ref, bk_ref, o_ref):
    h = pl.program_id(0)
    bk = bk_ref[...]
    bias = jnp.full(bk.shape, MASK_VALUE, F32)
    for b in range(REL_BUCKETS):
        bias = jnp.where(bk == b, tab_ref[b, h] * LOG2E, bias)
    o_ref[0] = bias


def bias_tiles(table, bucket):
    r, c = bucket.shape
    nh = table.shape[1]
    return pl.pallas_call(
        _bias_kernel,
        grid=(nh,),
        in_specs=[pl.BlockSpec(memory_space=pltpu.SMEM),
                  pl.BlockSpec((r, c), lambda h: (0, 0))],
        out_specs=pl.BlockSpec((1, r, c), lambda h: (h, 0, 0)),
        out_shape=jax.ShapeDtypeStruct((nh, r, c), F32),
        compiler_params=_cp(("arbitrary",)),
        name="bias_tiles",
    )(table, jnp.asarray(bucket))


def _lambda(lq1, lk1, lq2, lk2, lambda_init):
    return (jnp.exp(jnp.sum(lq1 * lk1, axis=-1, keepdims=True))
            - jnp.exp(jnp.sum(lq2 * lk2, axis=-1, keepdims=True)) + lambda_init)


def _softmax_step(q, k, v, bias, m_ref, l_ref, acc_ref, idx):
    s = _dot_nt(q, k) + bias
    m_prev = m_ref[idx]
    m_new = jnp.maximum(m_prev, jnp.max(s, axis=-1, keepdims=True))
    p = jnp.exp2(s - m_new)
    alpha = jnp.exp2(m_prev - m_new)
    l_ref[idx] = alpha * l_ref[idx] + jnp.sum(p, axis=-1, keepdims=True)
    acc_ref[idx] = alpha * acc_ref[idx] + _dot(p.astype(BF16), v)
    m_ref[idx] = m_new


def _flash_kernel(q_ref, k_ref, vt_ref, bias_ref, lam_ref, gain_ref, o_ref,
                  m_ref, l_ref, acc_ref, sa_ref, sb_ref, pa_ref, pb_ref, ala_ref, alb_ref, *, tq, tk, lambda_init):
    i = pl.program_id(1)
    dh = DA_DH
    m_ref[...] = jnp.full(m_ref.shape, -jnp.inf, F32)
    l_ref[...] = jnp.zeros(l_ref.shape, F32)
    acc_ref[...] = jnp.zeros(acc_ref.shape, F32)
    q = q_ref[...]

    nkb = vt_ref.shape[0]

    def scores(j, s_ref):
        jc = jnp.minimum(j, nkb - 1)
        kb = k_ref[pl.ds(pl.multiple_of(jc * tk, tk), tk), :]
        for c in range(2):
            s_ref[c] = _dot_nt(kb[:, c * dh:(c + 1) * dh], q[:, c * dh:(c + 1) * dh])

    def softmax(j, s_ref, p_ref, al_ref):
        bias = bias_ref[0, jnp.clip(j - (i - 2), 0, 3)]
        for c in range(2):
            s = s_ref[c] + bias
            m_prev = m_ref[c]
            m_new = jnp.maximum(m_prev, jnp.max(s, axis=0, keepdims=True))
            p = jnp.exp2(s - m_new)
            alpha = jnp.exp2(m_prev - m_new)
            l_ref[c] = alpha * l_ref[c] + jnp.sum(p, axis=0, keepdims=True)
            m_ref[c] = m_new
            al_ref[c] = alpha
            p_ref[c] = p.astype(BF16)

    def values(j, p_ref, al_ref):
        vt = vt_ref[jnp.minimum(j, nkb - 1)]
        for c in range(2):
            acc_ref[c] = al_ref[c] * acc_ref[c] + _dot(vt, p_ref[c])

    scores(0, sa_ref)
    scores(1, sb_ref)
    softmax(0, sa_ref, pa_ref, ala_ref)

    def pair(t, carry):
        j = 2 * t
        scores(j + 2, sa_ref)
        softmax(j + 1, sb_ref, pb_ref, alb_ref)
        values(j, pa_ref, ala_ref)
        scores(j + 3, sb_ref)
        softmax(j + 2, sa_ref, pa_ref, ala_ref)
        values(j + 1, pb_ref, alb_ref)
        return carry

    lax.fori_loop(0, (i + 2) // 2, pair, 0)

    lam = _lambda(lam_ref[0:1, :], lam_ref[1:2, :], lam_ref[2:3, :], lam_ref[3:4, :], lambda_init)
    o = acc_ref[0] / l_ref[0] - lam * (acc_ref[1] / l_ref[1])
    on = o * lax.rsqrt(jnp.mean(o * o, axis=0, keepdims=True) + EPS) * (1.0 - lambda_init)
    o_ref[...] = (on.T * gain_ref[...]).astype(BF16)


def flash_diff_attention(q, k, vt, table, lam_params, gain, lambda_init):
    t, d = q.shape
    tq, tk = min(CFG["fa_tq"], t), min(CFG["fa_tk"], t)
    assert tq == tk and tk % ATT_CHUNK == 0 and vt.shape == (t // tk, d, tk)
    hw = 2 * DA_DH
    nh = d // hw
    far = _rel_bucket_np(-np.arange(tk + 1, max(t, tk + 2)))
    far_bucket = int(far[0])
    assert (far == far_bucket).all()
    near = _bucket_tile(tk + np.arange(tq), np.arange(2 * tk)).T
    bucket = np.concatenate([np.full((tk, tq), far_bucket, np.int32), near,
                             np.full((tk, tq), -1, np.int32)], axis=0)
    bank = bias_tiles(table, bucket).reshape(nh, 4, tk, tq)
    once = pl.Buffered(1)
    return pl.pallas_call(
        functools.partial(_flash_kernel, tq=tq, tk=tk, lambda_init=lambda_init),
        grid=(nh, t // tq),
        in_specs=[pl.BlockSpec((tq, hw), lambda h, i: (i, h)),
                  pl.BlockSpec((t, hw), lambda h, i: (0, h), pipeline_mode=once),
                  pl.BlockSpec((t // tk, hw, tk), lambda h, i: (0, h, 0), pipeline_mode=once),
                  pl.BlockSpec((1, 4, tk, tq), lambda h, i: (h, 0, 0, 0), pipeline_mode=once),
                  pl.BlockSpec((4, DA_DH), lambda h, i: (0, 0)),
                  pl.BlockSpec((1, hw), lambda h, i: (0, 0))],
        out_specs=pl.BlockSpec((tq, hw), lambda h, i: (i, h)),
        out_shape=jax.ShapeDtypeStruct((t, d), BF16),
        scratch_shapes=[pltpu.VMEM((2, 1, tq), F32), pltpu.VMEM((2, 1, tq), F32),
                        pltpu.VMEM((2, hw, tq), F32),
                        pltpu.VMEM((2, tk, tq), F32), pltpu.VMEM((2, tk, tq), F32),
                        pltpu.VMEM((2, tk, tq), BF16), pltpu.VMEM((2, tk, tq), BF16),
                        pltpu.VMEM((2, 1, tq), F32), pltpu.VMEM((2, 1, tq), F32)],
        compiler_params=_cp(("parallel", "arbitrary")),
        name="flash_diff_attention",
    )(q, k, vt, bank, lam_params, gain.reshape(1, hw))


def _sample_attn_kernel(tab_ref, q_ref, ck_ref, cv_ref, kn_ref, vn_ref, blast_ref, bnew_ref, lam_ref, gain_ref,
                        o_ref, m_ref, l_ref, acc_ref, *, far_bucket, lambda_init):
    j = pl.program_id(1)
    nj = pl.num_programs(1)
    dh = DA_DH
    hw = 2 * dh
    nh = q_ref.shape[1] // hw
    ts = q_ref.shape[0]

    @pl.when(j == 0)
    def _():
        m_ref[...] = jnp.full(m_ref.shape, -jnp.inf, F32)
        l_ref[...] = jnp.zeros(l_ref.shape, F32)
        acc_ref[...] = jnp.zeros(acc_ref.shape, F32)

    q = q_ref[...]

    def block(k_of, v_of, bias_fn):
        for h in range(nh):
            vb = v_of(h)
            for c in range(2):
                cs = slice((2 * h + c) * dh, (2 * h + c + 1) * dh)
                _softmax_step(q[:, cs], k_of(2 * h + c), vb, bias_fn(h), m_ref, l_ref, acc_ref, 2 * h + c)

    kb = ck_ref[0]
    vb = cv_ref[0]
    cache_k_of = lambda hc: kb[:, hc * dh:(hc + 1) * dh]
    cache_v_of = lambda h: vb[:, h * hw:(h + 1) * hw]

    @pl.when(j < nj - 1)
    def _():
        block(cache_k_of, cache_v_of, lambda h: tab_ref[far_bucket, h] * LOG2E)

    @pl.when(j == nj - 1)
    def _():
        block(cache_k_of, cache_v_of, lambda h: blast_ref[h])
        block(lambda hc: kn_ref[:, hc * dh:(hc + 1) * dh], lambda h: vn_ref[:, h * hw:(h + 1) * hw],
              lambda h: bnew_ref[h][:, :ts])
        lam = _lambda(lam_ref[0:1, :], lam_ref[1:2, :], lam_ref[2:3, :], lam_ref[3:4, :], lambda_init)
        for h in range(nh):
            o = acc_ref[2 * h] / l_ref[2 * h] - lam * (acc_ref[2 * h + 1] / l_ref[2 * h + 1])
            o_ref[:, h * hw:(h + 1) * hw] = (_rms(o, gain_ref[...]) * (1.0 - lambda_init)).astype(BF16)


def sample_diff_attention(q, k_new, v_new, cache_k, cache_v, layer, table, lam_params, gain, lambda_init):
    _, batch, past, nhc, dh = cache_k.shape
    d = nhc * dh
    ts = q.shape[0] // batch
    hw = 2 * DA_DH
    nh = d // hw
    tk = min(CFG["sa_tk"], past)
    nkb = past // tk
    q_pos = past + np.arange(ts)
    blast = bias_tiles(table, _bucket_tile(q_pos, past - tk + np.arange(tk)))
    bnew = bias_tiles(table, _bucket_tile(q_pos, past + np.arange(128)))
    if nkb > 1:
        far = _bucket_tile(q_pos, np.arange(past - tk))
        far_bucket = int(far[0, 0])
        assert (far == far_bucket).all()
    else:
        far_bucket = 0
    return pl.pallas_call(
        functools.partial(_sample_attn_kernel, far_bucket=far_bucket, lambda_init=lambda_init),
        grid=(batch, nkb),
        in_specs=[pl.BlockSpec(memory_space=pltpu.SMEM),
                  pl.BlockSpec((ts, d), lambda b, j: (b, 0)),
                  pl.BlockSpec((1, tk, d), lambda b, j: (b, j, 0)),
                  pl.BlockSpec((1, tk, d), lambda b, j: (b, j, 0)),
                  pl.BlockSpec((ts, d), lambda b, j: (b, 0)),
                  pl.BlockSpec((ts, d), lambda b, j: (b, 0)),
                  pl.BlockSpec((nh, ts, tk), lambda b, j: (0, 0, 0)),
                  pl.BlockSpec((nh, ts, 128), lambda b, j: (0, 0, 0)),
                  pl.BlockSpec((4, DA_DH), lambda b, j: (0, 0)),
                  pl.BlockSpec((1, hw), lambda b, j: (0, 0))],
        out_specs=pl.BlockSpec((ts, d), lambda b, j: (b, 0)),
        out_shape=jax.ShapeDtypeStruct((batch * ts, d), BF16),
        scratch_shapes=[pltpu.VMEM((2 * nh, ts, 1), F32), pltpu.VMEM((2 * nh, ts, 1), F32),
                        pltpu.VMEM((2 * nh, ts, hw), F32)],
        compiler_params=_cp(("parallel", "arbitrary")),
        name="sample_diff_attention",
    )(table, q, cache_k[layer].astype(BF16).reshape(batch, past, d), cache_v[layer].astype(BF16).reshape(batch, past, d),
      k_new, v_new, blast, bnew, lam_params, gain.reshape(1, hw))


def kernel(x_prompt, x_sample, state_hgrn, state_pool, cache_k, cache_v, norm_mix, norm_ffn, norm_final, hgrn_w_q, hgrn_w_f, hgrn_w_i, hgrn_w_g, hgrn_w_o, hgrn_lb_logits, hgrn_norm_gain, pool_w, pool_scale, attn_w_q, attn_w_k, attn_w_v, attn_w_o, attn_lambda_q1, attn_lambda_k1, attn_lambda_q2, attn_lambda_k2, attn_subln_gain, rel_bias_table, ffn_w_gate, ffn_w_up, ffn_w_down):
    bp, tp, d = x_prompt.shape
    bs, ts, _ = x_sample.shape
    past = cache_k.shape[2]
    depth = norm_mix.shape[0]
    assert bp == 1
    bf = lambda a: a.astype(BF16)
    xs = [x_prompt.reshape(bp * tp, d), x_sample.reshape(bs * ts, d)]
    dims = [(bp, tp), (bs, ts)]
    hg, pool_st, k_out, v_out = [[], []], [[], []], [[], []], [[], []]

    for i in range(depth):
        m, j = i % N_MIXERS, i // N_MIXERS
        if m == 0:
            w_cat = bf(jnp.concatenate([hgrn_w_q[j], hgrn_w_f[j], hgrn_w_i[j], hgrn_w_g[j]], axis=1))
            w_o = bf(hgrn_w_o[j])
            s0s = [jnp.zeros((bp, HG_HEADS, HG_DK, HG_DK), F32), state_hgrn[j]]
            for r in range(2):
                b, t = dims[r]
                proj, = norm_mm(xs[r], norm_mix[i], w_cat, [F32])
                og, st = hgrn_recurrence(proj, hgrn_lb_logits, hgrn_norm_gain[j], s0s[r], b, t, j)
                xs[r] = mm_res(og, w_o, xs[r])
                hg[r].append(st)
        elif m == 1:
            w_p = bf(pool_w[j])
            hists = [jnp.zeros((bp, POOL_HIST, d), F32), state_pool[j]]
            for r in range(2):
                b, t = dims[r]
                xs[r], st = pool_mixer(xs[r], norm_mix[i], hists[r], w_p, pool_scale[j], b, t, (0, past)[r])
                pool_st[r].append(st[:, 1:, :])
        else:
            lambda_init = 0.8 - 0.6 * math.exp(-0.3 * i)
            w_q, w_k, w_v, w_o = bf(attn_w_q[j]), bf(attn_w_k[j]), bf(attn_w_v[j]), bf(attn_w_o[j])
            lam_params = jnp.stack([attn_lambda_q1[j], attn_lambda_k1[j], attn_lambda_q2[j], attn_lambda_k2[j]])
            for r in range(2):
                b, t = dims[r]
                tm = CFG["fa_tk"]
                q, = norm_mm(xs[r], norm_mix[i], w_q, [BF16], out_scale=DA_DH ** -0.5 * LOG2E, tm=tm)
                k32, k16 = norm_mm(xs[r], norm_mix[i], w_k, [F32, BF16], tm=tm)
                v32, v16 = norm_mm(xs[r], norm_mix[i], w_v, [F32, BF16], transposed=[False, r == 0], tm=tm)
                if r == 0:
                    o = flash_diff_attention(q, k16, v16, rel_bias_table, lam_params, attn_subln_gain[j],
                                             lambda_init)
                else:
                    o = sample_diff_attention(q, k16, v16, cache_k, cache_v, j, rel_bias_table, lam_params,
                                              attn_subln_gain[j], lambda_init)
                xs[r] = mm_res(o, w_o, xs[r])
                k_out[r].append(k32.reshape(b, t, 2 * DA_HEADS, DA_DH))
                v_out[r].append(v32.reshape(b, t, DA_HEADS, 2 * DA_DH))
        w_g, w_u, w_d = bf(ffn_w_gate[i]), bf(ffn_w_up[i]), bf(ffn_w_down[i])
        g_final = norm_final if i == depth - 1 else None
        for r in range(2):
            xs[r] = ffn(xs[r], norm_ffn[i], w_g, w_u, w_d, g_final)

    return (xs[0].reshape(bp, tp, d), xs[1].reshape(bs, ts, d),
            jnp.stack(hg[0]), jnp.stack(hg[1]), jnp.stack(pool_st[0]), jnp.stack(pool_st[1]),
            jnp.stack(k_out[0]), jnp.stack(v_out[0]), jnp.stack(k_out[1]), jnp.stack(v_out[1]))
```

```python
import functools
import math

import numpy as np
import jax
import jax.numpy as jnp
from jax import lax
from jax.experimental import pallas as pl
from jax.experimental.pallas import tpu as pltpu

F32 = jnp.float32
BF16 = jnp.bfloat16

EPS = 1e-6
LOG2E = math.log2(math.e)
F_MIN = 1e-6
MASK_VALUE = -1e30
HG_HEADS = 16
HG_DK = 128
POOL_WINDOWS = (2, 4, 8, 16)
POOL_HIST = 15
DA_HEADS = 8
DA_DH = 128
ATT_CHUNK = 64
REL_BUCKETS = 32
REL_MAX_DIST = 128
N_MIXERS = 3

CFG = dict(
    mm_tm=1024, mm_tn=1024,
    ffn_tm=512, ffn_tf=512,
    pool_tm=512,
    hg_chunk=256, hg_rows=512, hg_heads=4, hg_unroll=2,
    fa_tq=512, fa_tk=512,
    sa_tk=512,
    vmem=56 * 1024 * 1024,
)


def _cp(sem):
    return pltpu.CompilerParams(dimension_semantics=sem, vmem_limit_bytes=CFG["vmem"])


def _rms(x, g):
    return x * lax.rsqrt(jnp.mean(x * x, axis=-1, keepdims=True) + EPS) * g


def _dot(a, b):
    return jnp.dot(a, b, preferred_element_type=F32)


def _dot_nt(a, b):
    return lax.dot_general(a, b, (((1,), (1,)), ((), ())), preferred_element_type=F32)


def _norm_mm_kernel(x_ref, g_ref, w_ref, *rest, out_scale, transposed):
    outs, xn_ref = rest[:-1], rest[-1]

    @pl.when(pl.program_id(1) == 0)
    def _():
        xn_ref[...] = _rms(x_ref[...], g_ref[...]).astype(BF16)

    y = _dot(xn_ref[...], w_ref[...])
    if out_scale is not None:
        y = y * out_scale
    for o, tr in zip(outs, transposed):
        if tr:
            o[0] = y.T.astype(o.dtype)
        else:
            o[...] = y.astype(o.dtype)


def norm_mm(x, g, w, out_dtypes, out_scale=None, transposed=None, tm=None):
    m, d = x.shape
    n = w.shape[1]
    tm, tn = min(tm or CFG["mm_tm"], m), min(CFG["mm_tn"], n)
    transposed = transposed or [False] * len(out_dtypes)
    out_specs = [pl.BlockSpec((1, tn, tm), lambda i, j: (i, j, 0)) if tr
                 else pl.BlockSpec((tm, tn), lambda i, j: (i, j)) for tr in transposed]
    out_shape = [jax.ShapeDtypeStruct((m // tm, n, tm) if tr else (m, n), dt)
                 for dt, tr in zip(out_dtypes, transposed)]
    return pl.pallas_call(
        functools.partial(_norm_mm_kernel, out_scale=out_scale, transposed=tuple(transposed)),
        grid=(m // tm, n // tn),
        in_specs=[pl.BlockSpec((tm, d), lambda i, j: (i, 0)),
                  pl.BlockSpec((1, d), lambda i, j: (0, 0)),
                  pl.BlockSpec((d, tn), lambda i, j: (0, j))],
        out_specs=out_specs,
        out_shape=out_shape,
        scratch_shapes=[pltpu.VMEM((tm, d), BF16)],
        compiler_params=_cp(("parallel", "arbitrary")),
        name="norm_mm",
    )(x, g.reshape(1, d), w)


def _mm_res_kernel(a_ref, w_ref, r_ref, o_ref):
    o_ref[...] = r_ref[...] + _dot(a_ref[...], w_ref[...])


def mm_res(a, w, r):
    m, k = a.shape
    n = w.shape[1]
    tm, tn = min(CFG["mm_tm"], m), min(CFG["mm_tn"], n)
    return pl.pallas_call(
        _mm_res_kernel,
        grid=(m // tm, n // tn),
        in_specs=[pl.BlockSpec((tm, k), lambda i, j: (i, 0)),
                  pl.BlockSpec((k, tn), lambda i, j: (0, j)),
                  pl.BlockSpec((tm, tn), lambda i, j: (i, j))],
        out_specs=pl.BlockSpec((tm, tn), lambda i, j: (i, j)),
        out_shape=jax.ShapeDtypeStruct((m, n), F32),
        compiler_params=_cp(("parallel", "arbitrary")),
        name="mm_res",
    )(a, w, r)


def _ffn_kernel(x_ref, g_ref, wg_ref, wu_ref, wd_ref, gf_ref, o_ref, xn_ref, *, final_norm):
    j = pl.program_id(1)

    @pl.when(j == 0)
    def _():
        x = x_ref[...]
        xn_ref[...] = _rms(x, g_ref[...]).astype(BF16)
        o_ref[...] = x

    xn = xn_ref[...]
    a = _dot(xn, wg_ref[...])
    b = _dot(xn, wu_ref[...])
    h = (a * jax.nn.sigmoid(a) * b).astype(BF16)
    o_ref[...] += _dot(h, wd_ref[...])

    if final_norm:
        @pl.when(j == pl.num_programs(1) - 1)
        def _():
            o_ref[...] = _rms(o_ref[...], gf_ref[...])


def ffn(x, g, wg, wu, wd, g_final=None):
    m, d = x.shape
    f = wg.shape[1]
    tm, tf = min(CFG["ffn_tm"], m), min(CFG["ffn_tf"], f)
    final_norm = g_final is not None
    gf = (g_final if final_norm else g).reshape(1, d)
    return pl.pallas_call(
        functools.partial(_ffn_kernel, final_norm=final_norm),
        grid=(m // tm, f // tf),
        in_specs=[pl.BlockSpec((tm, d), lambda i, j: (i, 0)),
                  pl.BlockSpec((1, d), lambda i, j: (0, 0)),
                  pl.BlockSpec((d, tf), lambda i, j: (0, j)),
                  pl.BlockSpec((d, tf), lambda i, j: (0, j)),
                  pl.BlockSpec((tf, d), lambda i, j: (j, 0)),
                  pl.BlockSpec((1, d), lambda i, j: (0, 0))],
        out_specs=pl.BlockSpec((tm, d), lambda i, j: (i, 0)),
        out_shape=jax.ShapeDtypeStruct((m, d), F32),
        scratch_shapes=[pltpu.VMEM((tm, d), BF16)],
        compiler_params=_cp(("parallel", "arbitrary")),
        name="ffn",
    )(x, g.reshape(1, d), wg, wu, wd, gf)


def _hgrn_levels(c):
    lv, h = [], c // 2
    while h >= 8:
        lv.append(h)
        h //= 2
    return lv


def _bcast_rows(g, rows, rep):
    return jnp.concatenate([jnp.broadcast_to(g[r:r + 1, :], (rep, g.shape[1])) for r in rows], axis=0)


def _hgrn_kernel(q_ref, f_ref, v_ref, gt_ref, lbl_ref, gain_ref, s0_ref, l_ref, og_ref, st_ref, stt_ref,
                 *, c, n_sub, hb, layer):
    cb = pl.program_id(2)
    dk = HG_DK

    @pl.when(cb == 0)
    def _():
        for h in range(hb):
            stt_ref[h] = s0_ref[0, h].T

    lg = lbl_ref[...]
    e = jnp.exp(lg - jnp.max(lg, axis=0, keepdims=True))
    p = e / jnp.sum(e, axis=0, keepdims=True)
    lb = p[0:1, :]
    for i in range(1, layer + 1):
        lb = lb + p[i:i + 1, :]
    lb = jnp.maximum(lb - p[0:1, :], 0.0)

    ti = lax.broadcasted_iota(jnp.int32, (c, c), 0)
    si = lax.broadcasted_iota(jnp.int32, (c, c), 1)
    levels = _hgrn_levels(c)
    masks = []
    for hh in levels:
        sh = int(math.log2(2 * hh))
        masks.append(((ti >> sh) == (si >> sh)) & ((ti & (2 * hh - 1)) >= hh) & ((si & (2 * hh - 1)) < hh))
    mask_loc = ((ti >> 3) == (si >> 3)) & (si <= ti)
    lmat = l_ref[...]
    gain = gain_ref[...]

    def chunk(ci, carry):
        r0 = pl.multiple_of(ci * c, c)
        fp = f_ref[pl.ds(r0, c), :]
        f = lb + (1.0 - lb) * jax.nn.sigmoid(fp)
        g = jnp.log2(jnp.maximum(f, F_MIN))
        kk = 1.0 - f
        g_hi = g.astype(BF16)
        r1 = g - g_hi.astype(F32)
        g_mid = r1.astype(BF16)
        g_lo = (r1 - g_mid.astype(F32)).astype(BF16)
        gc = _dot(lmat, g_hi) + _dot(lmat, g_mid) + _dot(lmat, g_lo)
        qa = q_ref[pl.ds(r0, c), :]
        va = v_ref[pl.ds(r0, c), :]
        ga = gt_ref[pl.ds(r0, c), :]
        for h in range(hb):
            sl = slice(h * dk, (h + 1) * dk)
            gh, qh, kh = gc[:, sl], qa[:, sl], kk[:, sl]
            vh = va[:, sl].astype(BF16)
            stt = stt_ref[h]
            inter = _dot_nt((qh * jnp.exp2(gh)).astype(BF16), stt.astype(BF16))
            g_last = gh[c - 1:c, :]
            k_dec = (kh * jnp.exp2(g_last - gh)).astype(BF16)
            stt_ref[h] = stt * jnp.exp2(g_last) + lax.dot_general(
                vh, k_dec, (((0,), (0,)), ((), ())), preferred_element_type=F32)
            q16, k16 = qh.astype(BF16), kh.astype(BF16)
            g_loc = _bcast_rows(gh, [8 * b + 3 for b in range(c // 8)], 8)
            sc = jnp.where(mask_loc,
                           _dot_nt(q16 * jnp.exp2(gh - g_loc).astype(BF16),
                                   k16 * jnp.exp2(g_loc - gh).astype(BF16)), 0.0)
            for hh, mk in zip(levels, masks):
                g_mid_rows = _bcast_rows(gh, [b * 2 * hh + hh - 1 for b in range(c // (2 * hh))], 2 * hh)
                fac = jnp.exp2(-jnp.abs(gh - g_mid_rows)).astype(BF16)
                sc = jnp.where(mk, _dot_nt(q16 * fac, k16 * fac), sc)
            o = inter + _dot(sc.astype(BF16), vh)
            on = _rms(o, gain)
            gate = ga[:, sl]
            og_ref[pl.ds(r0, c), sl] = (on * (gate * jax.nn.sigmoid(gate))).astype(BF16)
        return carry

    lax.fori_loop(0, n_sub, chunk, 0, unroll=min(CFG["hg_unroll"], n_sub))

    @pl.when(cb == pl.num_programs(2) - 1)
    def _():
        for h in range(hb):
            st_ref[0, h] = stt_ref[h].T


def hgrn_recurrence(proj, lb_logits, gain, s0, batch, t, layer):
    d = HG_HEADS * HG_DK
    c = min(CFG["hg_chunk"], t)
    rows = min(CFG["hg_rows"], t)
    hb = CFG["hg_heads"]
    w = hb * HG_DK
    ncb = t // rows
    n_layers = lb_logits.shape[0]
    lmat = jnp.asarray(np.tril(np.ones((c, c), np.float32)), BF16)

    def col(sec):
        return lambda b, hg, cb: (b * ncb + cb, sec * (d // w) + hg)

    og, st = pl.pallas_call(
        functools.partial(_hgrn_kernel, c=c, n_sub=rows // c, hb=hb, layer=layer),
        grid=(batch, HG_HEADS // hb, ncb),
        in_specs=[pl.BlockSpec((rows, w), col(0)),
                  pl.BlockSpec((rows, w), col(1)),
                  pl.BlockSpec((rows, w), col(2)),
                  pl.BlockSpec((rows, w), col(3)),
                  pl.BlockSpec((n_layers, w), lambda b, hg, cb: (0, hg)),
                  pl.BlockSpec((1, HG_DK), lambda b, hg, cb: (0, 0)),
                  pl.BlockSpec((1, hb, HG_DK, HG_DK), lambda b, hg, cb: (b, hg, 0, 0)),
                  pl.BlockSpec((c, c), lambda b, hg, cb: (0, 0))],
        out_specs=[pl.BlockSpec((rows, w), lambda b, hg, cb: (b * ncb + cb, hg)),
                   pl.BlockSpec((1, hb, HG_DK, HG_DK), lambda b, hg, cb: (b, hg, 0, 0))],
        out_shape=[jax.ShapeDtypeStruct((batch * t, d), BF16),
                   jax.ShapeDtypeStruct((batch, HG_HEADS, HG_DK, HG_DK), F32)],
        scratch_shapes=[pltpu.VMEM((hb, HG_DK, HG_DK), F32)],
        compiler_params=_cp(("parallel", "parallel", "arbitrary")),
        name="hgrn_recurrence",
    )(proj, proj, proj, proj, lb_logits, gain.reshape(1, HG_DK), s0, lmat)
    return og, st


def _pool_kernel(x_ref, g_ref, hist_ref, w_ref, sc_ref, o_ref, st_ref, buf_ref, *, tm, pos0):
    tb = pl.program_id(1)
    x = x_ref[...]
    xn = _rms(x, g_ref[...])
    hp = POOL_HIST + 1

    @pl.when(tb == 0)
    def _():
        buf_ref[0:hp, :] = hist_ref[0]

    @pl.when(tb > 0)
    def _():
        buf_ref[0:hp, :] = buf_ref[tm:tm + hp, :]

    buf_ref[hp:hp + tm, :] = xn
    st_ref[0] = buf_ref[tm:tm + hp, :]
    pos = pos0 + tb * tm + lax.broadcasted_iota(jnp.int32, (tm, 1), 0)
    gc = x.shape[1] // len(POOL_WINDOWS)
    for gi, win in enumerate(POOL_WINDOWS):
        cs = slice(gi * gc, (gi + 1) * gc)
        ws = xn[:, cs]
        for sft in range(1, win):
            ws = ws + buf_ref[hp - sft:hp - sft + tm, cs]
        cnt = jnp.minimum(win, pos + 1).astype(F32)
        pooled = (ws / cnt - xn[:, cs]).astype(BF16)
        o_ref[:, cs] = x[:, cs] + _dot(pooled, w_ref[gi]) * sc_ref[:, cs]


def pool_mixer(x, g, hist, w, scale, batch, t, pos0):
    d = x.shape[1]
    tm = min(CFG["pool_tm"], t)
    nb = t // tm
    hp = POOL_HIST + 1
    hist16 = jnp.concatenate([jnp.zeros((batch, 1, d), F32), hist], axis=1)
    ng, gc = w.shape[0], w.shape[1]
    return pl.pallas_call(
        functools.partial(_pool_kernel, tm=tm, pos0=pos0),
        grid=(batch, nb),
        in_specs=[pl.BlockSpec((tm, d), lambda b, i: (b * nb + i, 0)),
                  pl.BlockSpec((1, d), lambda b, i: (0, 0)),
                  pl.BlockSpec((1, hp, d), lambda b, i: (b, 0, 0)),
                  pl.BlockSpec((ng, gc, gc), lambda b, i: (0, 0, 0)),
                  pl.BlockSpec((1, d), lambda b, i: (0, 0))],
        out_specs=[pl.BlockSpec((tm, d), lambda b, i: (b * nb + i, 0)),
                   pl.BlockSpec((1, hp, d), lambda b, i: (b, 0, 0))],
        out_shape=[jax.ShapeDtypeStruct(x.shape, F32),
                   jax.ShapeDtypeStruct((batch, hp, d), F32)],
        scratch_shapes=[pltpu.VMEM((tm + hp, d), F32)],
        compiler_params=_cp(("parallel", "arbitrary")),
        name="pool_mixer",
    )(x, g.reshape(1, d), hist16, w, scale.reshape(1, d))


def _rel_bucket_np(rel):
    nb = REL_BUCKETS // 2
    max_exact = nb // 2
    n = np.abs(rel)
    large = max_exact + (np.log(np.maximum(n, max_exact).astype(np.float64) / max_exact)
                         / math.log(REL_MAX_DIST / max_exact) * (nb - max_exact)).astype(np.int64)
    large = np.minimum(large, nb - 1)
    return (np.where(rel > 0, nb, 0) + np.where(n < max_exact, n, large)).astype(np.int32)


def _bucket_tile(q_pos, k_pos):
    q_pos, k_pos = np.asarray(q_pos)[:, None], np.asarray(k_pos)[None, :]
    b = _rel_bucket_np(k_pos - q_pos)
    return np.where((k_pos // ATT_CHUNK) <= (q_pos // ATT_CHUNK), b, -1).astype(np.int32)


def _bias_kernel(tab_ref, bk_ref, o_ref):
    h = pl.program_id(0)
    bk = bk_ref[...]
    bias = jnp.full(bk.shape, MASK_VALUE, F32)
    for b in range(REL_BUCKETS):
        bias = jnp.where(bk == b, tab_ref[b, h] * LOG2E, bias)
    o_ref[0] = bias


def bias_tiles(table, bucket):
    r, c = bucket.shape
    nh = table.shape[1]
    return pl.pallas_call(
        _bias_kernel,
        grid=(nh,),
        in_specs=[pl.BlockSpec(memory_space=pltpu.SMEM),
                  pl.BlockSpec((r, c), lambda h: (0, 0))],
        out_specs=pl.BlockSpec((1, r, c), lambda h: (h, 0, 0)),
        out_shape=jax.ShapeDtypeStruct((nh, r, c), F32),
        compiler_params=_cp(("arbitrary",)),
        name="bias_tiles",
    )(table, jnp.asarray(bucket))


def _lambda(lq1, lk1, lq2, lk2, lambda_init):
    return (jnp.exp(jnp.sum(lq1 * lk1, axis=-1, keepdims=True))
            - jnp.exp(jnp.sum(lq2 * lk2, axis=-1, keepdims=True)) + lambda_init)


def _softmax_step(q, k, v, bias, m_ref, l_ref, acc_ref, idx):
    s = _dot_nt(q, k) + bias
    m_prev = m_ref[idx]
    m_new = jnp.maximum(m_prev, jnp.max(s, axis=-1, keepdims=True))
    p = jnp.exp2(s - m_new)
    alpha = jnp.exp2(m_prev - m_new)
    l_ref[idx] = alpha * l_ref[idx] + jnp.sum(p, axis=-1, keepdims=True)
    acc_ref[idx] = alpha * acc_ref[idx] + _dot(p.astype(BF16), v)
    m_ref[idx] = m_new


def _flash_kernel(q_ref, k_ref, vt_ref, bias_ref, lam_ref, gain_ref, o_ref,
                  m_ref, l_ref, acc_ref, sa_ref, sb_ref, pa_ref, pb_ref, ala_ref, alb_ref, *, tq, tk, lambda_init):
    i = pl.program_id(1)
    dh = DA_DH
    m_ref[...] = jnp.full(m_ref.shape, -jnp.inf, F32)
    l_ref[...] = jnp.zeros(l_ref.shape, F32)
    acc_ref[...] = jnp.zeros(acc_ref.shape, F32)
    q = q_ref[...]

    nkb = vt_ref.shape[0]

    def scores(j, s_ref):
        jc = jnp.minimum(j, nkb - 1)
        kb = k_ref[pl.ds(pl.multiple_of(jc * tk, tk), tk), :]
        for c in range(2):
            s_ref[c] = _dot_nt(kb[:, c * dh:(c + 1) * dh], q[:, c * dh:(c + 1) * dh])

    def softmax(j, s_ref, p_ref, al_ref):
        bias = bias_ref[0, jnp.clip(j - (i - 2), 0, 3)]
        for c in range(2):
            s = s_ref[c] + bias
            m_prev = m_ref[c]
            m_new = jnp.maximum(m_prev, jnp.max(s, axis=0, keepdims=True))
            p = jnp.exp2(s - m_new)
            alpha = jnp.exp2(m_prev - m_new)
            l_ref[c] = alpha * l_ref[c] + jnp.sum(p, axis=0, keepdims=True)
            m_ref[c] = m_new
            al_ref[c] = alpha
            p_ref[c] = p.astype(BF16)

    def values(j, p_ref, al_ref):
        vt = vt_ref[jnp.minimum(j, nkb - 1)]
        for c in range(2):
            acc_ref[c] = al_ref[c] * acc_ref[c] + _dot(vt, p_ref[c])

    scores(0, sa_ref)
    scores(1, sb_ref)
    softmax(0, sa_ref, pa_ref, ala_ref)

    def pair(t, carry):
        j = 2 * t
        scores(j + 2, sa_ref)
        softmax(j + 1, sb_ref, pb_ref, alb_ref)
        values(j, pa_ref, ala_ref)
        scores(j + 3, sb_ref)
        softmax(j + 2, sa_ref, pa_ref, ala_ref)
        values(j + 1, pb_ref, alb_ref)
        return carry

    lax.fori_loop(0, (i + 2) // 2, pair, 0)

    lam = _lambda(lam_ref[0:1, :], lam_ref[1:2, :], lam_ref[2:3, :], lam_ref[3:4, :], lambda_init)
    o = acc_ref[0] / l_ref[0] - lam * (acc_ref[1] / l_ref[1])
    on = o * lax.rsqrt(jnp.mean(o * o, axis=0, keepdims=True) + EPS) * (1.0 - lambda_init)
    o_ref[...] = (on.T * gain_ref[...]).astype(BF16)


def flash_diff_attention(q, k, vt, table, lam_params, gain, lambda_init):
    t, d = q.shape
    tq, tk = min(CFG["fa_tq"], t), min(CFG["fa_tk"], t)
    assert tq == tk and tk % ATT_CHUNK == 0 and vt.shape == (t // tk, d, tk)
    hw = 2 * DA_DH
    nh = d // hw
    far = _rel_bucket_np(-np.arange(tk + 1, max(t, tk + 2)))
    far_bucket = int(far[0])
    assert (far == far_bucket).all()
    near = _bucket_tile(tk + np.arange(tq), np.arange(2 * tk)).T
    bucket = np.concatenate([np.full((tk, tq), far_bucket, np.int32), near,
                             np.full((tk, tq), -1, np.int32)], axis=0)
    bank = bias_tiles(table, bucket).reshape(nh, 4, tk, tq)
    once = pl.Buffered(1)
    return pl.pallas_call(
        functools.partial(_flash_kernel, tq=tq, tk=tk, lambda_init=lambda_init),
        grid=(nh, t // tq),
        in_specs=[pl.BlockSpec((tq, hw), lambda h, i: (i, h)),
                  pl.BlockSpec((t, hw), lambda h, i: (0, h), pipeline_mode=once),
                  pl.BlockSpec((t // tk, hw, tk), lambda h, i: (0, h, 0), pipeline_mode=once),
                  pl.BlockSpec((1, 4, tk, tq), lambda h, i: (h, 0, 0, 0), pipeline_mode=once),
                  pl.BlockSpec((4, DA_DH), lambda h, i: (0, 0)),
                  pl.BlockSpec((1, hw), lambda h, i: (0, 0))],
        out_specs=pl.BlockSpec((tq, hw), lambda h, i: (i, h)),
        out_shape=jax.ShapeDtypeStruct((t, d), BF16),
        scratch_shapes=[pltpu.VMEM((2, 1, tq), F32), pltpu.VMEM((2, 1, tq), F32),
                        pltpu.VMEM((2, hw, tq), F32),
                        pltpu.VMEM((2, tk, tq), F32), pltpu.VMEM((2, tk, tq), F32),
                        pltpu.VMEM((2, tk, tq), BF16), pltpu.VMEM((2, tk, tq), BF16),
                        pltpu.VMEM((2, 1, tq), F32), pltpu.VMEM((2, 1, tq), F32)],
        compiler_params=_cp(("parallel", "arbitrary")),
        name="flash_diff_attention",
    )(q, k, vt, bank, lam_params, gain.reshape(1, hw))


def _sample_attn_kernel(tab_ref, q_ref, ck_ref, cvl_ref, cvh_ref, kn_ref, vn_ref, blast_ref, bnew_ref, lam_ref,
                        gain_ref, o_ref, m_ref, l_ref, acc_ref, *, tk, far_bucket, lambda_init):
    j = pl.program_id(1)
    nj = pl.num_programs(1)
    dh = DA_DH
    hw = 2 * dh
    nh = q_ref.shape[1] // hw
    ts = q_ref.shape[0]

    @pl.when(j == 0)
    def _():
        m_ref[...] = jnp.full(m_ref.shape, -jnp.inf, F32)
        l_ref[...] = jnp.zeros(l_ref.shape, F32)
        acc_ref[...] = jnp.zeros(acc_ref.shape, F32)

    q = q_ref[...]

    def block(k_of, v_of, bias_fn):
        for h in range(nh):
            vb = v_of(h)
            for c in range(2):
                cs = slice((2 * h + c) * dh, (2 * h + c + 1) * dh)
                _softmax_step(q[:, cs], k_of(2 * h + c), vb, bias_fn(h), m_ref, l_ref, acc_ref, 2 * h + c)

    def cache_k_of(hc):
        return ck_ref[0, pl.ds(hc, tk, stride=2 * nh), :].astype(BF16)

    def cache_v_of(h):
        return jnp.concatenate([cvl_ref[0, pl.ds(h, tk, stride=nh), :],
                                cvh_ref[0, pl.ds(h, tk, stride=nh), :]], axis=1).astype(BF16)

    @pl.when(j < nj - 1)
    def _():
        block(cache_k_of, cache_v_of, lambda h: tab_ref[far_bucket, h] * LOG2E)

    @pl.when(j == nj - 1)
    def _():
        block(cache_k_of, cache_v_of, lambda h: blast_ref[h])
        block(lambda hc: kn_ref[:, hc * dh:(hc + 1) * dh], lambda h: vn_ref[:, h * hw:(h + 1) * hw],
              lambda h: bnew_ref[h][:, :ts])
        lam = _lambda(lam_ref[0:1, :], lam_ref[1:2, :], lam_ref[2:3, :], lam_ref[3:4, :], lambda_init)
        for h in range(nh):
            o = acc_ref[2 * h] / l_ref[2 * h] - lam * (acc_ref[2 * h + 1] / l_ref[2 * h + 1])
            o_ref[:, h * hw:(h + 1) * hw] = (_rms(o, gain_ref[...]) * (1.0 - lambda_init)).astype(BF16)


def sample_diff_attention(q, k_new, v_new, cache_k, cache_v, layer, table, lam_params, gain, lambda_init):
    n_layers, batch, past, nhc, dh = cache_k.shape
    d = nhc * dh
    ts = q.shape[0] // batch
    hw = 2 * DA_DH
    nh = d // hw
    tk = min(CFG["sa_tk"], past)
    nkb = past // tk
    q_pos = past + np.arange(ts)
    blast = bias_tiles(table, _bucket_tile(q_pos, past - tk + np.arange(tk)))
    bnew = bias_tiles(table, _bucket_tile(q_pos, past + np.arange(128)))
    if nkb > 1:
        far = _bucket_tile(q_pos, np.arange(past - tk))
        far_bucket = int(far[0, 0])
        assert (far == far_bucket).all()
    else:
        far_bucket = 0
    cache_v_rows = cache_v.reshape(n_layers, batch, past * nh, hw)
    return pl.pallas_call(
        functools.partial(_sample_attn_kernel, tk=tk, far_bucket=far_bucket, lambda_init=lambda_init),
        grid=(batch, nkb),
        in_specs=[pl.BlockSpec(memory_space=pltpu.SMEM),
                  pl.BlockSpec((ts, d), lambda b, j: (b, 0)),
                  pl.BlockSpec((None, 1, tk * nhc, dh), lambda b, j: (layer, b, j, 0)),
                  pl.BlockSpec((None, 1, tk * nh, dh), lambda b, j: (layer, b, j, 0)),
                  pl.BlockSpec((None, 1, tk * nh, dh), lambda b, j: (layer, b, j, 1)),
                  pl.BlockSpec((ts, d), lambda b, j: (b, 0)),
                  pl.BlockSpec((ts, d), lambda b, j: (b, 0)),
                  pl.BlockSpec((nh, ts, tk), lambda b, j: (0, 0, 0)),
                  pl.BlockSpec((nh, ts, 128), lambda b, j: (0, 0, 0)),
                  pl.BlockSpec((4, DA_DH), lambda b, j: (0, 0)),
                  pl.BlockSpec((1, hw), lambda b, j: (0, 0))],
        out_specs=pl.BlockSpec((ts, d), lambda b, j: (b, 0)),
        out_shape=jax.ShapeDtypeStruct((batch * ts, d), BF16),
        scratch_shapes=[pltpu.VMEM((2 * nh, ts, 1), F32), pltpu.VMEM((2 * nh, ts, 1), F32),
                        pltpu.VMEM((2 * nh, ts, hw), F32)],
        compiler_params=_cp(("parallel", "arbitrary")),
        name="sample_diff_attention",
    )(table, q, cache_k.reshape(n_layers, batch, past * nhc, dh), cache_v_rows, cache_v_rows,
      k_new, v_new, blast, bnew, lam_params, gain.reshape(1, hw))


def kernel(x_prompt, x_sample, state_hgrn, state_pool, cache_k, cache_v, norm_mix, norm_ffn, norm_final, hgrn_w_q, hgrn_w_f, hgrn_w_i, hgrn_w_g, hgrn_w_o, hgrn_lb_logits, hgrn_norm_gain, pool_w, pool_scale, attn_w_q, attn_w_k, attn_w_v, attn_w_o, attn_lambda_q1, attn_lambda_k1, attn_lambda_q2, attn_lambda_k2, attn_subln_gain, rel_bias_table, ffn_w_gate, ffn_w_up, ffn_w_down):
    bp, tp, d = x_prompt.shape
    bs, ts, _ = x_sample.shape
    past = cache_k.shape[2]
    depth = norm_mix.shape[0]
    assert bp == 1
    bf = lambda a: a.astype(BF16)
    xs = [x_prompt.reshape(bp * tp, d), x_sample.reshape(bs * ts, d)]
    dims = [(bp, tp), (bs, ts)]
    hg, pool_st, k_out, v_out = [[], []], [[], []], [[], []], [[], []]

    for i in range(depth):
        m, j = i % N_MIXERS, i // N_MIXERS
        if m == 0:
            w_cat = bf(jnp.concatenate([hgrn_w_q[j], hgrn_w_f[j], hgrn_w_i[j], hgrn_w_g[j]], axis=1))
            w_o = bf(hgrn_w_o[j])
            s0s = [jnp.zeros((bp, HG_HEADS, HG_DK, HG_DK), F32), state_hgrn[j]]
            for r in range(2):
                b, t = dims[r]
                proj, = norm_mm(xs[r], norm_mix[i], w_cat, [F32])
                og, st = hgrn_recurrence(proj, hgrn_lb_logits, hgrn_norm_gain[j], s0s[r], b, t, j)
                xs[r] = mm_res(og, w_o, xs[r])
                hg[r].append(st)
        elif m == 1:
            w_p = bf(pool_w[j])
            hists = [jnp.zeros((bp, POOL_HIST, d), F32), state_pool[j]]
            for r in range(2):
                b, t = dims[r]
                xs[r], st = pool_mixer(xs[r], norm_mix[i], hists[r], w_p, pool_scale[j], b, t, (0, past)[r])
                pool_st[r].append(st[:, 1:, :])
        else:
            lambda_init = 0.8 - 0.6 * math.exp(-0.3 * i)
            w_q, w_k, w_v, w_o = bf(attn_w_q[j]), bf(attn_w_k[j]), bf(attn_w_v[j]), bf(attn_w_o[j])
            lam_params = jnp.stack([attn_lambda_q1[j], attn_lambda_k1[j], attn_lambda_q2[j], attn_lambda_k2[j]])
            for r in range(2):
                b, t = dims[r]
                q, = norm_mm(xs[r], norm_mix[i], w_q, [BF16], out_scale=DA_DH ** -0.5 * LOG2E)
                k32, k16 = norm_mm(xs[r], norm_mix[i], w_k, [F32, BF16])
                v32, v16 = norm_mm(xs[r], norm_mix[i], w_v, [F32, BF16], transposed=[False, r == 0],
                                   tm=CFG["fa_tk"])
                if r == 0:
                    o = flash_diff_attention(q, k16, v16, rel_bias_table, lam_params, attn_subln_gain[j],
                                             lambda_init)
                else:
                    o = sample_diff_attention(q, k16, v16, cache_k, cache_v, j, rel_bias_table, lam_params,
                                              attn_subln_gain[j], lambda_init)
                xs[r] = mm_res(o, w_o, xs[r])
                k_out[r].append(k32.reshape(b, t, 2 * DA_HEADS, DA_DH))
                v_out[r].append(v32.reshape(b, t, DA_HEADS, 2 * DA_DH))
        w_g, w_u, w_d = bf(ffn_w_gate[i]), bf(ffn_w_up[i]), bf(ffn_w_down[i])
        g_final = norm_final if i == depth - 1 else None
        for r in range(2):
            xs[r] = ffn(xs[r], norm_ffn[i], w_g, w_u, w_d, g_final)

    return (xs[0].reshape(bp, tp, d), xs[1].reshape(bs, ts, d),
            jnp.stack(hg[0]), jnp.stack(hg[1]), jnp.stack(pool_st[0]), jnp.stack(pool_st[1]),
            jnp.stack(k_out[0]), jnp.stack(v_out[0]), jnp.stack(k_out[1]), jnp.stack(v_out[1]))
```

```python
import functools
import math

import numpy as np
import jax
import jax.numpy as jnp
from jax import lax
from jax.experimental import pallas as pl
from jax.experimental.pallas import tpu as pltpu

F32 = jnp.float32
BF16 = jnp.bfloat16

EPS = 1e-6
LOG2E = math.log2(math.e)
F_MIN = 1e-6
MASK_VALUE = -1e30
HG_HEADS = 16
HG_DK = 128
POOL_WINDOWS = (2, 4, 8, 16)
POOL_HIST = 15
DA_HEADS = 8
DA_DH = 128
ATT_CHUNK = 64
ONES_ROWS = 16
REL_BUCKETS = 32
REL_MAX_DIST = 128
N_MIXERS = 3

CFG = dict(
    mm_tm=1024, mm_tn=1024,
    ffn_tm=512, ffn_tf=512,
    pool_tm=512,
    hg_chunk=256, hg_rows=512, hg_heads=4, hg_unroll=2,
    fa_tq=512, fa_tk=512,
    sa_tk=512,
    vmem=56 * 1024 * 1024,
)


def _cp(sem):
    return pltpu.CompilerParams(dimension_semantics=sem, vmem_limit_bytes=CFG["vmem"])


def _rms(x, g):
    return x * lax.rsqrt(jnp.mean(x * x, axis=-1, keepdims=True) + EPS) * g


def _dot(a, b):
    return jnp.dot(a, b, preferred_element_type=F32)


def _dot_nt(a, b):
    return lax.dot_general(a, b, (((1,), (1,)), ((), ())), preferred_element_type=F32)


def _norm_mm_kernel(x_ref, g_ref, w_ref, *rest, out_scale, transposed):
    outs, xn_ref = rest[:-1], rest[-1]

    @pl.when(pl.program_id(1) == 0)
    def _():
        xn_ref[...] = _rms(x_ref[...], g_ref[...]).astype(BF16)

    y = _dot(xn_ref[...], w_ref[...])
    if out_scale is not None:
        y = y * out_scale
    for o, tr in zip(outs, transposed):
        if tr:
            o[0] = y.T.astype(o.dtype)
        else:
            o[...] = y.astype(o.dtype)


def norm_mm(x, g, w, out_dtypes, out_scale=None, transposed=None, tm=None):
    m, d = x.shape
    n = w.shape[1]
    tm, tn = min(tm or CFG["mm_tm"], m), min(CFG["mm_tn"], n)
    transposed = transposed or [False] * len(out_dtypes)
    out_specs = [pl.BlockSpec((1, tn, tm), lambda i, j: (i, j, 0)) if tr
                 else pl.BlockSpec((tm, tn), lambda i, j: (i, j)) for tr in transposed]
    out_shape = [jax.ShapeDtypeStruct((m // tm, n, tm) if tr else (m, n), dt)
                 for dt, tr in zip(out_dtypes, transposed)]
    return pl.pallas_call(
        functools.partial(_norm_mm_kernel, out_scale=out_scale, transposed=tuple(transposed)),
        grid=(m // tm, n // tn),
        in_specs=[pl.BlockSpec((tm, d), lambda i, j: (i, 0)),
                  pl.BlockSpec((1, d), lambda i, j: (0, 0)),
                  pl.BlockSpec((d, tn), lambda i, j: (0, j))],
        out_specs=out_specs,
        out_shape=out_shape,
        scratch_shapes=[pltpu.VMEM((tm, d), BF16)],
        compiler_params=_cp(("parallel", "arbitrary")),
        name="norm_mm",
    )(x, g.reshape(1, d), w)


def _qkv_kernel(x_ref, g_ref, w_ref, q_ref, k32_ref, k16_ref, v32_ref, v16_ref, xn_ref, *, nq, q_scale, v_transposed):
    j = pl.program_id(1)

    @pl.when(j == 0)
    def _():
        xn_ref[...] = _rms(x_ref[...], g_ref[...]).astype(BF16)

    y = _dot(xn_ref[...], w_ref[...])

    @pl.when(j < nq)
    def _():
        q_ref[...] = (y * q_scale).astype(BF16)

    @pl.when((j >= nq) & (j < 2 * nq))
    def _():
        k32_ref[...] = y
        k16_ref[...] = y.astype(BF16)

    @pl.when(j >= 2 * nq)
    def _():
        v32_ref[...] = y
        if v_transposed:
            v16_ref[0] = y.T.astype(BF16)
        else:
            v16_ref[...] = y.astype(BF16)


def qkv_proj(x, g, w, q_scale, v_transposed, tm):
    m, d = x.shape
    n = w.shape[1] // 3
    tm, tn = min(tm, m), min(CFG["mm_tn"], n)
    nq = n // tn
    spec = lambda off: pl.BlockSpec((tm, tn), lambda i, j: (i, jnp.clip(j - off, 0, nq - 1)))
    if v_transposed:
        v16_spec = pl.BlockSpec((1, tn, tm), lambda i, j: (i, jnp.clip(j - 2 * nq, 0, nq - 1), 0))
        v16_shape = jax.ShapeDtypeStruct((m // tm, n, tm), BF16)
    else:
        v16_spec, v16_shape = spec(2 * nq), jax.ShapeDtypeStruct((m, n), BF16)
    return pl.pallas_call(
        functools.partial(_qkv_kernel, nq=nq, q_scale=q_scale, v_transposed=v_transposed),
        grid=(m // tm, 3 * nq),
        in_specs=[pl.BlockSpec((tm, d), lambda i, j: (i, 0)),
                  pl.BlockSpec((1, d), lambda i, j: (0, 0)),
                  pl.BlockSpec((d, tn), lambda i, j: (0, j))],
        out_specs=[spec(0), spec(nq), spec(nq), spec(2 * nq), v16_spec],
        out_shape=[jax.ShapeDtypeStruct((m, n), BF16), jax.ShapeDtypeStruct((m, n), F32),
                   jax.ShapeDtypeStruct((m, n), BF16), jax.ShapeDtypeStruct((m, n), F32), v16_shape],
        scratch_shapes=[pltpu.VMEM((tm, d), BF16)],
        compiler_params=_cp(("parallel", "arbitrary")),
        name="qkv_proj",
    )(x, g.reshape(1, d), w)


def _mm_res_kernel(a_ref, w_ref, r_ref, o_ref):
    o_ref[...] = r_ref[...] + _dot(a_ref[...], w_ref[...])


def mm_res(a, w, r):
    m, k = a.shape
    n = w.shape[1]
    tm, tn = min(CFG["mm_tm"], m), min(CFG["mm_tn"], n)
    return pl.pallas_call(
        _mm_res_kernel,
        grid=(m // tm, n // tn),
        in_specs=[pl.BlockSpec((tm, k), lambda i, j: (i, 0)),
                  pl.BlockSpec((k, tn), lambda i, j: (0, j)),
                  pl.BlockSpec((tm, tn), lambda i, j: (i, j))],
        out_specs=pl.BlockSpec((tm, tn), lambda i, j: (i, j)),
        out_shape=jax.ShapeDtypeStruct((m, n), F32),
        compiler_params=_cp(("parallel", "arbitrary")),
        name="mm_res",
    )(a, w, r)


def _ffn_kernel(x_ref, g_ref, wg_ref, wu_ref, wd_ref, gf_ref, o_ref, xn_ref, *, final_norm):
    j = pl.program_id(1)

    @pl.when(j == 0)
    def _():
        x = x_ref[...]
        xn_ref[...] = _rms(x, g_ref[...]).astype(BF16)
        o_ref[...] = x

    xn = xn_ref[...]
    a = _dot(xn, wg_ref[...])
    b = _dot(xn, wu_ref[...])
    h = (a * jax.nn.sigmoid(a) * b).astype(BF16)
    o_ref[...] += _dot(h, wd_ref[...])

    if final_norm:
        @pl.when(j == pl.num_programs(1) - 1)
        def _():
            o_ref[...] = _rms(o_ref[...], gf_ref[...])


def ffn(x, g, wg, wu, wd, layer, g_final=None):
    m, d = x.shape
    f = wg.shape[2]
    tm, tf = min(CFG["ffn_tm"], m), min(CFG["ffn_tf"], f)
    final_norm = g_final is not None
    gf = (g_final if final_norm else g).reshape(1, d)
    return pl.pallas_call(
        functools.partial(_ffn_kernel, final_norm=final_norm),
        grid=(m // tm, f // tf),
        in_specs=[pl.BlockSpec((tm, d), lambda i, j: (i, 0)),
                  pl.BlockSpec((1, d), lambda i, j: (0, 0)),
                  pl.BlockSpec((None, d, tf), lambda i, j: (layer, 0, j)),
                  pl.BlockSpec((None, d, tf), lambda i, j: (layer, 0, j)),
                  pl.BlockSpec((None, tf, d), lambda i, j: (layer, j, 0)),
                  pl.BlockSpec((1, d), lambda i, j: (0, 0))],
        out_specs=pl.BlockSpec((tm, d), lambda i, j: (i, 0)),
        out_shape=jax.ShapeDtypeStruct((m, d), F32),
        scratch_shapes=[pltpu.VMEM((tm, d), BF16)],
        compiler_params=_cp(("parallel", "arbitrary")),
        name="ffn",
    )(x, g.reshape(1, d), wg, wu, wd, gf)


def _hgrn_levels(c):
    lv, h = [], c // 2
    while h >= 8:
        lv.append(h)
        h //= 2
    return lv


def _bcast_rows(g, rows, rep):
    return jnp.concatenate([jnp.broadcast_to(g[r:r + 1, :], (rep, g.shape[1])) for r in rows], axis=0)


def _hgrn_kernel(q_ref, f_ref, v_ref, gt_ref, lbl_ref, gain_ref, s0_ref, l_ref, og_ref, st_ref, stt_ref,
                 *, c, n_sub, hb, layer):
    cb = pl.program_id(2)
    dk = HG_DK

    @pl.when(cb == 0)
    def _():
        for h in range(hb):
            stt_ref[h] = s0_ref[0, h].T

    lg = lbl_ref[...]
    e = jnp.exp(lg - jnp.max(lg, axis=0, keepdims=True))
    p = e / jnp.sum(e, axis=0, keepdims=True)
    lb = p[0:1, :]
    for i in range(1, layer + 1):
        lb = lb + p[i:i + 1, :]
    lb = jnp.maximum(lb - p[0:1, :], 0.0)

    ti = lax.broadcasted_iota(jnp.int32, (c, c), 0)
    si = lax.broadcasted_iota(jnp.int32, (c, c), 1)
    levels = _hgrn_levels(c)
    masks = []
    for hh in levels:
        sh = int(math.log2(2 * hh))
        masks.append(((ti >> sh) == (si >> sh)) & ((ti & (2 * hh - 1)) >= hh) & ((si & (2 * hh - 1)) < hh))
    mask_loc = ((ti >> 3) == (si >> 3)) & (si <= ti)
    lmat = l_ref[...]
    gain = gain_ref[...]

    def chunk(ci, carry):
        r0 = pl.multiple_of(ci * c, c)
        fp = f_ref[pl.ds(r0, c), :]
        f = lb + (1.0 - lb) * jax.nn.sigmoid(fp)
        g = jnp.log2(jnp.maximum(f, F_MIN))
        kk = 1.0 - f
        g_hi = g.astype(BF16)
        r1 = g - g_hi.astype(F32)
        g_mid = r1.astype(BF16)
        g_lo = (r1 - g_mid.astype(F32)).astype(BF16)
        gc = _dot(lmat, g_hi) + _dot(lmat, g_mid) + _dot(lmat, g_lo)
        qa = q_ref[pl.ds(r0, c), :]
        va = v_ref[pl.ds(r0, c), :]
        ga = gt_ref[pl.ds(r0, c), :]
        for h in range(hb):
            sl = slice(h * dk, (h + 1) * dk)
            gh, qh, kh = gc[:, sl], qa[:, sl], kk[:, sl]
            vh = va[:, sl].astype(BF16)
            stt = stt_ref[h]
            inter = _dot_nt((qh * jnp.exp2(gh)).astype(BF16), stt.astype(BF16))
            g_last = gh[c - 1:c, :]
            k_dec = (kh * jnp.exp2(g_last - gh)).astype(BF16)
            stt_ref[h] = stt * jnp.exp2(g_last) + lax.dot_general(
                vh, k_dec, (((0,), (0,)), ((), ())), preferred_element_type=F32)
            q16, k16 = qh.astype(BF16), kh.astype(BF16)
            g_loc = _bcast_rows(gh, [8 * b + 3 for b in range(c // 8)], 8)
            sc = jnp.where(mask_loc,
                           _dot_nt(q16 * jnp.exp2(gh - g_loc).astype(BF16),
                                   k16 * jnp.exp2(g_loc - gh).astype(BF16)), 0.0)
            for hh, mk in zip(levels, masks):
                g_mid_rows = _bcast_rows(gh, [b * 2 * hh + hh - 1 for b in range(c // (2 * hh))], 2 * hh)
                fac = jnp.exp2(-jnp.abs(gh - g_mid_rows)).astype(BF16)
                sc = jnp.where(mk, _dot_nt(q16 * fac, k16 * fac), sc)
            o = inter + _dot(sc.astype(BF16), vh)
            on = _rms(o, gain)
            gate = ga[:, sl]
            og_ref[pl.ds(r0, c), sl] = (on * (gate * jax.nn.sigmoid(gate))).astype(BF16)
        return carry

    lax.fori_loop(0, n_sub, chunk, 0, unroll=min(CFG["hg_unroll"], n_sub))

    @pl.when(cb == pl.num_programs(2) - 1)
    def _():
        for h in range(hb):
            st_ref[0, h] = stt_ref[h].T


def hgrn_recurrence(proj, lb_logits, gain, s0, batch, t, layer):
    s0, s0_layer = s0
    d = HG_HEADS * HG_DK
    c = min(CFG["hg_chunk"], t)
    rows = min(CFG["hg_rows"], t)
    hb = CFG["hg_heads"]
    w = hb * HG_DK
    ncb = t // rows
    n_layers = lb_logits.shape[0]
    lmat = jnp.asarray(np.tril(np.ones((c, c), np.float32)), BF16)

    def col(sec):
        return lambda b, hg, cb: (b * ncb + cb, sec * (d // w) + hg)

    og, st = pl.pallas_call(
        functools.partial(_hgrn_kernel, c=c, n_sub=rows // c, hb=hb, layer=layer),
        grid=(batch, HG_HEADS // hb, ncb),
        in_specs=[pl.BlockSpec((rows, w), col(0)),
                  pl.BlockSpec((rows, w), col(1)),
                  pl.BlockSpec((rows, w), col(2)),
                  pl.BlockSpec((rows, w), col(3)),
                  pl.BlockSpec((n_layers, w), lambda b, hg, cb: (0, hg)),
                  pl.BlockSpec((1, HG_DK), lambda b, hg, cb: (0, 0)),
                  pl.BlockSpec((None, 1, hb, HG_DK, HG_DK), lambda b, hg, cb: (s0_layer, b, hg, 0, 0)),
                  pl.BlockSpec((c, c), lambda b, hg, cb: (0, 0))],
        out_specs=[pl.BlockSpec((rows, w), lambda b, hg, cb: (b * ncb + cb, hg)),
                   pl.BlockSpec((1, hb, HG_DK, HG_DK), lambda b, hg, cb: (b, hg, 0, 0))],
        out_shape=[jax.ShapeDtypeStruct((batch * t, d), BF16),
                   jax.ShapeDtypeStruct((batch, HG_HEADS, HG_DK, HG_DK), F32)],
        scratch_shapes=[pltpu.VMEM((hb, HG_DK, HG_DK), F32)],
        compiler_params=_cp(("parallel", "parallel", "arbitrary")),
        name="hgrn_recurrence",
    )(proj, proj, proj, proj, lb_logits, gain.reshape(1, HG_DK), s0, lmat)
    return og, st


def _pool_kernel(x_ref, g_ref, hist_ref, w_ref, sc_ref, o_ref, st_ref, buf_ref, *, tm, pos0):
    tb = pl.program_id(1)
    x = x_ref[...]
    xn = _rms(x, g_ref[...])
    hp = POOL_HIST + 1

    @pl.when(tb == 0)
    def _():
        buf_ref[0:hp, :] = hist_ref[0]

    @pl.when(tb > 0)
    def _():
        buf_ref[0:hp, :] = buf_ref[tm:tm + hp, :]

    buf_ref[hp:hp + tm, :] = xn
    st_ref[0] = buf_ref[tm:tm + hp, :]
    pos = pos0 + tb * tm + lax.broadcasted_iota(jnp.int32, (tm, 1), 0)
    gc = x.shape[1] // len(POOL_WINDOWS)
    for gi, win in enumerate(POOL_WINDOWS):
        cs = slice(gi * gc, (gi + 1) * gc)
        ws = xn[:, cs]
        for sft in range(1, win):
            ws = ws + buf_ref[hp - sft:hp - sft + tm, cs]
        cnt = jnp.minimum(win, pos + 1).astype(F32)
        pooled = (ws / cnt - xn[:, cs]).astype(BF16)
        o_ref[:, cs] = x[:, cs] + _dot(pooled, w_ref[gi]) * sc_ref[:, cs]


def pool_mixer(x, g, hist, w, scale, batch, t, pos0):
    d = x.shape[1]
    tm = min(CFG["pool_tm"], t)
    nb = t // tm
    hp = POOL_HIST + 1
    hist16 = jnp.concatenate([jnp.zeros((batch, 1, d), F32), hist], axis=1)
    ng, gc = w.shape[0], w.shape[1]
    return pl.pallas_call(
        functools.partial(_pool_kernel, tm=tm, pos0=pos0),
        grid=(batch, nb),
        in_specs=[pl.BlockSpec((tm, d), lambda b, i: (b * nb + i, 0)),
                  pl.BlockSpec((1, d), lambda b, i: (0, 0)),
                  pl.BlockSpec((1, hp, d), lambda b, i: (b, 0, 0)),
                  pl.BlockSpec((ng, gc, gc), lambda b, i: (0, 0, 0)),
                  pl.BlockSpec((1, d), lambda b, i: (0, 0))],
        out_specs=[pl.BlockSpec((tm, d), lambda b, i: (b * nb + i, 0)),
                   pl.BlockSpec((1, hp, d), lambda b, i: (b, 0, 0))],
        out_shape=[jax.ShapeDtypeStruct(x.shape, F32),
                   jax.ShapeDtypeStruct((batch, hp, d), F32)],
        scratch_shapes=[pltpu.VMEM((tm + hp, d), F32)],
        compiler_params=_cp(("parallel", "arbitrary")),
        name="pool_mixer",
    )(x, g.reshape(1, d), hist16, w, scale.reshape(1, d))


def _rel_bucket_np(rel):
    nb = REL_BUCKETS // 2
    max_exact = nb // 2
    n = np.abs(rel)
    large = max_exact + (np.log(np.maximum(n, max_exact).astype(np.float64) / max_exact)
                         / math.log(REL_MAX_DIST / max_exact) * (nb - max_exact)).astype(np.int64)
    large = np.minimum(large, nb - 1)
    return (np.where(rel > 0, nb, 0) + np.where(n < max_exact, n, large)).astype(np.int32)


def _bucket_tile(q_pos, k_pos):
    q_pos, k_pos = np.asarray(q_pos)[:, None], np.asarray(k_pos)[None, :]
    b = _rel_bucket_np(k_pos - q_pos)
    return np.where((k_pos // ATT_CHUNK) <= (q_pos // ATT_CHUNK), b, -1).astype(np.int32)


def _bias_kernel(tab_ref, bk_ref, o_ref):
    h = pl.program_id(0)
    bk = bk_ref[...]
    bias = jnp.full(bk.shape, MASK_VALUE, F32)
    for b in range(REL_BUCKETS):
        bias = jnp.where(bk == b, tab_ref[b, h] * LOG2E, bias)
    o_ref[0] = bias


def bias_tiles(table, bucket):
    r, c = bucket.shape
    nh = table.shape[1]
    return pl.pallas_call(
        _bias_kernel,
        grid=(nh,),
        in_specs=[pl.BlockSpec(memory_space=pltpu.SMEM),
                  pl.BlockSpec((r, c), lambda h: (0, 0))],
        out_specs=pl.BlockSpec((1, r, c), lambda h: (h, 0, 0)),
        out_shape=jax.ShapeDtypeStruct((nh, r, c), F32),
        compiler_params=_cp(("arbitrary",)),
        name="bias_tiles",
    )(table, jnp.asarray(bucket))


def _lambda(lq1, lk1, lq2, lk2, lambda_init):
    return (jnp.exp(jnp.sum(lq1 * lk1, axis=-1, keepdims=True))
            - jnp.exp(jnp.sum(lq2 * lk2, axis=-1, keepdims=True)) + lambda_init)


def _softmax_step(q, k, v, bias, m_ref, l_ref, acc_ref, idx):
    s = _dot_nt(q, k) + bias
    m_prev = m_ref[idx]
    m_new = jnp.maximum(m_prev, jnp.max(s, axis=-1, keepdims=True))
    p = jnp.exp2(s - m_new)
    alpha = jnp.exp2(m_prev - m_new)
    l_ref[idx] = alpha * l_ref[idx] + jnp.sum(p, axis=-1, keepdims=True)
    acc_ref[idx] = alpha * acc_ref[idx] + _dot(p.astype(BF16), v)
    m_ref[idx] = m_new


def _flash_kernel(q_ref, k_ref, vt_ref, bias_ref, lam_ref, gain_ref, o_ref,
                  m_ref, acc_ref, sa_ref, sb_ref, pa_ref, pb_ref, ala_ref, alb_ref, mxa_ref, mxb_ref,
                  *, tq, tk, lambda_init):
    i = pl.program_id(1)
    dh = DA_DH
    m_ref[...] = jnp.full(m_ref.shape, -jnp.inf, F32)
    acc_ref[...] = jnp.zeros(acc_ref.shape, F32)
    q = q_ref[...]

    nkb = vt_ref.shape[0]

    def scores(j, s_ref, mx_ref):
        jc = jnp.minimum(j, nkb - 1)
        kb = k_ref[pl.ds(pl.multiple_of(jc * tk, tk), tk), :]
        bias = bias_ref[0, jnp.clip(j - (i - 2), 0, 3)]
        for c in range(2):
            s = _dot_nt(kb[:, c * dh:(c + 1) * dh], q[:, c * dh:(c + 1) * dh]) + bias
            s_ref[c] = s
            mx_ref[c] = jnp.max(s, axis=0, keepdims=True)

    def softmax(j, s_ref, mx_ref, p_ref, al_ref):
        for c in range(2):
            m_prev = m_ref[c]
            m_new = jnp.maximum(m_prev, mx_ref[c])
            p_ref[c] = jnp.exp2(s_ref[c] - m_new).astype(BF16)
            al_ref[c] = jnp.exp2(m_prev - m_new)
            m_ref[c] = m_new

    ones_rows = jnp.ones((ONES_ROWS, tk), BF16)

    def values(j, p_ref, al_ref):
        vt = jnp.concatenate([vt_ref[jnp.minimum(j, nkb - 1)], ones_rows], axis=0)
        for c in range(2):
            acc_ref[c] = al_ref[c] * acc_ref[c] + _dot(vt, p_ref[c])

    scores(0, sa_ref, mxa_ref)
    scores(1, sb_ref, mxb_ref)
    softmax(0, sa_ref, mxa_ref, pa_ref, ala_ref)

    def pair(t, carry):
        j = 2 * t
        scores(j + 2, sa_ref, mxa_ref)
        softmax(j + 1, sb_ref, mxb_ref, pb_ref, alb_ref)
        values(j, pa_ref, ala_ref)
        scores(j + 3, sb_ref, mxb_ref)
        softmax(j + 2, sa_ref, mxa_ref, pa_ref, ala_ref)
        values(j + 1, pb_ref, alb_ref)
        return carry

    lax.fori_loop(0, (i + 2) // 2, pair, 0)

    lam = _lambda(lam_ref[0:1, :], lam_ref[1:2, :], lam_ref[2:3, :], lam_ref[3:4, :], lambda_init)
    hw = 2 * dh
    o = (acc_ref[0, 0:hw, :] / acc_ref[0, hw:hw + 1, :]
         - lam * (acc_ref[1, 0:hw, :] / acc_ref[1, hw:hw + 1, :]))
    on = o * lax.rsqrt(jnp.mean(o * o, axis=0, keepdims=True) + EPS) * (1.0 - lambda_init)
    o_ref[...] = (on.T * gain_ref[...]).astype(BF16)


def flash_diff_attention(q, k, vt, table, lam_params, gain, lambda_init):
    t, d = q.shape
    tq, tk = min(CFG["fa_tq"], t), min(CFG["fa_tk"], t)
    assert tq == tk and tk % ATT_CHUNK == 0 and vt.shape == (t // tk, d, tk)
    hw = 2 * DA_DH
    nh = d // hw
    far = _rel_bucket_np(-np.arange(tk + 1, max(t, tk + 2)))
    far_bucket = int(far[0])
    assert (far == far_bucket).all()
    near = _bucket_tile(tk + np.arange(tq), np.arange(2 * tk)).T
    bucket = np.concatenate([np.full((tk, tq), far_bucket, np.int32), near,
                             np.full((tk, tq), -1, np.int32)], axis=0)
    bank = bias_tiles(table, bucket).reshape(nh, 4, tk, tq)
    once = pl.Buffered(1)
    return pl.pallas_call(
        functools.partial(_flash_kernel, tq=tq, tk=tk, lambda_init=lambda_init),
        grid=(nh, t // tq),
        in_specs=[pl.BlockSpec((tq, hw), lambda h, i: (i, h)),
                  pl.BlockSpec((t, hw), lambda h, i: (0, h), pipeline_mode=once),
                  pl.BlockSpec((t // tk, hw, tk), lambda h, i: (0, h, 0), pipeline_mode=once),
                  pl.BlockSpec((1, 4, tk, tq), lambda h, i: (h, 0, 0, 0), pipeline_mode=once),
                  pl.BlockSpec((4, DA_DH), lambda h, i: (0, 0)),
                  pl.BlockSpec((1, hw), lambda h, i: (0, 0))],
        out_specs=pl.BlockSpec((tq, hw), lambda h, i: (i, h)),
        out_shape=jax.ShapeDtypeStruct((t, d), BF16),
        scratch_shapes=[pltpu.VMEM((2, 1, tq), F32),
                        pltpu.VMEM((2, hw + ONES_ROWS, tq), F32),
                        pltpu.VMEM((2, tk, tq), F32), pltpu.VMEM((2, tk, tq), F32),
                        pltpu.VMEM((2, tk, tq), BF16), pltpu.VMEM((2, tk, tq), BF16),
                        pltpu.VMEM((2, 1, tq), F32), pltpu.VMEM((2, 1, tq), F32),
                        pltpu.VMEM((2, 1, tq), F32), pltpu.VMEM((2, 1, tq), F32)],
        compiler_params=_cp(("parallel", "arbitrary")),
        name="flash_diff_attention",
    )(q, k, vt, bank, lam_params, gain.reshape(1, hw))


def _sample_attn_kernel(tab_ref, q_ref, ck_ref, cvl_ref, cvh_ref, kn_ref, vn_ref, blast_ref, bnew_ref, lam_ref,
                        gain_ref, o_ref, m_ref, l_ref, acc_ref, *, tk, far_bucket, lambda_init):
    j = pl.program_id(1)
    nj = pl.num_programs(1)
    dh = DA_DH
    hw = 2 * dh
    nh = q_ref.shape[1] // hw
    ts = q_ref.shape[0]

    @pl.when(j == 0)
    def _():
        m_ref[...] = jnp.full(m_ref.shape, -jnp.inf, F32)
        l_ref[...] = jnp.zeros(l_ref.shape, F32)
        acc_ref[...] = jnp.zeros(acc_ref.shape, F32)

    q = q_ref[...]

    def block(k_of, v_of, bias_fn):
        for h in range(nh):
            vb = v_of(h)
            for c in range(2):
                cs = slice((2 * h + c) * dh, (2 * h + c + 1) * dh)
                _softmax_step(q[:, cs], k_of(2 * h + c), vb, bias_fn(h), m_ref, l_ref, acc_ref, 2 * h + c)

    def cache_k_of(hc):
        return ck_ref[0, pl.ds(hc, tk, stride=2 * nh), :].astype(BF16)

    def cache_v_of(h):
        return jnp.concatenate([cvl_ref[0, pl.ds(h, tk, stride=nh), :],
                                cvh_ref[0, pl.ds(h, tk, stride=nh), :]], axis=1).astype(BF16)

    @pl.when(j < nj - 1)
    def _():
        block(cache_k_of, cache_v_of, lambda h: tab_ref[far_bucket, h] * LOG2E)

    @pl.when(j == nj - 1)
    def _():
        block(cache_k_of, cache_v_of, lambda h: blast_ref[h])
        block(lambda hc: kn_ref[:, hc * dh:(hc + 1) * dh], lambda h: vn_ref[:, h * hw:(h + 1) * hw],
              lambda h: bnew_ref[h][:, :ts])
        lam = _lambda(lam_ref[0:1, :], lam_ref[1:2, :], lam_ref[2:3, :], lam_ref[3:4, :], lambda_init)
        for h in range(nh):
            o = acc_ref[2 * h] / l_ref[2 * h] - lam * (acc_ref[2 * h + 1] / l_ref[2 * h + 1])
            o_ref[:, h * hw:(h + 1) * hw] = (_rms(o, gain_ref[...]) * (1.0 - lambda_init)).astype(BF16)


def sample_diff_attention(q, k_new, v_new, cache_k, cache_v, layer, table, lam_params, gain, lambda_init):
    n_layers, batch, past, nhc, dh = cache_k.shape
    d = nhc * dh
    ts = q.shape[0] // batch
    hw = 2 * DA_DH
    nh = d // hw
    tk = min(CFG["sa_tk"], past)
    nkb = past // tk
    q_pos = past + np.arange(ts)
    blast = bias_tiles(table, _bucket_tile(q_pos, past - tk + np.arange(tk)))
    bnew = bias_tiles(table, _bucket_tile(q_pos, past + np.arange(128)))
    if nkb > 1:
        far = _bucket_tile(q_pos, np.arange(past - tk))
        far_bucket = int(far[0, 0])
        assert (far == far_bucket).all()
    else:
        far_bucket = 0
    cache_v_rows = cache_v.reshape(n_layers, batch, past * nh, hw)
    return pl.pallas_call(
        functools.partial(_sample_attn_kernel, tk=tk, far_bucket=far_bucket, lambda_init=lambda_init),
        grid=(batch, nkb),
        in_specs=[pl.BlockSpec(memory_space=pltpu.SMEM),
                  pl.BlockSpec((ts, d), lambda b, j: (b, 0)),
                  pl.BlockSpec((None, 1, tk * nhc, dh), lambda b, j: (layer, b, j, 0)),
                  pl.BlockSpec((None, 1, tk * nh, dh), lambda b, j: (layer, b, j, 0)),
                  pl.BlockSpec((None, 1, tk * nh, dh), lambda b, j: (layer, b, j, 1)),
                  pl.BlockSpec((ts, d), lambda b, j: (b, 0)),
                  pl.BlockSpec((ts, d), lambda b, j: (b, 0)),
                  pl.BlockSpec((nh, ts, tk), lambda b, j: (0, 0, 0)),
                  pl.BlockSpec((nh, ts, 128), lambda b, j: (0, 0, 0)),
                  pl.BlockSpec((4, DA_DH), lambda b, j: (0, 0)),
                  pl.BlockSpec((1, hw), lambda b, j: (0, 0))],
        out_specs=pl.BlockSpec((ts, d), lambda b, j: (b, 0)),
        out_shape=jax.ShapeDtypeStruct((batch * ts, d), BF16),
        scratch_shapes=[pltpu.VMEM((2 * nh, ts, 1), F32), pltpu.VMEM((2 * nh, ts, 1), F32),
                        pltpu.VMEM((2 * nh, ts, hw), F32)],
        compiler_params=_cp(("parallel", "arbitrary")),
        name="sample_diff_attention",
    )(table, q, cache_k.reshape(n_layers, batch, past * nhc, dh), cache_v_rows, cache_v_rows,
      k_new, v_new, blast, bnew, lam_params, gain.reshape(1, hw))


def kernel(x_prompt, x_sample, state_hgrn, state_pool, cache_k, cache_v, norm_mix, norm_ffn, norm_final, hgrn_w_q, hgrn_w_f, hgrn_w_i, hgrn_w_g, hgrn_w_o, hgrn_lb_logits, hgrn_norm_gain, pool_w, pool_scale, attn_w_q, attn_w_k, attn_w_v, attn_w_o, attn_lambda_q1, attn_lambda_k1, attn_lambda_q2, attn_lambda_k2, attn_subln_gain, rel_bias_table, ffn_w_gate, ffn_w_up, ffn_w_down):
    bp, tp, d = x_prompt.shape
    bs, ts, _ = x_sample.shape
    past = cache_k.shape[2]
    depth = norm_mix.shape[0]
    assert bp == 1
    bf = lambda a: a.astype(BF16)
    xs = [x_prompt.reshape(bp * tp, d), x_sample.reshape(bs * ts, d)]
    dims = [(bp, tp), (bs, ts)]
    hg, pool_st, k_out, v_out = [[], []], [[], []], [[], []], [[], []]
    w_g, w_u, w_d = bf(ffn_w_gate), bf(ffn_w_up), bf(ffn_w_down)
    zero_state = jnp.zeros((1, bp, HG_HEADS, HG_DK, HG_DK), F32)

    for i in range(depth):
        m, j = i % N_MIXERS, i // N_MIXERS
        if m == 0:
            w_cat = bf(jnp.concatenate([hgrn_w_q[j], hgrn_w_f[j], hgrn_w_i[j], hgrn_w_g[j]], axis=1))
            w_o = bf(hgrn_w_o[j])
            s0s = [(zero_state, 0), (state_hgrn, j)]
            for r in range(2):
                b, t = dims[r]
                proj, = norm_mm(xs[r], norm_mix[i], w_cat, [F32])
                og, st = hgrn_recurrence(proj, hgrn_lb_logits, hgrn_norm_gain[j], s0s[r], b, t, j)
                xs[r] = mm_res(og, w_o, xs[r])
                hg[r].append(st)
        elif m == 1:
            w_p = bf(pool_w[j])
            hists = [jnp.zeros((bp, POOL_HIST, d), F32), state_pool[j]]
            for r in range(2):
                b, t = dims[r]
                xs[r], st = pool_mixer(xs[r], norm_mix[i], hists[r], w_p, pool_scale[j], b, t, (0, past)[r])
                pool_st[r].append(st[:, 1:, :])
        else:
            lambda_init = 0.8 - 0.6 * math.exp(-0.3 * i)
            w_qkv = bf(jnp.concatenate([attn_w_q[j], attn_w_k[j], attn_w_v[j]], axis=1))
            w_o = bf(attn_w_o[j])
            lam_params = jnp.stack([attn_lambda_q1[j], attn_lambda_k1[j], attn_lambda_q2[j], attn_lambda_k2[j]])
            for r in range(2):
                b, t = dims[r]
                q, k32, k16, v32, v16 = qkv_proj(xs[r], norm_mix[i], w_qkv, DA_DH ** -0.5 * LOG2E, r == 0,
                                                 CFG["fa_tk"])
                if r == 0:
                    o = flash_diff_attention(q, k16, v16, rel_bias_table, lam_params, attn_subln_gain[j],
                                             lambda_init)
                else:
                    o = sample_diff_attention(q, k16, v16, cache_k, cache_v, j, rel_bias_table, lam_params,
                                              attn_subln_gain[j], lambda_init)
                xs[r] = mm_res(o, w_o, xs[r])
                k_out[r].append(k32.reshape(b, t, 2 * DA_HEADS, DA_DH))
                v_out[r].append(v32.reshape(b, t, DA_HEADS, 2 * DA_DH))
        g_final = norm_final if i == depth - 1 else None
        for r in range(2):
            xs[r] = ffn(xs[r], norm_ffn[i], w_g, w_u, w_d, i, g_final)

    return (xs[0].reshape(bp, tp, d), xs[1].reshape(bs, ts, d),
            jnp.stack(hg[0]), jnp.stack(hg[1]), jnp.stack(pool_st[0]), jnp.stack(pool_st[1]),
            jnp.stack(k_out[0]), jnp.stack(v_out[0]), jnp.stack(k_out[1]), jnp.stack(v_out[1]))
```

```python
import functools
import math

import numpy as np
import jax
import jax.numpy as jnp
from jax import lax
from jax.experimental import pallas as pl
from jax.experimental.pallas import tpu as pltpu

F32 = jnp.float32
BF16 = jnp.bfloat16

EPS = 1e-6
LOG2E = math.log2(math.e)
F_MIN = 1e-6
MASK_VALUE = -1e30
HG_HEADS = 16
HG_DK = 128
POOL_WINDOWS = (2, 4, 8, 16)
POOL_HIST = 15
DA_HEADS = 8
DA_DH = 128
ATT_CHUNK = 64
REL_BUCKETS = 32
REL_MAX_DIST = 128
N_MIXERS = 3

CFG = dict(
    mm_tm=1024, mm_tn=1024,
    ffn_tm=512, ffn_tf=512,
    pool_tm=512,
    hg_chunk=256, hg_rows=512, hg_heads=4, hg_unroll=2,
    fa_tq=512, fa_tk=512,
    sa_tk=512,
    vmem=56 * 1024 * 1024,
)


def _cp(sem):
    return pltpu.CompilerParams(dimension_semantics=sem, vmem_limit_bytes=CFG["vmem"])


def _rms(x, g):
    return x * lax.rsqrt(jnp.mean(x * x, axis=-1, keepdims=True) + EPS) * g


def _dot(a, b):
    return jnp.dot(a, b, preferred_element_type=F32)


def _dot_nt(a, b):
    return lax.dot_general(a, b, (((1,), (1,)), ((), ())), preferred_element_type=F32)


def _norm_mm_kernel(x_ref, g_ref, w_ref, o_ref, xn_ref):
    @pl.when(pl.program_id(1) == 0)
    def _():
        xn_ref[...] = _rms(x_ref[...], g_ref[...]).astype(BF16)

    o_ref[...] = _dot(xn_ref[...], w_ref[...])


def norm_mm(x, g, w):
    m, d = x.shape
    n = w.shape[1]
    tm, tn = min(CFG["mm_tm"], m), min(CFG["mm_tn"], n)
    return pl.pallas_call(
        _norm_mm_kernel,
        grid=(m // tm, n // tn),
        in_specs=[pl.BlockSpec((tm, d), lambda i, j: (i, 0)),
                  pl.BlockSpec((1, d), lambda i, j: (0, 0)),
                  pl.BlockSpec((d, tn), lambda i, j: (0, j))],
        out_specs=pl.BlockSpec((tm, tn), lambda i, j: (i, j)),
        out_shape=jax.ShapeDtypeStruct((m, n), F32),
        scratch_shapes=[pltpu.VMEM((tm, d), BF16)],
        compiler_params=_cp(("parallel", "arbitrary")),
        name="norm_mm",
    )(x, g.reshape(1, d), w)


def _qkv_kernel(x_ref, g_ref, w_ref, q_ref, k32_ref, k16_ref, v32_ref, v16_ref, xn_ref, *, nq, q_scale, v_transposed):
    j = pl.program_id(1)

    @pl.when(j == 0)
    def _():
        xn_ref[...] = _rms(x_ref[...], g_ref[...]).astype(BF16)

    y = _dot(xn_ref[...], w_ref[...])

    @pl.when(j < nq)
    def _():
        q_ref[...] = (y * q_scale).astype(BF16)

    @pl.when((j >= nq) & (j < 2 * nq))
    def _():
        k32_ref[...] = y
        k16_ref[...] = y.astype(BF16)

    @pl.when(j >= 2 * nq)
    def _():
        v32_ref[...] = y
        if v_transposed:
            v16_ref[0] = y.T.astype(BF16)
        else:
            v16_ref[...] = y.astype(BF16)


def qkv_proj(x, g, w, q_scale, v_transposed, tm):
    m, d = x.shape
    n = w.shape[1] // 3
    tm, tn = min(tm, m), min(CFG["mm_tn"], n)
    nq = n // tn
    spec = lambda off: pl.BlockSpec((tm, tn), lambda i, j: (i, jnp.clip(j - off, 0, nq - 1)))
    if v_transposed:
        v16_spec = pl.BlockSpec((1, tn, tm), lambda i, j: (i, jnp.clip(j - 2 * nq, 0, nq - 1), 0))
        v16_shape = jax.ShapeDtypeStruct((m // tm, n, tm), BF16)
    else:
        v16_spec, v16_shape = spec(2 * nq), jax.ShapeDtypeStruct((m, n), BF16)
    return pl.pallas_call(
        functools.partial(_qkv_kernel, nq=nq, q_scale=q_scale, v_transposed=v_transposed),
        grid=(m // tm, 3 * nq),
        in_specs=[pl.BlockSpec((tm, d), lambda i, j: (i, 0)),
                  pl.BlockSpec((1, d), lambda i, j: (0, 0)),
                  pl.BlockSpec((d, tn), lambda i, j: (0, j))],
        out_specs=[spec(0), spec(nq), spec(nq), spec(2 * nq), v16_spec],
        out_shape=[jax.ShapeDtypeStruct((m, n), BF16), jax.ShapeDtypeStruct((m, n), F32),
                   jax.ShapeDtypeStruct((m, n), BF16), jax.ShapeDtypeStruct((m, n), F32), v16_shape],
        scratch_shapes=[pltpu.VMEM((tm, d), BF16)],
        compiler_params=_cp(("parallel", "arbitrary")),
        name="qkv_proj",
    )(x, g.reshape(1, d), w)


def _mm_res_kernel(a_ref, w_ref, r_ref, o_ref):
    o_ref[...] = r_ref[...] + _dot(a_ref[...], w_ref[...])


def mm_res(a, w, r):
    m, k = a.shape
    n = w.shape[1]
    tm, tn = min(CFG["mm_tm"], m), min(CFG["mm_tn"], n)
    return pl.pallas_call(
        _mm_res_kernel,
        grid=(m // tm, n // tn),
        in_specs=[pl.BlockSpec((tm, k), lambda i, j: (i, 0)),
                  pl.BlockSpec((k, tn), lambda i, j: (0, j)),
                  pl.BlockSpec((tm, tn), lambda i, j: (i, j))],
        out_specs=pl.BlockSpec((tm, tn), lambda i, j: (i, j)),
        out_shape=jax.ShapeDtypeStruct((m, n), F32),
        compiler_params=_cp(("parallel", "arbitrary")),
        name="mm_res",
    )(a, w, r)


def _ffn_kernel(x_ref, g_ref, wg_ref, wu_ref, wd_ref, gf_ref, o_ref, xn_ref, *, final_norm):
    j = pl.program_id(1)

    @pl.when(j == 0)
    def _():
        x = x_ref[...]
        xn_ref[...] = _rms(x, g_ref[...]).astype(BF16)
        o_ref[...] = x

    xn = xn_ref[...]
    a = _dot(xn, wg_ref[...])
    b = _dot(xn, wu_ref[...])
    h = (a * jax.nn.sigmoid(a) * b).astype(BF16)
    o_ref[...] += _dot(h, wd_ref[...])

    if final_norm:
        @pl.when(j == pl.num_programs(1) - 1)
        def _():
            o_ref[...] = _rms(o_ref[...], gf_ref[...])


def ffn(x, g, wg, wu, wd, layer, g_final=None):
    m, d = x.shape
    f = wg.shape[2]
    tm, tf = min(CFG["ffn_tm"], m), min(CFG["ffn_tf"], f)
    final_norm = g_final is not None
    gf = (g_final if final_norm else g).reshape(1, d)
    return pl.pallas_call(
        functools.partial(_ffn_kernel, final_norm=final_norm),
        grid=(m // tm, f // tf),
        in_specs=[pl.BlockSpec((tm, d), lambda i, j: (i, 0)),
                  pl.BlockSpec((1, d), lambda i, j: (0, 0)),
                  pl.BlockSpec((None, d, tf), lambda i, j: (layer, 0, j)),
                  pl.BlockSpec((None, d, tf), lambda i, j: (layer, 0, j)),
                  pl.BlockSpec((None, tf, d), lambda i, j: (layer, j, 0)),
                  pl.BlockSpec((1, d), lambda i, j: (0, 0))],
        out_specs=pl.BlockSpec((tm, d), lambda i, j: (i, 0)),
        out_shape=jax.ShapeDtypeStruct((m, d), F32),
        scratch_shapes=[pltpu.VMEM((tm, d), BF16)],
        compiler_params=_cp(("parallel", "arbitrary")),
        name="ffn",
    )(x, g.reshape(1, d), wg, wu, wd, gf)


def _hgrn_levels(c):
    lv, h = [], c // 2
    while h >= 8:
        lv.append(h)
        h //= 2
    return lv


def _bcast_rows(g, rows, rep):
    return jnp.concatenate([jnp.broadcast_to(g[r:r + 1, :], (rep, g.shape[1])) for r in rows], axis=0)


def _hgrn_kernel(q_ref, f_ref, v_ref, gt_ref, lbl_ref, gain_ref, s0_ref, l_ref, og_ref, st_ref, stt_ref,
                 *, c, n_sub, hb, layer):
    cb = pl.program_id(2)
    dk = HG_DK

    @pl.when(cb == 0)
    def _():
        for h in range(hb):
            stt_ref[h] = s0_ref[0, h].T

    lg = lbl_ref[...]
    e = jnp.exp(lg - jnp.max(lg, axis=0, keepdims=True))
    p = e / jnp.sum(e, axis=0, keepdims=True)
    lb = p[0:1, :]
    for i in range(1, layer + 1):
        lb = lb + p[i:i + 1, :]
    lb = jnp.maximum(lb - p[0:1, :], 0.0)

    ti = lax.broadcasted_iota(jnp.int32, (c, c), 0)
    si = lax.broadcasted_iota(jnp.int32, (c, c), 1)
    levels = _hgrn_levels(c)
    masks = []
    for hh in levels:
        sh = int(math.log2(2 * hh))
        masks.append(((ti >> sh) == (si >> sh)) & ((ti & (2 * hh - 1)) >= hh) & ((si & (2 * hh - 1)) < hh))
    mask_loc = ((ti >> 3) == (si >> 3)) & (si <= ti)
    lmat = l_ref[...]
    gain = gain_ref[...]

    def chunk(ci, carry):
        r0 = pl.multiple_of(ci * c, c)
        fp = f_ref[pl.ds(r0, c), :]
        f = lb + (1.0 - lb) * jax.nn.sigmoid(fp)
        g = jnp.log2(jnp.maximum(f, F_MIN))
        kk = 1.0 - f
        g_hi = g.astype(BF16)
        r1 = g - g_hi.astype(F32)
        g_mid = r1.astype(BF16)
        g_lo = (r1 - g_mid.astype(F32)).astype(BF16)
        gc = _dot(lmat, g_hi) + _dot(lmat, g_mid) + _dot(lmat, g_lo)
        qa = q_ref[pl.ds(r0, c), :]
        va = v_ref[pl.ds(r0, c), :]
        ga = gt_ref[pl.ds(r0, c), :]
        for h in range(hb):
            sl = slice(h * dk, (h + 1) * dk)
            gh, qh, kh = gc[:, sl], qa[:, sl], kk[:, sl]
            vh = va[:, sl].astype(BF16)
            stt = stt_ref[h]
            inter = _dot_nt((qh * jnp.exp2(gh)).astype(BF16), stt.astype(BF16))
            g_last = gh[c - 1:c, :]
            k_dec = (kh * jnp.exp2(g_last - gh)).astype(BF16)
            stt_ref[h] = stt * jnp.exp2(g_last) + lax.dot_general(
                vh, k_dec, (((0,), (0,)), ((), ())), preferred_element_type=F32)
            q16, k16 = qh.astype(BF16), kh.astype(BF16)
            g_loc = _bcast_rows(gh, [8 * b + 3 for b in range(c // 8)], 8)
            sc = jnp.where(mask_loc,
                           _dot_nt(q16 * jnp.exp2(gh - g_loc).astype(BF16),
                                   k16 * jnp.exp2(g_loc - gh).astype(BF16)), 0.0)
            for hh, mk in zip(levels, masks):
                g_mid_rows = _bcast_rows(gh, [b * 2 * hh + hh - 1 for b in range(c // (2 * hh))], 2 * hh)
                fac = jnp.exp2(-jnp.abs(gh - g_mid_rows)).astype(BF16)
                sc = jnp.where(mk, _dot_nt(q16 * fac, k16 * fac), sc)
            o = inter + _dot(sc.astype(BF16), vh)
            on = _rms(o, gain)
            gate = ga[:, sl]
            og_ref[pl.ds(r0, c), sl] = (on * (gate * jax.nn.sigmoid(gate))).astype(BF16)
        return carry

    lax.fori_loop(0, n_sub, chunk, 0, unroll=min(CFG["hg_unroll"], n_sub))

    @pl.when(cb == pl.num_programs(2) - 1)
    def _():
        for h in range(hb):
            st_ref[0, h] = stt_ref[h].T


def hgrn_recurrence(proj, lb_logits, gain, s0, batch, t, layer):
    s0, s0_layer = s0
    d = HG_HEADS * HG_DK
    c = min(CFG["hg_chunk"], t)
    rows = min(CFG["hg_rows"], t)
    hb = CFG["hg_heads"] if t > c else HG_HEADS
    w = hb * HG_DK
    ncb = t // rows
    n_layers = lb_logits.shape[0]
    lmat = jnp.asarray(np.tril(np.ones((c, c), np.float32)), BF16)

    def col(sec):
        return lambda b, hg, cb: (b * ncb + cb, sec * (d // w) + hg)

    og, st = pl.pallas_call(
        functools.partial(_hgrn_kernel, c=c, n_sub=rows // c, hb=hb, layer=layer),
        grid=(batch, HG_HEADS // hb, ncb),
        in_specs=[pl.BlockSpec((rows, w), col(0)),
                  pl.BlockSpec((rows, w), col(1)),
                  pl.BlockSpec((rows, w), col(2)),
                  pl.BlockSpec((rows, w), col(3)),
                  pl.BlockSpec((n_layers, w), lambda b, hg, cb: (0, hg)),
                  pl.BlockSpec((1, HG_DK), lambda b, hg, cb: (0, 0)),
                  pl.BlockSpec((None, 1, hb, HG_DK, HG_DK), lambda b, hg, cb: (s0_layer, b, hg, 0, 0)),
                  pl.BlockSpec((c, c), lambda b, hg, cb: (0, 0))],
        out_specs=[pl.BlockSpec((rows, w), lambda b, hg, cb: (b * ncb + cb, hg)),
                   pl.BlockSpec((1, hb, HG_DK, HG_DK), lambda b, hg, cb: (b, hg, 0, 0))],
        out_shape=[jax.ShapeDtypeStruct((batch * t, d), BF16),
                   jax.ShapeDtypeStruct((batch, HG_HEADS, HG_DK, HG_DK), F32)],
        scratch_shapes=[pltpu.VMEM((hb, HG_DK, HG_DK), F32)],
        compiler_params=_cp(("parallel", "parallel", "arbitrary")),
        name="hgrn_recurrence",
    )(proj, proj, proj, proj, lb_logits, gain.reshape(1, HG_DK), s0, lmat)
    return og, st


def _pool_kernel(x_ref, g_ref, hist_ref, w_ref, sc_ref, o_ref, st_ref, buf_ref, *, tm, pos0):
    tb = pl.program_id(1)
    x = x_ref[...]
    xn = _rms(x, g_ref[...])
    hp = POOL_HIST + 1

    @pl.when(tb == 0)
    def _():
        buf_ref[0:hp, :] = hist_ref[0]

    @pl.when(tb > 0)
    def _():
        buf_ref[0:hp, :] = buf_ref[tm:tm + hp, :]

    buf_ref[hp:hp + tm, :] = xn
    st_ref[0] = buf_ref[tm:tm + hp, :]
    pos = pos0 + tb * tm + lax.broadcasted_iota(jnp.int32, (tm, 1), 0)
    gc = x.shape[1] // len(POOL_WINDOWS)
    for gi, win in enumerate(POOL_WINDOWS):
        cs = slice(gi * gc, (gi + 1) * gc)
        ws = xn[:, cs]
        for sft in range(1, win):
            ws = ws + buf_ref[hp - sft:hp - sft + tm, cs]
        cnt = jnp.minimum(win, pos + 1).astype(F32)
        pooled = (ws / cnt - xn[:, cs]).astype(BF16)
        o_ref[:, cs] = x[:, cs] + _dot(pooled, w_ref[gi]) * sc_ref[:, cs]


def pool_mixer(x, g, hist, w, scale, batch, t, pos0):
    d = x.shape[1]
    tm = min(CFG["pool_tm"], t)
    nb = t // tm
    hp = POOL_HIST + 1
    hist16 = jnp.concatenate([jnp.zeros((batch, 1, d), F32), hist], axis=1)
    ng, gc = w.shape[0], w.shape[1]
    return pl.pallas_call(
        functools.partial(_pool_kernel, tm=tm, pos0=pos0),
        grid=(batch, nb),
        in_specs=[pl.BlockSpec((tm, d), lambda b, i: (b * nb + i, 0)),
                  pl.BlockSpec((1, d), lambda b, i: (0, 0)),
                  pl.BlockSpec((1, hp, d), lambda b, i: (b, 0, 0)),
                  pl.BlockSpec((ng, gc, gc), lambda b, i: (0, 0, 0)),
                  pl.BlockSpec((1, d), lambda b, i: (0, 0))],
        out_specs=[pl.BlockSpec((tm, d), lambda b, i: (b * nb + i, 0)),
                   pl.BlockSpec((1, hp, d), lambda b, i: (b, 0, 0))],
        out_shape=[jax.ShapeDtypeStruct(x.shape, F32),
                   jax.ShapeDtypeStruct((batch, hp, d), F32)],
        scratch_shapes=[pltpu.VMEM((tm + hp, d), F32)],
        compiler_params=_cp(("parallel", "arbitrary")),
        name="pool_mixer",
    )(x, g.reshape(1, d), hist16, w, scale.reshape(1, d))


def _rel_bucket_np(rel):
    nb = REL_BUCKETS // 2
    max_exact = nb // 2
    n = np.abs(rel)
    large = max_exact + (np.log(np.maximum(n, max_exact).astype(np.float64) / max_exact)
                         / math.log(REL_MAX_DIST / max_exact) * (nb - max_exact)).astype(np.int64)
    large = np.minimum(large, nb - 1)
    return (np.where(rel > 0, nb, 0) + np.where(n < max_exact, n, large)).astype(np.int32)


def _bucket_tile(q_pos, k_pos):
    q_pos, k_pos = np.asarray(q_pos)[:, None], np.asarray(k_pos)[None, :]
    b = _rel_bucket_np(k_pos - q_pos)
    return np.where((k_pos // ATT_CHUNK) <= (q_pos // ATT_CHUNK), b, -1).astype(np.int32)


def _bias_kernel(tab_ref, bk_ref, o_ref):
    h = pl.program_id(0)
    bk = bk_ref[...]
    bias = jnp.full(bk.shape, MASK_VALUE, F32)
    for b in range(REL_BUCKETS):
        bias = jnp.where(bk == b, tab_ref[b, h] * LOG2E, bias)
    o_ref[0] = bias


def bias_tiles(table, bucket):
    r, c = bucket.shape
    nh = table.shape[1]
    return pl.pallas_call(
        _bias_kernel,
        grid=(nh,),
        in_specs=[pl.BlockSpec(memory_space=pltpu.SMEM),
                  pl.BlockSpec((r, c), lambda h: (0, 0))],
        out_specs=pl.BlockSpec((1, r, c), lambda h: (h, 0, 0)),
        out_shape=jax.ShapeDtypeStruct((nh, r, c), F32),
        compiler_params=_cp(("arbitrary",)),
        name="bias_tiles",
    )(table, jnp.asarray(bucket))


def _lambda(lq1, lk1, lq2, lk2, lambda_init):
    return (jnp.exp(jnp.sum(lq1 * lk1, axis=-1, keepdims=True))
            - jnp.exp(jnp.sum(lq2 * lk2, axis=-1, keepdims=True)) + lambda_init)


def _softmax_step(q, k, v, bias, m_ref, l_ref, acc_ref, idx):
    s = _dot_nt(q, k) + bias
    m_prev = m_ref[idx]
    m_new = jnp.maximum(m_prev, jnp.max(s, axis=-1, keepdims=True))
    p = jnp.exp2(s - m_new)
    alpha = jnp.exp2(m_prev - m_new)
    l_ref[idx] = alpha * l_ref[idx] + jnp.sum(p, axis=-1, keepdims=True)
    acc_ref[idx] = alpha * acc_ref[idx] + _dot(p.astype(BF16), v)
    m_ref[idx] = m_new


def _flash_kernel(q_ref, k_ref, vt_ref, bias_ref, lam_ref, gain_ref, o_ref,
                  m_ref, l_ref, acc_ref, sa_ref, sb_ref, pa_ref, pb_ref, ala_ref, alb_ref, *, tq, tk, lambda_init):
    i = pl.program_id(1)
    dh = DA_DH
    m_ref[...] = jnp.full(m_ref.shape, -jnp.inf, F32)
    l_ref[...] = jnp.zeros(l_ref.shape, F32)
    acc_ref[...] = jnp.zeros(acc_ref.shape, F32)
    q = q_ref[...]

    nkb = vt_ref.shape[0]

    def scores(j, s_ref):
        jc = jnp.minimum(j, nkb - 1)
        kb = k_ref[pl.ds(pl.multiple_of(jc * tk, tk), tk), :]
        for c in range(2):
            s_ref[c] = _dot_nt(kb[:, c * dh:(c + 1) * dh], q[:, c * dh:(c + 1) * dh])

    def softmax(j, s_ref, p_ref, al_ref):
        bias = bias_ref[0, jnp.clip(j - (i - 2), 0, 3)]
        for c in range(2):
            s = s_ref[c] + bias
            m_prev = m_ref[c]
            m_new = jnp.maximum(m_prev, jnp.max(s, axis=0, keepdims=True))
            p = jnp.exp2(s - m_new)
            alpha = jnp.exp2(m_prev - m_new)
            l_ref[c] = alpha * l_ref[c] + jnp.sum(p, axis=0, keepdims=True)
            m_ref[c] = m_new
            al_ref[c] = alpha
            p_ref[c] = p.astype(BF16)

    def values(j, p_ref, al_ref):
        vt = vt_ref[jnp.minimum(j, nkb - 1)]
        for c in range(2):
            acc_ref[c] = al_ref[c] * acc_ref[c] + _dot(vt, p_ref[c])

    scores(0, sa_ref)
    scores(1, sb_ref)
    softmax(0, sa_ref, pa_ref, ala_ref)

    def pair(t, carry):
        j = 2 * t
        scores(j + 2, sa_ref)
        softmax(j + 1, sb_ref, pb_ref, alb_ref)
        values(j, pa_ref, ala_ref)
        scores(j + 3, sb_ref)
        softmax(j + 2, sa_ref, pa_ref, ala_ref)
        values(j + 1, pb_ref, alb_ref)
        return carry

    lax.fori_loop(0, (i + 2) // 2, pair, 0)

    lam = _lambda(lam_ref[0:1, :], lam_ref[1:2, :], lam_ref[2:3, :], lam_ref[3:4, :], lambda_init)
    o = acc_ref[0] / l_ref[0] - lam * (acc_ref[1] / l_ref[1])
    on = o * lax.rsqrt(jnp.mean(o * o, axis=0, keepdims=True) + EPS) * (1.0 - lambda_init)
    o_ref[...] = (on.T * gain_ref[...]).astype(BF16)


def flash_diff_attention(q, k, vt, table, lam_params, gain, lambda_init):
    t, d = q.shape
    tq, tk = min(CFG["fa_tq"], t), min(CFG["fa_tk"], t)
    assert tq == tk and tk % ATT_CHUNK == 0 and vt.shape == (t // tk, d, tk)
    hw = 2 * DA_DH
    nh = d // hw
    far = _rel_bucket_np(-np.arange(tk + 1, max(t, tk + 2)))
    far_bucket = int(far[0])
    assert (far == far_bucket).all()
    near = _bucket_tile(tk + np.arange(tq), np.arange(2 * tk)).T
    bucket = np.concatenate([np.full((tk, tq), far_bucket, np.int32), near,
                             np.full((tk, tq), -1, np.int32)], axis=0)
    bank = bias_tiles(table, bucket).reshape(nh, 4, tk, tq)
    once = pl.Buffered(1)
    return pl.pallas_call(
        functools.partial(_flash_kernel, tq=tq, tk=tk, lambda_init=lambda_init),
        grid=(nh, t // tq),
        in_specs=[pl.BlockSpec((tq, hw), lambda h, i: (i, h)),
                  pl.BlockSpec((t, hw), lambda h, i: (0, h), pipeline_mode=once),
                  pl.BlockSpec((t // tk, hw, tk), lambda h, i: (0, h, 0), pipeline_mode=once),
                  pl.BlockSpec((1, 4, tk, tq), lambda h, i: (h, 0, 0, 0), pipeline_mode=once),
                  pl.BlockSpec((4, DA_DH), lambda h, i: (0, 0)),
                  pl.BlockSpec((1, hw), lambda h, i: (0, 0))],
        out_specs=pl.BlockSpec((tq, hw), lambda h, i: (i, h)),
        out_shape=jax.ShapeDtypeStruct((t, d), BF16),
        scratch_shapes=[pltpu.VMEM((2, 1, tq), F32), pltpu.VMEM((2, 1, tq), F32),
                        pltpu.VMEM((2, hw, tq), F32),
                        pltpu.VMEM((2, tk, tq), F32), pltpu.VMEM((2, tk, tq), F32),
                        pltpu.VMEM((2, tk, tq), BF16), pltpu.VMEM((2, tk, tq), BF16),
                        pltpu.VMEM((2, 1, tq), F32), pltpu.VMEM((2, 1, tq), F32)],
        compiler_params=_cp(("parallel", "arbitrary")),
        name="flash_diff_attention",
    )(q, k, vt, bank, lam_params, gain.reshape(1, hw))


def _sample_attn_kernel(tab_ref, q_ref, ck_ref, cvl_ref, cvh_ref, kn_ref, vn_ref, blast_ref, bnew_ref, lam_ref,
                        gain_ref, o_ref, m_ref, l_ref, acc_ref, *, tk, far_bucket, lambda_init):
    j = pl.program_id(1)
    nj = pl.num_programs(1)
    dh = DA_DH
    hw = 2 * dh
    nh = q_ref.shape[1] // hw
    ts = q_ref.shape[0]

    @pl.when(j == 0)
    def _():
        m_ref[...] = jnp.full(m_ref.shape, -jnp.inf, F32)
        l_ref[...] = jnp.zeros(l_ref.shape, F32)
        acc_ref[...] = jnp.zeros(acc_ref.shape, F32)

    q = q_ref[...]

    def block(k_of, v_of, bias_fn):
        for h in range(nh):
            vb = v_of(h)
            for c in range(2):
                cs = slice((2 * h + c) * dh, (2 * h + c + 1) * dh)
                _softmax_step(q[:, cs], k_of(2 * h + c), vb, bias_fn(h), m_ref, l_ref, acc_ref, 2 * h + c)

    def cache_k_of(hc):
        return ck_ref[0, pl.ds(hc, tk, stride=2 * nh), :].astype(BF16)

    def cache_v_of(h):
        return jnp.concatenate([cvl_ref[0, pl.ds(h, tk, stride=nh), :],
                                cvh_ref[0, pl.ds(h, tk, stride=nh), :]], axis=1).astype(BF16)

    @pl.when(j < nj - 1)
    def _():
        block(cache_k_of, cache_v_of, lambda h: tab_ref[far_bucket, h] * LOG2E)

    @pl.when(j == nj - 1)
    def _():
        block(cache_k_of, cache_v_of, lambda h: blast_ref[h])
        block(lambda hc: kn_ref[:, hc * dh:(hc + 1) * dh], lambda h: vn_ref[:, h * hw:(h + 1) * hw],
              lambda h: bnew_ref[h][:, :ts])
        lam = _lambda(lam_ref[0:1, :], lam_ref[1:2, :], lam_ref[2:3, :], lam_ref[3:4, :], lambda_init)
        for h in range(nh):
            o = acc_ref[2 * h] / l_ref[2 * h] - lam * (acc_ref[2 * h + 1] / l_ref[2 * h + 1])
            o_ref[:, h * hw:(h + 1) * hw] = (_rms(o, gain_ref[...]) * (1.0 - lambda_init)).astype(BF16)


def sample_diff_attention(q, k_new, v_new, cache_k, cache_v, layer, table, lam_params, gain, lambda_init):
    n_layers, batch, past, nhc, dh = cache_k.shape
    d = nhc * dh
    ts = q.shape[0] // batch
    hw = 2 * DA_DH
    nh = d // hw
    tk = min(CFG["sa_tk"], past)
    nkb = past // tk
    q_pos = past + np.arange(ts)
    blast = bias_tiles(table, _bucket_tile(q_pos, past - tk + np.arange(tk)))
    bnew = bias_tiles(table, _bucket_tile(q_pos, past + np.arange(128)))
    if nkb > 1:
        far = _bucket_tile(q_pos, np.arange(past - tk))
        far_bucket = int(far[0, 0])
        assert (far == far_bucket).all()
    else:
        far_bucket = 0
    cache_v_rows = cache_v.reshape(n_layers, batch, past * nh, hw)
    return pl.pallas_call(
        functools.partial(_sample_attn_kernel, tk=tk, far_bucket=far_bucket, lambda_init=lambda_init),
        grid=(batch, nkb),
        in_specs=[pl.BlockSpec(memory_space=pltpu.SMEM),
                  pl.BlockSpec((ts, d), lambda b, j: (b, 0)),
                  pl.BlockSpec((None, 1, tk * nhc, dh), lambda b, j: (layer, b, j, 0)),
                  pl.BlockSpec((None, 1, tk * nh, dh), lambda b, j: (layer, b, j, 0)),
                  pl.BlockSpec((None, 1, tk * nh, dh), lambda b, j: (layer, b, j, 1)),
                  pl.BlockSpec((ts, d), lambda b, j: (b, 0)),
                  pl.BlockSpec((ts, d), lambda b, j: (b, 0)),
                  pl.BlockSpec((nh, ts, tk), lambda b, j: (0, 0, 0)),
                  pl.BlockSpec((nh, ts, 128), lambda b, j: (0, 0, 0)),
                  pl.BlockSpec((4, DA_DH), lambda b, j: (0, 0)),
                  pl.BlockSpec((1, hw), lambda b, j: (0, 0))],
        out_specs=pl.BlockSpec((ts, d), lambda b, j: (b, 0)),
        out_shape=jax.ShapeDtypeStruct((batch * ts, d), BF16),
        scratch_shapes=[pltpu.VMEM((2 * nh, ts, 1), F32), pltpu.VMEM((2 * nh, ts, 1), F32),
                        pltpu.VMEM((2 * nh, ts, hw), F32)],
        compiler_params=_cp(("parallel", "arbitrary")),
        name="sample_diff_attention",
    )(table, q, cache_k.reshape(n_layers, batch, past * nhc, dh), cache_v_rows, cache_v_rows,
      k_new, v_new, blast, bnew, lam_params, gain.reshape(1, hw))


def kernel(x_prompt, x_sample, state_hgrn, state_pool, cache_k, cache_v, norm_mix, norm_ffn, norm_final, hgrn_w_q, hgrn_w_f, hgrn_w_i, hgrn_w_g, hgrn_w_o, hgrn_lb_logits, hgrn_norm_gain, pool_w, pool_scale, attn_w_q, attn_w_k, attn_w_v, attn_w_o, attn_lambda_q1, attn_lambda_k1, attn_lambda_q2, attn_lambda_k2, attn_subln_gain, rel_bias_table, ffn_w_gate, ffn_w_up, ffn_w_down):
    bp, tp, d = x_prompt.shape
    bs, ts, _ = x_sample.shape
    past = cache_k.shape[2]
    depth = norm_mix.shape[0]
    assert bp == 1
    bf = lambda a: a.astype(BF16)
    xs = [x_prompt.reshape(bp * tp, d), x_sample.reshape(bs * ts, d)]
    dims = [(bp, tp), (bs, ts)]
    hg, pool_st, k_out, v_out = [[], []], [[], []], [[], []], [[], []]
    w_g, w_u, w_d = bf(ffn_w_gate), bf(ffn_w_up), bf(ffn_w_down)
    zero_state = jnp.zeros((1, bp, HG_HEADS, HG_DK, HG_DK), F32)

    for i in range(depth):
        m, j = i % N_MIXERS, i // N_MIXERS
        if m == 0:
            w_cat = bf(jnp.concatenate([hgrn_w_q[j], hgrn_w_f[j], hgrn_w_i[j], hgrn_w_g[j]], axis=1))
            w_o = bf(hgrn_w_o[j])
            s0s = [(zero_state, 0), (state_hgrn, j)]
            for r in range(2):
                b, t = dims[r]
                proj = norm_mm(xs[r], norm_mix[i], w_cat)
                og, st = hgrn_recurrence(proj, hgrn_lb_logits, hgrn_norm_gain[j], s0s[r], b, t, j)
                xs[r] = mm_res(og, w_o, xs[r])
                hg[r].append(st)
        elif m == 1:
            w_p = bf(pool_w[j])
            hists = [jnp.zeros((bp, POOL_HIST, d), F32), state_pool[j]]
            for r in range(2):
                b, t = dims[r]
                xs[r], st = pool_mixer(xs[r], norm_mix[i], hists[r], w_p, pool_scale[j], b, t, (0, past)[r])
                pool_st[r].append(st[:, 1:, :])
        else:
            lambda_init = 0.8 - 0.6 * math.exp(-0.3 * i)
            w_qkv = bf(jnp.concatenate([attn_w_q[j], attn_w_k[j], attn_w_v[j]], axis=1))
            w_o = bf(attn_w_o[j])
            lam_params = jnp.stack([attn_lambda_q1[j], attn_lambda_k1[j], attn_lambda_q2[j], attn_lambda_k2[j]])
            for r in range(2):
                b, t = dims[r]
                q, k32, k16, v32, v16 = qkv_proj(xs[r], norm_mix[i], w_qkv, DA_DH ** -0.5 * LOG2E, r == 0,
                                                 CFG["fa_tk"])
                if r == 0:
                    o = flash_diff_attention(q, k16, v16, rel_bias_table, lam_params, attn_subln_gain[j],
                                             lambda_init)
                else:
                    o = sample_diff_attention(q, k16, v16, cache_k, cache_v, j, rel_bias_table, lam_params,
                                              attn_subln_gain[j], lambda_init)
                xs[r] = mm_res(o, w_o, xs[r])
                k_out[r].append(k32.reshape(b, t, 2 * DA_HEADS, DA_DH))
                v_out[r].append(v32.reshape(b, t, DA_HEADS, 2 * DA_DH))
        g_final = norm_final if i == depth - 1 else None
        for r in range(2):
            xs[r] = ffn(xs[r], norm_ffn[i], w_g, w_u, w_d, i, g_final)

    return (xs[0].reshape(bp, tp, d), xs[1].reshape(bs, ts, d),
            jnp.stack(hg[0]), jnp.stack(hg[1]), jnp.stack(pool_st[0]), jnp.stack(pool_st[1]),
            jnp.stack(k_out[0]), jnp.stack(v_out[0]), jnp.stack(k_out[1]), jnp.stack(v_out[1]))
```

```python
import functools
import math

import numpy as np
import jax
import jax.numpy as jnp
from jax import lax
from jax.experimental import pallas as pl
from jax.experimental.pallas import tpu as pltpu

F32 = jnp.float32
BF16 = jnp.bfloat16

EPS = 1e-6
LOG2E = math.log2(math.e)
F_MIN = 1e-6
MASK_VALUE = -1e30
HG_HEADS = 16
HG_DK = 128
POOL_WINDOWS = (2, 4, 8, 16)
POOL_HIST = 15
DA_HEADS = 8
DA_DH = 128
ATT_CHUNK = 64
REL_BUCKETS = 32
REL_MAX_DIST = 128
N_MIXERS = 3

CFG = dict(
    mm_tm=1024, mm_tn=1024,
    ffn_tm=512, ffn_tf=512,
    pool_tm=512,
    hg_chunk=256, hg_rows=512, hg_heads=4, hg_unroll=2,
    fa_tq=512, fa_tk=512,
    sa_tk=1024,
    vmem=56 * 1024 * 1024,
)


def _cp(sem):
    return pltpu.CompilerParams(dimension_semantics=sem, vmem_limit_bytes=CFG["vmem"])


def _rms(x, g):
    return x * lax.rsqrt(jnp.mean(x * x, axis=-1, keepdims=True) + EPS) * g


def _dot(a, b):
    return jnp.dot(a, b, preferred_element_type=F32)


def _dot_nt(a, b):
    return lax.dot_general(a, b, (((1,), (1,)), ((), ())), preferred_element_type=F32)


def _norm_mm_kernel(x_ref, g_ref, w_ref, o_ref, xn_ref):
    @pl.when(pl.program_id(1) == 0)
    def _():
        xn_ref[...] = _rms(x_ref[...], g_ref[...]).astype(BF16)

    o_ref[...] = _dot(xn_ref[...], w_ref[...])


def norm_mm(x, g, w):
    m, d = x.shape
    n = w.shape[1]
    tm, tn = min(CFG["mm_tm"], m), min(CFG["mm_tn"], n)
    return pl.pallas_call(
        _norm_mm_kernel,
        grid=(m // tm, n // tn),
        in_specs=[pl.BlockSpec((tm, d), lambda i, j: (i, 0)),
                  pl.BlockSpec((1, d), lambda i, j: (0, 0)),
                  pl.BlockSpec((d, tn), lambda i, j: (0, j))],
        out_specs=pl.BlockSpec((tm, tn), lambda i, j: (i, j)),
        out_shape=jax.ShapeDtypeStruct((m, n), F32),
        scratch_shapes=[pltpu.VMEM((tm, d), BF16)],
        compiler_params=_cp(("parallel", "arbitrary")),
        name="norm_mm",
    )(x, g.reshape(1, d), w)


def _qkv_kernel(x_ref, g_ref, w_ref, q_ref, k32_ref, k16_ref, v32_ref, v16_ref, xn_ref, *, nq, q_scale, v_transposed):
    j = pl.program_id(1)

    @pl.when(j == 0)
    def _():
        xn_ref[...] = _rms(x_ref[...], g_ref[...]).astype(BF16)

    y = _dot(xn_ref[...], w_ref[...])

    @pl.when(j < nq)
    def _():
        q_ref[...] = (y * q_scale).astype(BF16)

    @pl.when((j >= nq) & (j < 2 * nq))
    def _():
        k32_ref[...] = y
        k16_ref[...] = y.astype(BF16)

    @pl.when(j >= 2 * nq)
    def _():
        v32_ref[...] = y
        if v_transposed:
            v16_ref[0] = y.T.astype(BF16)
        else:
            v16_ref[...] = y.astype(BF16)


def qkv_proj(x, g, w, q_scale, v_transposed, tm):
    m, d = x.shape
    n = w.shape[1] // 3
    tm, tn = min(tm, m), min(CFG["mm_tn"], n)
    nq = n // tn
    spec = lambda off: pl.BlockSpec((tm, tn), lambda i, j: (i, jnp.clip(j - off, 0, nq - 1)))
    if v_transposed:
        v16_spec = pl.BlockSpec((1, tn, tm), lambda i, j: (i, jnp.clip(j - 2 * nq, 0, nq - 1), 0))
        v16_shape = jax.ShapeDtypeStruct((m // tm, n, tm), BF16)
    else:
        v16_spec, v16_shape = spec(2 * nq), jax.ShapeDtypeStruct((m, n), BF16)
    return pl.pallas_call(
        functools.partial(_qkv_kernel, nq=nq, q_scale=q_scale, v_transposed=v_transposed),
        grid=(m // tm, 3 * nq),
        in_specs=[pl.BlockSpec((tm, d), lambda i, j: (i, 0)),
                  pl.BlockSpec((1, d), lambda i, j: (0, 0)),
                  pl.BlockSpec((d, tn), lambda i, j: (0, j))],
        out_specs=[spec(0), spec(nq), spec(nq), spec(2 * nq), v16_spec],
        out_shape=[jax.ShapeDtypeStruct((m, n), BF16), jax.ShapeDtypeStruct((m, n), F32),
                   jax.ShapeDtypeStruct((m, n), BF16), jax.ShapeDtypeStruct((m, n), F32), v16_shape],
        scratch_shapes=[pltpu.VMEM((tm, d), BF16)],
        compiler_params=_cp(("parallel", "arbitrary")),
        name="qkv_proj",
    )(x, g.reshape(1, d), w)


def _mm_res_kernel(a_ref, w_ref, r_ref, o_ref):
    o_ref[...] = r_ref[...] + _dot(a_ref[...], w_ref[...])


def mm_res(a, w, r):
    m, k = a.shape
    n = w.shape[1]
    tm, tn = min(CFG["mm_tm"], m), min(CFG["mm_tn"], n)
    return pl.pallas_call(
        _mm_res_kernel,
        grid=(m // tm, n // tn),
        in_specs=[pl.BlockSpec((tm, k), lambda i, j: (i, 0)),
                  pl.BlockSpec((k, tn), lambda i, j: (0, j)),
                  pl.BlockSpec((tm, tn), lambda i, j: (i, j))],
        out_specs=pl.BlockSpec((tm, tn), lambda i, j: (i, j)),
        out_shape=jax.ShapeDtypeStruct((m, n), F32),
        compiler_params=_cp(("parallel", "arbitrary")),
        name="mm_res",
    )(a, w, r)


def _ffn_kernel(x_ref, g_ref, wg_ref, wu_ref, wd_ref, gf_ref, o_ref, xn_ref, *, final_norm):
    j = pl.program_id(1)

    @pl.when(j == 0)
    def _():
        x = x_ref[...]
        xn_ref[...] = _rms(x, g_ref[...]).astype(BF16)
        o_ref[...] = x

    xn = xn_ref[...]
    a = _dot(xn, wg_ref[...])
    b = _dot(xn, wu_ref[...])
    h = (a * jax.nn.sigmoid(a) * b).astype(BF16)
    o_ref[...] += _dot(h, wd_ref[...])

    if final_norm:
        @pl.when(j == pl.num_programs(1) - 1)
        def _():
            o_ref[...] = _rms(o_ref[...], gf_ref[...])


def ffn(x, g, wg, wu, wd, layer, g_final=None):
    m, d = x.shape
    f = wg.shape[2]
    tm, tf = min(CFG["ffn_tm"], m), min(CFG["ffn_tf"], f)
    final_norm = g_final is not None
    gf = (g_final if final_norm else g).reshape(1, d)
    return pl.pallas_call(
        functools.partial(_ffn_kernel, final_norm=final_norm),
        grid=(m // tm, f // tf),
        in_specs=[pl.BlockSpec((tm, d), lambda i, j: (i, 0)),
                  pl.BlockSpec((1, d), lambda i, j: (0, 0)),
                  pl.BlockSpec((None, d, tf), lambda i, j: (layer, 0, j)),
                  pl.BlockSpec((None, d, tf), lambda i, j: (layer, 0, j)),
                  pl.BlockSpec((None, tf, d), lambda i, j: (layer, j, 0)),
                  pl.BlockSpec((1, d), lambda i, j: (0, 0))],
        out_specs=pl.BlockSpec((tm, d), lambda i, j: (i, 0)),
        out_shape=jax.ShapeDtypeStruct((m, d), F32),
        scratch_shapes=[pltpu.VMEM((tm, d), BF16)],
        compiler_params=_cp(("parallel", "arbitrary")),
        name="ffn",
    )(x, g.reshape(1, d), wg, wu, wd, gf)


def _hgrn_levels(c):
    lv, h = [], c // 2
    while h >= 8:
        lv.append(h)
        h //= 2
    return lv


def _bcast_rows(g, rows, rep):
    return jnp.concatenate([jnp.broadcast_to(g[r:r + 1, :], (rep, g.shape[1])) for r in rows], axis=0)


def _hgrn_kernel(q_ref, f_ref, v_ref, gt_ref, lbl_ref, gain_ref, s0_ref, l_ref, og_ref, st_ref, stt_ref,
                 *, c, n_sub, hb, layer):
    cb = pl.program_id(2)
    dk = HG_DK

    @pl.when(cb == 0)
    def _():
        for h in range(hb):
            stt_ref[h] = s0_ref[0, h].T

    lg = lbl_ref[...]
    e = jnp.exp(lg - jnp.max(lg, axis=0, keepdims=True))
    p = e / jnp.sum(e, axis=0, keepdims=True)
    lb = p[0:1, :]
    for i in range(1, layer + 1):
        lb = lb + p[i:i + 1, :]
    lb = jnp.maximum(lb - p[0:1, :], 0.0)

    ti = lax.broadcasted_iota(jnp.int32, (c, c), 0)
    si = lax.broadcasted_iota(jnp.int32, (c, c), 1)
    levels = _hgrn_levels(c)
    masks = []
    for hh in levels:
        sh = int(math.log2(2 * hh))
        masks.append(((ti >> sh) == (si >> sh)) & ((ti & (2 * hh - 1)) >= hh) & ((si & (2 * hh - 1)) < hh))
    mask_loc = ((ti >> 3) == (si >> 3)) & (si <= ti)
    lmat = l_ref[...]
    gain = gain_ref[...]

    def chunk(ci, carry):
        r0 = pl.multiple_of(ci * c, c)
        fp = f_ref[pl.ds(r0, c), :]
        f = lb + (1.0 - lb) * jax.nn.sigmoid(fp)
        g = jnp.log2(jnp.maximum(f, F_MIN))
        kk = 1.0 - f
        g_hi = g.astype(BF16)
        r1 = g - g_hi.astype(F32)
        g_mid = r1.astype(BF16)
        g_lo = (r1 - g_mid.astype(F32)).astype(BF16)
        gc = _dot(lmat, g_hi) + _dot(lmat, g_mid) + _dot(lmat, g_lo)
        qa = q_ref[pl.ds(r0, c), :]
        va = v_ref[pl.ds(r0, c), :]
        ga = gt_ref[pl.ds(r0, c), :]
        for h in range(hb):
            sl = slice(h * dk, (h + 1) * dk)
            gh, qh, kh = gc[:, sl], qa[:, sl], kk[:, sl]
            vh = va[:, sl].astype(BF16)
            stt = stt_ref[h]
            inter = _dot_nt((qh * jnp.exp2(gh)).astype(BF16), stt.astype(BF16))
            g_last = gh[c - 1:c, :]
            k_dec = (kh * jnp.exp2(g_last - gh)).astype(BF16)
            stt_ref[h] = stt * jnp.exp2(g_last) + lax.dot_general(
                vh, k_dec, (((0,), (0,)), ((), ())), preferred_element_type=F32)
            q16, k16 = qh.astype(BF16), kh.astype(BF16)
            g_loc = _bcast_rows(gh, [8 * b + 3 for b in range(c // 8)], 8)
            sc = jnp.where(mask_loc,
                           _dot_nt(q16 * jnp.exp2(gh - g_loc).astype(BF16),
                                   k16 * jnp.exp2(g_loc - gh).astype(BF16)), 0.0)
            for hh, mk in zip(levels, masks):
                g_mid_rows = _bcast_rows(gh, [b * 2 * hh + hh - 1 for b in range(c // (2 * hh))], 2 * hh)
                fac = jnp.exp2(-jnp.abs(gh - g_mid_rows)).astype(BF16)
                sc = jnp.where(mk, _dot_nt(q16 * fac, k16 * fac), sc)
            o = inter + _dot(sc.astype(BF16), vh)
            on = _rms(o, gain)
            gate = ga[:, sl]
            og_ref[pl.ds(r0, c), sl] = (on * (gate * jax.nn.sigmoid(gate))).astype(BF16)
        return carry

    lax.fori_loop(0, n_sub, chunk, 0, unroll=min(CFG["hg_unroll"], n_sub))

    @pl.when(cb == pl.num_programs(2) - 1)
    def _():
        for h in range(hb):
            st_ref[0, h] = stt_ref[h].T


def hgrn_recurrence(proj, lb_logits, gain, s0, batch, t, layer):
    s0, s0_layer = s0
    d = HG_HEADS * HG_DK
    c = min(CFG["hg_chunk"], t)
    rows = min(CFG["hg_rows"], t)
    hb = CFG["hg_heads"] if t > c else HG_HEADS
    w = hb * HG_DK
    ncb = t // rows
    n_layers = lb_logits.shape[0]
    lmat = jnp.asarray(np.tril(np.ones((c, c), np.float32)), BF16)

    def col(sec):
        return lambda b, hg, cb: (b * ncb + cb, sec * (d // w) + hg)

    og, st = pl.pallas_call(
        functools.partial(_hgrn_kernel, c=c, n_sub=rows // c, hb=hb, layer=layer),
        grid=(batch, HG_HEADS // hb, ncb),
        in_specs=[pl.BlockSpec((rows, w), col(0)),
                  pl.BlockSpec((rows, w), col(1)),
                  pl.BlockSpec((rows, w), col(2)),
                  pl.BlockSpec((rows, w), col(3)),
                  pl.BlockSpec((n_layers, w), lambda b, hg, cb: (0, hg)),
                  pl.BlockSpec((1, HG_DK), lambda b, hg, cb: (0, 0)),
                  pl.BlockSpec((None, 1, hb, HG_DK, HG_DK), lambda b, hg, cb: (s0_layer, b, hg, 0, 0)),
                  pl.BlockSpec((c, c), lambda b, hg, cb: (0, 0))],
        out_specs=[pl.BlockSpec((rows, w), lambda b, hg, cb: (b * ncb + cb, hg)),
                   pl.BlockSpec((1, hb, HG_DK, HG_DK), lambda b, hg, cb: (b, hg, 0, 0))],
        out_shape=[jax.ShapeDtypeStruct((batch * t, d), BF16),
                   jax.ShapeDtypeStruct((batch, HG_HEADS, HG_DK, HG_DK), F32)],
        scratch_shapes=[pltpu.VMEM((hb, HG_DK, HG_DK), F32)],
        compiler_params=_cp(("parallel", "parallel", "arbitrary")),
        name="hgrn_recurrence",
    )(proj, proj, proj, proj, lb_logits, gain.reshape(1, HG_DK), s0, lmat)
    return og, st


def _pool_kernel(x_ref, g_ref, hist_ref, w_ref, sc_ref, o_ref, st_ref, buf_ref, *, tm, pos0):
    tb = pl.program_id(1)
    x = x_ref[...]
    xn = _rms(x, g_ref[...])
    hp = POOL_HIST + 1

    @pl.when(tb == 0)
    def _():
        buf_ref[0:hp, :] = hist_ref[0]

    @pl.when(tb > 0)
    def _():
        buf_ref[0:hp, :] = buf_ref[tm:tm + hp, :]

    buf_ref[hp:hp + tm, :] = xn
    st_ref[0] = buf_ref[tm:tm + hp, :]
    pos = pos0 + tb * tm + lax.broadcasted_iota(jnp.int32, (tm, 1), 0)
    gc = x.shape[1] // len(POOL_WINDOWS)
    for gi, win in enumerate(POOL_WINDOWS):
        cs = slice(gi * gc, (gi + 1) * gc)
        ws = xn[:, cs]
        for sft in range(1, win):
            ws = ws + buf_ref[hp - sft:hp - sft + tm, cs]
        cnt = jnp.minimum(win, pos + 1).astype(F32)
        pooled = (ws / cnt - xn[:, cs]).astype(BF16)
        o_ref[:, cs] = x[:, cs] + _dot(pooled, w_ref[gi]) * sc_ref[:, cs]


def pool_mixer(x, g, hist, w, scale, batch, t, pos0):
    d = x.shape[1]
    tm = min(CFG["pool_tm"], t)
    nb = t // tm
    hp = POOL_HIST + 1
    hist16 = jnp.concatenate([jnp.zeros((batch, 1, d), F32), hist], axis=1)
    ng, gc = w.shape[0], w.shape[1]
    return pl.pallas_call(
        functools.partial(_pool_kernel, tm=tm, pos0=pos0),
        grid=(batch, nb),
        in_specs=[pl.BlockSpec((tm, d), lambda b, i: (b * nb + i, 0)),
                  pl.BlockSpec((1, d), lambda b, i: (0, 0)),
                  pl.BlockSpec((1, hp, d), lambda b, i: (b, 0, 0)),
                  pl.BlockSpec((ng, gc, gc), lambda b, i: (0, 0, 0)),
                  pl.BlockSpec((1, d), lambda b, i: (0, 0))],
        out_specs=[pl.BlockSpec((tm, d), lambda b, i: (b * nb + i, 0)),
                   pl.BlockSpec((1, hp, d), lambda b, i: (b, 0, 0))],
        out_shape=[jax.ShapeDtypeStruct(x.shape, F32),
                   jax.ShapeDtypeStruct((batch, hp, d), F32)],
        scratch_shapes=[pltpu.VMEM((tm + hp, d), F32)],
        compiler_params=_cp(("parallel", "arbitrary")),
        name="pool_mixer",
    )(x, g.reshape(1, d), hist16, w, scale.reshape(1, d))


def _rel_bucket_np(rel):
    nb = REL_BUCKETS // 2
    max_exact = nb // 2
    n = np.abs(rel)
    large = max_exact + (np.log(np.maximum(n, max_exact).astype(np.float64) / max_exact)
                         / math.log(REL_MAX_DIST / max_exact) * (nb - max_exact)).astype(np.int64)
    large = np.minimum(large, nb - 1)
    return (np.where(rel > 0, nb, 0) + np.where(n < max_exact, n, large)).astype(np.int32)


def _bucket_tile(q_pos, k_pos):
    q_pos, k_pos = np.asarray(q_pos)[:, None], np.asarray(k_pos)[None, :]
    b = _rel_bucket_np(k_pos - q_pos)
    return np.where((k_pos // ATT_CHUNK) <= (q_pos // ATT_CHUNK), b, -1).astype(np.int32)


def _bias_kernel(tab_ref, bk_ref, o_ref, *, present):
    h = pl.program_id(0)
    rows = bk_ref.shape[0] // len(present)
    for k, values in enumerate(present):
        bk = bk_ref[k * rows:(k + 1) * rows, :]
        bias = jnp.full(bk.shape, MASK_VALUE, F32)
        for b in values:
            bias = jnp.where(bk == b, tab_ref[b, h] * LOG2E, bias)
        o_ref[0, k * rows:(k + 1) * rows, :] = bias


def bias_tiles(table, bucket, bands=1):
    r, c = bucket.shape
    nh = table.shape[1]
    present = tuple(tuple(int(b) for b in np.unique(band) if b >= 0) for band in np.split(bucket, bands, axis=0))
    return pl.pallas_call(
        functools.partial(_bias_kernel, present=present),
        grid=(nh,),
        in_specs=[pl.BlockSpec(memory_space=pltpu.SMEM),
                  pl.BlockSpec((r, c), lambda h: (0, 0))],
        out_specs=pl.BlockSpec((1, r, c), lambda h: (h, 0, 0)),
        out_shape=jax.ShapeDtypeStruct((nh, r, c), F32),
        compiler_params=_cp(("arbitrary",)),
        name="bias_tiles",
    )(table, jnp.asarray(bucket))


def _lambda(lq1, lk1, lq2, lk2, lambda_init):
    return (jnp.exp(jnp.sum(lq1 * lk1, axis=-1, keepdims=True))
            - jnp.exp(jnp.sum(lq2 * lk2, axis=-1, keepdims=True)) + lambda_init)


def _softmax_step(q, k, v, bias, m_ref, l_ref, acc_ref, idx):
    s = _dot_nt(q, k) + bias
    m_prev = m_ref[idx]
    m_new = jnp.maximum(m_prev, jnp.max(s, axis=-1, keepdims=True))
    p = jnp.exp2(s - m_new)
    alpha = jnp.exp2(m_prev - m_new)
    l_ref[idx] = alpha * l_ref[idx] + jnp.sum(p, axis=-1, keepdims=True)
    acc_ref[idx] = alpha * acc_ref[idx] + _dot(p.astype(BF16), v)
    m_ref[idx] = m_new


def _flash_kernel(q_ref, k_ref, vt_ref, bias_ref, lam_ref, gain_ref, o_ref,
                  m_ref, l_ref, acc_ref, sa_ref, sb_ref, pa_ref, pb_ref, ala_ref, alb_ref, mxa_ref, mxb_ref,
                  *, tq, tk, lambda_init):
    i = pl.program_id(1)
    dh = DA_DH
    m_ref[...] = jnp.full(m_ref.shape, -jnp.inf, F32)
    l_ref[...] = jnp.zeros(l_ref.shape, F32)
    acc_ref[...] = jnp.zeros(acc_ref.shape, F32)
    q = q_ref[...]

    nkb = vt_ref.shape[0]

    def scores(j, s_ref, mx_ref):
        jc = jnp.minimum(j, nkb - 1)
        kb = k_ref[pl.ds(pl.multiple_of(jc * tk, tk), tk), :]
        bias = bias_ref[0, jnp.clip(j - (i - 2), 0, 3)]
        for c in range(2):
            s = _dot_nt(kb[:, c * dh:(c + 1) * dh], q[:, c * dh:(c + 1) * dh]) + bias
            s_ref[c] = s
            mx_ref[c] = jnp.max(s, axis=0, keepdims=True)

    def softmax(j, s_ref, mx_ref, p_ref, al_ref):
        for c in range(2):
            m_prev = m_ref[c]
            m_new = jnp.maximum(m_prev, mx_ref[c])
            p = jnp.exp2(s_ref[c] - m_new)
            alpha = jnp.exp2(m_prev - m_new)
            l_ref[c] = alpha * l_ref[c] + jnp.sum(p, axis=0, keepdims=True)
            m_ref[c] = m_new
            al_ref[c] = alpha
            p_ref[c] = p.astype(BF16)

    def values(j, p_ref, al_ref):
        vt = vt_ref[jnp.minimum(j, nkb - 1)]
        for c in range(2):
            acc_ref[c] = al_ref[c] * acc_ref[c] + _dot(vt, p_ref[c])

    scores(0, sa_ref, mxa_ref)
    scores(1, sb_ref, mxb_ref)
    softmax(0, sa_ref, mxa_ref, pa_ref, ala_ref)

    def pair(t, carry):
        j = 2 * t
        scores(j + 2, sa_ref, mxa_ref)
        softmax(j + 1, sb_ref, mxb_ref, pb_ref, alb_ref)
        values(j, pa_ref, ala_ref)
        scores(j + 3, sb_ref, mxb_ref)
        softmax(j + 2, sa_ref, mxa_ref, pa_ref, ala_ref)
        values(j + 1, pb_ref, alb_ref)
        return carry

    lax.fori_loop(0, (i + 2) // 2, pair, 0)

    lam = _lambda(lam_ref[0:1, :], lam_ref[1:2, :], lam_ref[2:3, :], lam_ref[3:4, :], lambda_init)
    o = acc_ref[0] / l_ref[0] - lam * (acc_ref[1] / l_ref[1])
    on = o * lax.rsqrt(jnp.mean(o * o, axis=0, keepdims=True) + EPS) * (1.0 - lambda_init)
    o_ref[...] = (on.T * gain_ref[...]).astype(BF16)


def flash_diff_attention(q, k, vt, table, lam_params, gain, lambda_init):
    t, d = q.shape
    tq, tk = min(CFG["fa_tq"], t), min(CFG["fa_tk"], t)
    assert tq == tk and tk % ATT_CHUNK == 0 and vt.shape == (t // tk, d, tk)
    hw = 2 * DA_DH
    nh = d // hw
    far = _rel_bucket_np(-np.arange(tk + 1, max(t, tk + 2)))
    far_bucket = int(far[0])
    assert (far == far_bucket).all()
    near = _bucket_tile(tk + np.arange(tq), np.arange(2 * tk)).T
    bucket = np.concatenate([np.full((tk, tq), far_bucket, np.int32), near,
                             np.full((tk, tq), -1, np.int32)], axis=0)
    bank = bias_tiles(table, bucket, bands=4).reshape(nh, 4, tk, tq)
    once = pl.Buffered(1)
    return pl.pallas_call(
        functools.partial(_flash_kernel, tq=tq, tk=tk, lambda_init=lambda_init),
        grid=(nh, t // tq),
        in_specs=[pl.BlockSpec((tq, hw), lambda h, i: (i, h)),
                  pl.BlockSpec((t, hw), lambda h, i: (0, h), pipeline_mode=once),
                  pl.BlockSpec((t // tk, hw, tk), lambda h, i: (0, h, 0), pipeline_mode=once),
                  pl.BlockSpec((1, 4, tk, tq), lambda h, i: (h, 0, 0, 0), pipeline_mode=once),
                  pl.BlockSpec((4, DA_DH), lambda h, i: (0, 0)),
                  pl.BlockSpec((1, hw), lambda h, i: (0, 0))],
        out_specs=pl.BlockSpec((tq, hw), lambda h, i: (i, h)),
        out_shape=jax.ShapeDtypeStruct((t, d), BF16),
        scratch_shapes=[pltpu.VMEM((2, 1, tq), F32), pltpu.VMEM((2, 1, tq), F32),
                        pltpu.VMEM((2, hw, tq), F32),
                        pltpu.VMEM((2, tk, tq), F32), pltpu.VMEM((2, tk, tq), F32),
                        pltpu.VMEM((2, tk, tq), BF16), pltpu.VMEM((2, tk, tq), BF16),
                        pltpu.VMEM((2, 1, tq), F32), pltpu.VMEM((2, 1, tq), F32),
                        pltpu.VMEM((2, 1, tq), F32), pltpu.VMEM((2, 1, tq), F32)],
        compiler_params=_cp(("parallel", "arbitrary")),
        name="flash_diff_attention",
    )(q, k, vt, bank, lam_params, gain.reshape(1, hw))


def _sample_attn_kernel(tab_ref, q_ref, ck_ref, cvl_ref, cvh_ref, kn_ref, vn_ref, blast_ref, bnew_ref, lam_ref,
                        gain_ref, o_ref, m_ref, l_ref, acc_ref, *, tk, far_bucket, lambda_init):
    j = pl.program_id(1)
    nj = pl.num_programs(1)
    dh = DA_DH
    hw = 2 * dh
    nh = q_ref.shape[1] // hw
    ts = q_ref.shape[0]

    @pl.when(j == 0)
    def _():
        m_ref[...] = jnp.full(m_ref.shape, -jnp.inf, F32)
        l_ref[...] = jnp.zeros(l_ref.shape, F32)
        acc_ref[...] = jnp.zeros(acc_ref.shape, F32)

    q = q_ref[...]

    def block(k_of, v_of, bias_fn):
        for h in range(nh):
            vb = v_of(h)
            for c in range(2):
                cs = slice((2 * h + c) * dh, (2 * h + c + 1) * dh)
                _softmax_step(q[:, cs], k_of(2 * h + c), vb, bias_fn(h), m_ref, l_ref, acc_ref, 2 * h + c)

    def cache_k_of(hc):
        return ck_ref[0, pl.ds(hc, tk, stride=2 * nh), :].astype(BF16)

    def cache_v_of(h):
        return jnp.concatenate([cvl_ref[0, pl.ds(h, tk, stride=nh), :],
                                cvh_ref[0, pl.ds(h, tk, stride=nh), :]], axis=1).astype(BF16)

    @pl.when(j < nj - 1)
    def _():
        block(cache_k_of, cache_v_of, lambda h: tab_ref[far_bucket, h] * LOG2E)

    @pl.when(j == nj - 1)
    def _():
        block(cache_k_of, cache_v_of, lambda h: blast_ref[h])
        block(lambda hc: kn_ref[:, hc * dh:(hc + 1) * dh], lambda h: vn_ref[:, h * hw:(h + 1) * hw],
              lambda h: bnew_ref[h][:, :ts])
        lam = _lambda(lam_ref[0:1, :], lam_ref[1:2, :], lam_ref[2:3, :], lam_ref[3:4, :], lambda_init)
        for h in range(nh):
            o = acc_ref[2 * h] / l_ref[2 * h] - lam * (acc_ref[2 * h + 1] / l_ref[2 * h + 1])
            o_ref[:, h * hw:(h + 1) * hw] = (_rms(o, gain_ref[...]) * (1.0 - lambda_init)).astype(BF16)


def sample_diff_attention(q, k_new, v_new, cache_k, cache_v, layer, table, lam_params, gain, lambda_init):
    n_layers, batch, past, nhc, dh = cache_k.shape
    d = nhc * dh
    ts = q.shape[0] // batch
    hw = 2 * DA_DH
    nh = d // hw
    tk = min(CFG["sa_tk"], past)
    nkb = past // tk
    q_pos = past + np.arange(ts)
    blast = bias_tiles(table, _bucket_tile(q_pos, past - tk + np.arange(tk)))
    bnew = bias_tiles(table, _bucket_tile(q_pos, past + np.arange(128)))
    if nkb > 1:
        far = _bucket_tile(q_pos, np.arange(past - tk))
        far_bucket = int(far[0, 0])
        assert (far == far_bucket).all()
    else:
        far_bucket = 0
    cache_v_rows = cache_v.reshape(n_layers, batch, past * nh, hw)
    return pl.pallas_call(
        functools.partial(_sample_attn_kernel, tk=tk, far_bucket=far_bucket, lambda_init=lambda_init),
        grid=(batch, nkb),
        in_specs=[pl.BlockSpec(memory_space=pltpu.SMEM),
                  pl.BlockSpec((ts, d), lambda b, j: (b, 0)),
                  pl.BlockSpec((None, 1, tk * nhc, dh), lambda b, j: (layer, b, j, 0)),
                  pl.BlockSpec((None, 1, tk * nh, dh), lambda b, j: (layer, b, j, 0)),
                  pl.BlockSpec((None, 1, tk * nh, dh), lambda b, j: (layer, b, j, 1)),
                  pl.BlockSpec((ts, d), lambda b, j: (b, 0)),
                  pl.BlockSpec((ts, d), lambda b, j: (b, 0)),
                  pl.BlockSpec((nh, ts, tk), lambda b, j: (0, 0, 0)),
                  pl.BlockSpec((nh, ts, 128), lambda b, j: (0, 0, 0)),
                  pl.BlockSpec((4, DA_DH), lambda b, j: (0, 0)),
                  pl.BlockSpec((1, hw), lambda b, j: (0, 0))],
        out_specs=pl.BlockSpec((ts, d), lambda b, j: (b, 0)),
        out_shape=jax.ShapeDtypeStruct((batch * ts, d), BF16),
        scratch_shapes=[pltpu.VMEM((2 * nh, ts, 1), F32), pltpu.VMEM((2 * nh, ts, 1), F32),
                        pltpu.VMEM((2 * nh, ts, hw), F32)],
        compiler_params=_cp(("parallel", "arbitrary")),
        name="sample_diff_attention",
    )(table, q, cache_k.reshape(n_layers, batch, past * nhc, dh), cache_v_rows, cache_v_rows,
      k_new, v_new, blast, bnew, lam_params, gain.reshape(1, hw))


def kernel(x_prompt, x_sample, state_hgrn, state_pool, cache_k, cache_v, norm_mix, norm_ffn, norm_final, hgrn_w_q, hgrn_w_f, hgrn_w_i, hgrn_w_g, hgrn_w_o, hgrn_lb_logits, hgrn_norm_gain, pool_w, pool_scale, attn_w_q, attn_w_k, attn_w_v, attn_w_o, attn_lambda_q1, attn_lambda_k1, attn_lambda_q2, attn_lambda_k2, attn_subln_gain, rel_bias_table, ffn_w_gate, ffn_w_up, ffn_w_down):
    bp, tp, d = x_prompt.shape
    bs, ts, _ = x_sample.shape
    past = cache_k.shape[2]
    depth = norm_mix.shape[0]
    assert bp == 1
    bf = lambda a: a.astype(BF16)
    xs = [x_prompt.reshape(bp * tp, d), x_sample.reshape(bs * ts, d)]
    dims = [(bp, tp), (bs, ts)]
    hg, pool_st, k_out, v_out = [[], []], [[], []], [[], []], [[], []]
    w_g, w_u, w_d = bf(ffn_w_gate), bf(ffn_w_up), bf(ffn_w_down)
    zero_state = jnp.zeros((1, bp, HG_HEADS, HG_DK, HG_DK), F32)

    for i in range(depth):
        m, j = i % N_MIXERS, i // N_MIXERS
        if m == 0:
            w_cat = bf(jnp.concatenate([hgrn_w_q[j], hgrn_w_f[j], hgrn_w_i[j], hgrn_w_g[j]], axis=1))
            w_o = bf(hgrn_w_o[j])
            s0s = [(zero_state, 0), (state_hgrn, j)]
            for r in range(2):
                b, t = dims[r]
                proj = norm_mm(xs[r], norm_mix[i], w_cat)
                og, st = hgrn_recurrence(proj, hgrn_lb_logits, hgrn_norm_gain[j], s0s[r], b, t, j)
                xs[r] = mm_res(og, w_o, xs[r])
                hg[r].append(st)
        elif m == 1:
            w_p = bf(pool_w[j])
            hists = [jnp.zeros((bp, POOL_HIST, d), F32), state_pool[j]]
            for r in range(2):
                b, t = dims[r]
                xs[r], st = pool_mixer(xs[r], norm_mix[i], hists[r], w_p, pool_scale[j], b, t, (0, past)[r])
                pool_st[r].append(st[:, 1:, :])
        else:
            lambda_init = 0.8 - 0.6 * math.exp(-0.3 * i)
            w_qkv = bf(jnp.concatenate([attn_w_q[j], attn_w_k[j], attn_w_v[j]], axis=1))
            w_o = bf(attn_w_o[j])
            lam_params = jnp.stack([attn_lambda_q1[j], attn_lambda_k1[j], attn_lambda_q2[j], attn_lambda_k2[j]])
            for r in range(2):
                b, t = dims[r]
                q, k32, k16, v32, v16 = qkv_proj(xs[r], norm_mix[i], w_qkv, DA_DH ** -0.5 * LOG2E, r == 0,
                                                 CFG["fa_tk"])
                if r == 0:
                    o = flash_diff_attention(q, k16, v16, rel_bias_table, lam_params, attn_subln_gain[j],
                                             lambda_init)
                else:
                    o = sample_diff_attention(q, k16, v16, cache_k, cache_v, j, rel_bias_table, lam_params,
                                              attn_subln_gain[j], lambda_init)
                xs[r] = mm_res(o, w_o, xs[r])
                k_out[r].append(k32.reshape(b, t, 2 * DA_HEADS, DA_DH))
                v_out[r].append(v32.reshape(b, t, DA_HEADS, 2 * DA_DH))
        g_final = norm_final if i == depth - 1 else None
        for r in range(2):
            xs[r] = ffn(xs[r], norm_ffn[i], w_g, w_u, w_d, i, g_final)

    return (xs[0].reshape(bp, tp, d), xs[1].reshape(bs, ts, d),
            jnp.stack(hg[0]), jnp.stack(hg[1]), jnp.stack(pool_st[0]), jnp.stack(pool_st[1]),
            jnp.stack(k_out[0]), jnp.stack(v_out[0]), jnp.stack(k_out[1]), jnp.stack(v_out[1]))
```

```python
import functools
import math

import numpy as np
import jax
import jax.numpy as jnp
from jax import lax
from jax.experimental import pallas as pl
from jax.experimental.pallas import tpu as pltpu

F32 = jnp.float32
BF16 = jnp.bfloat16

EPS = 1e-6
LOG2E = math.log2(math.e)
F_MIN = 1e-6
MASK_VALUE = -1e30
HG_HEADS = 16
HG_DK = 128
POOL_WINDOWS = (2, 4, 8, 16)
POOL_HIST = 15
DA_HEADS = 8
DA_DH = 128
ATT_CHUNK = 64
ONES_ROWS = 16
REL_BUCKETS = 32
REL_MAX_DIST = 128
N_MIXERS = 3

CFG = dict(
    mm_tm=1024, mm_tn=1024,
    ffn_tm=512, ffn_tf=512,
    pool_tm=512,
    hg_chunk=256, hg_rows=512, hg_heads=4, hg_unroll=2,
    fa_tq=512, fa_tk=512,
    sa_tk=1024,
    vmem=56 * 1024 * 1024,
)


def _cp(sem):
    return pltpu.CompilerParams(dimension_semantics=sem, vmem_limit_bytes=CFG["vmem"])


def _rms(x, g):
    return x * lax.rsqrt(jnp.mean(x * x, axis=-1, keepdims=True) + EPS) * g


def _dot(a, b):
    return jnp.dot(a, b, preferred_element_type=F32)


def _dot_nt(a, b):
    return lax.dot_general(a, b, (((1,), (1,)), ((), ())), preferred_element_type=F32)


def _norm_mm_kernel(x_ref, g_ref, w_ref, o_ref, xn_ref):
    @pl.when(pl.program_id(1) == 0)
    def _():
        xn_ref[...] = _rms(x_ref[...], g_ref[...]).astype(BF16)

    o_ref[...] = _dot(xn_ref[...], w_ref[...])


def norm_mm(x, g, w):
    m, d = x.shape
    n = w.shape[1]
    tm, tn = min(CFG["mm_tm"], m), min(CFG["mm_tn"], n)
    return pl.pallas_call(
        _norm_mm_kernel,
        grid=(m // tm, n // tn),
        in_specs=[pl.BlockSpec((tm, d), lambda i, j: (i, 0)),
                  pl.BlockSpec((1, d), lambda i, j: (0, 0)),
                  pl.BlockSpec((d, tn), lambda i, j: (0, j))],
        out_specs=pl.BlockSpec((tm, tn), lambda i, j: (i, j)),
        out_shape=jax.ShapeDtypeStruct((m, n), F32),
        scratch_shapes=[pltpu.VMEM((tm, d), BF16)],
        compiler_params=_cp(("parallel", "arbitrary")),
        name="norm_mm",
    )(x, g.reshape(1, d), w)


def _qkv_kernel(x_ref, g_ref, w_ref, q_ref, k32_ref, k16_ref, v32_ref, v16_ref, xn_ref, *, nq, q_scale, v_transposed):
    j = pl.program_id(1)

    @pl.when(j == 0)
    def _():
        xn_ref[...] = _rms(x_ref[...], g_ref[...]).astype(BF16)

    y = _dot(xn_ref[...], w_ref[...])

    @pl.when(j < nq)
    def _():
        q_ref[...] = (y * q_scale).astype(BF16)

    @pl.when((j >= nq) & (j < 2 * nq))
    def _():
        k32_ref[...] = y
        k16_ref[...] = y.astype(BF16)

    @pl.when(j >= 2 * nq)
    def _():
        v32_ref[...] = y
        if v_transposed:
            v16_ref[0] = y.T.astype(BF16)
        else:
            v16_ref[...] = y.astype(BF16)


def qkv_proj(x, g, w, q_scale, v_transposed, tm):
    m, d = x.shape
    n = w.shape[1] // 3
    tm, tn = min(tm, m), min(CFG["mm_tn"], n)
    nq = n // tn
    spec = lambda off: pl.BlockSpec((tm, tn), lambda i, j: (i, jnp.clip(j - off, 0, nq - 1)))
    if v_transposed:
        v16_spec = pl.BlockSpec((1, tn, tm), lambda i, j: (i, jnp.clip(j - 2 * nq, 0, nq - 1), 0))
        v16_shape = jax.ShapeDtypeStruct((m // tm, n, tm), BF16)
    else:
        v16_spec, v16_shape = spec(2 * nq), jax.ShapeDtypeStruct((m, n), BF16)
    return pl.pallas_call(
        functools.partial(_qkv_kernel, nq=nq, q_scale=q_scale, v_transposed=v_transposed),
        grid=(m // tm, 3 * nq),
        in_specs=[pl.BlockSpec((tm, d), lambda i, j: (i, 0)),
                  pl.BlockSpec((1, d), lambda i, j: (0, 0)),
                  pl.BlockSpec((d, tn), lambda i, j: (0, j))],
        out_specs=[spec(0), spec(nq), spec(nq), spec(2 * nq), v16_spec],
        out_shape=[jax.ShapeDtypeStruct((m, n), BF16), jax.ShapeDtypeStruct((m, n), F32),
                   jax.ShapeDtypeStruct((m, n), BF16), jax.ShapeDtypeStruct((m, n), F32), v16_shape],
        scratch_shapes=[pltpu.VMEM((tm, d), BF16)],
        compiler_params=_cp(("parallel", "arbitrary")),
        name="qkv_proj",
    )(x, g.reshape(1, d), w)


def _mm_res_kernel(a_ref, w_ref, r_ref, o_ref):
    o_ref[...] = r_ref[...] + _dot(a_ref[...], w_ref[...])


def mm_res(a, w, r):
    m, k = a.shape
    n = w.shape[1]
    tm, tn = min(CFG["mm_tm"], m), min(CFG["mm_tn"], n)
    return pl.pallas_call(
        _mm_res_kernel,
        grid=(m // tm, n // tn),
        in_specs=[pl.BlockSpec((tm, k), lambda i, j: (i, 0)),
                  pl.BlockSpec((k, tn), lambda i, j: (0, j)),
                  pl.BlockSpec((tm, tn), lambda i, j: (i, j))],
        out_specs=pl.BlockSpec((tm, tn), lambda i, j: (i, j)),
        out_shape=jax.ShapeDtypeStruct((m, n), F32),
        compiler_params=_cp(("parallel", "arbitrary")),
        name="mm_res",
    )(a, w, r)


def _ffn_kernel(x_ref, g_ref, wg_ref, wu_ref, wd_ref, gf_ref, o_ref, xn_ref, *, final_norm):
    j = pl.program_id(1)

    @pl.when(j == 0)
    def _():
        x = x_ref[...]
        xn_ref[...] = _rms(x, g_ref[...]).astype(BF16)
        o_ref[...] = x

    xn = xn_ref[...]
    a = _dot(xn, wg_ref[...])
    b = _dot(xn, wu_ref[...])
    h = (a * jax.nn.sigmoid(a) * b).astype(BF16)
    o_ref[...] += _dot(h, wd_ref[...])

    if final_norm:
        @pl.when(j == pl.num_programs(1) - 1)
        def _():
            o_ref[...] = _rms(o_ref[...], gf_ref[...])


def ffn(x, g, wg, wu, wd, layer, g_final=None):
    m, d = x.shape
    f = wg.shape[2]
    tm, tf = min(CFG["ffn_tm"], m), min(CFG["ffn_tf"], f)
    final_norm = g_final is not None
    gf = (g_final if final_norm else g).reshape(1, d)
    return pl.pallas_call(
        functools.partial(_ffn_kernel, final_norm=final_norm),
        grid=(m // tm, f // tf),
        in_specs=[pl.BlockSpec((tm, d), lambda i, j: (i, 0)),
                  pl.BlockSpec((1, d), lambda i, j: (0, 0)),
                  pl.BlockSpec((None, d, tf), lambda i, j: (layer, 0, j)),
                  pl.BlockSpec((None, d, tf), lambda i, j: (layer, 0, j)),
                  pl.BlockSpec((None, tf, d), lambda i, j: (layer, j, 0)),
                  pl.BlockSpec((1, d), lambda i, j: (0, 0))],
        out_specs=pl.BlockSpec((tm, d), lambda i, j: (i, 0)),
        out_shape=jax.ShapeDtypeStruct((m, d), F32),
        scratch_shapes=[pltpu.VMEM((tm, d), BF16)],
        compiler_params=_cp(("parallel", "arbitrary")),
        name="ffn",
    )(x, g.reshape(1, d), wg, wu, wd, gf)


def _hgrn_levels(c):
    lv, h = [], c // 2
    while h >= 8:
        lv.append(h)
        h //= 2
    return lv


def _bcast_rows(g, rows, rep):
    return jnp.concatenate([jnp.broadcast_to(g[r:r + 1, :], (rep, g.shape[1])) for r in rows], axis=0)


def _hgrn_kernel(q_ref, f_ref, v_ref, gt_ref, lbl_ref, gain_ref, s0_ref, l_ref, og_ref, st_ref, stt_ref,
                 *, c, n_sub, hb, layer):
    cb = pl.program_id(2)
    dk = HG_DK

    @pl.when(cb == 0)
    def _():
        for h in range(hb):
            stt_ref[h] = s0_ref[0, h].T

    lg = lbl_ref[...]
    e = jnp.exp(lg - jnp.max(lg, axis=0, keepdims=True))
    p = e / jnp.sum(e, axis=0, keepdims=True)
    lb = p[0:1, :]
    for i in range(1, layer + 1):
        lb = lb + p[i:i + 1, :]
    lb = jnp.maximum(lb - p[0:1, :], 0.0)

    ti = lax.broadcasted_iota(jnp.int32, (c, c), 0)
    si = lax.broadcasted_iota(jnp.int32, (c, c), 1)
    levels = _hgrn_levels(c)
    masks = []
    for hh in levels:
        sh = int(math.log2(2 * hh))
        masks.append(((ti >> sh) == (si >> sh)) & ((ti & (2 * hh - 1)) >= hh) & ((si & (2 * hh - 1)) < hh))
    mask_loc = ((ti >> 3) == (si >> 3)) & (si <= ti)
    lmat = l_ref[...]
    gain = gain_ref[...]

    def chunk(ci, carry):
        r0 = pl.multiple_of(ci * c, c)
        fp = f_ref[pl.ds(r0, c), :]
        f = lb + (1.0 - lb) * jax.nn.sigmoid(fp)
        g = jnp.log2(jnp.maximum(f, F_MIN))
        kk = 1.0 - f
        g_hi = g.astype(BF16)
        r1 = g - g_hi.astype(F32)
        g_mid = r1.astype(BF16)
        g_lo = (r1 - g_mid.astype(F32)).astype(BF16)
        gc = _dot(lmat, g_hi) + _dot(lmat, g_mid) + _dot(lmat, g_lo)
        qa = q_ref[pl.ds(r0, c), :]
        va = v_ref[pl.ds(r0, c), :]
        ga = gt_ref[pl.ds(r0, c), :]
        for h in range(hb):
            sl = slice(h * dk, (h + 1) * dk)
            gh, qh, kh = gc[:, sl], qa[:, sl], kk[:, sl]
            vh = va[:, sl].astype(BF16)
            stt = stt_ref[h]
            inter = _dot_nt((qh * jnp.exp2(gh)).astype(BF16), stt.astype(BF16))
            g_last = gh[c - 1:c, :]
            k_dec = (kh * jnp.exp2(g_last - gh)).astype(BF16)
            stt_ref[h] = stt * jnp.exp2(g_last) + lax.dot_general(
                vh, k_dec, (((0,), (0,)), ((), ())), preferred_element_type=F32)
            q16, k16 = qh.astype(BF16), kh.astype(BF16)
            g_loc = _bcast_rows(gh, [8 * b + 3 for b in range(c // 8)], 8)
            sc = jnp.where(mask_loc,
                           _dot_nt(q16 * jnp.exp2(gh - g_loc).astype(BF16),
                                   k16 * jnp.exp2(g_loc - gh).astype(BF16)), 0.0)
            for hh, mk in zip(levels, masks):
                g_mid_rows = _bcast_rows(gh, [b * 2 * hh + hh - 1 for b in range(c // (2 * hh))], 2 * hh)
                fac = jnp.exp2(-jnp.abs(gh - g_mid_rows)).astype(BF16)
                sc = jnp.where(mk, _dot_nt(q16 * fac, k16 * fac), sc)
            o = inter + _dot(sc.astype(BF16), vh)
            on = _rms(o, gain)
            gate = ga[:, sl]
            og_ref[pl.ds(r0, c), sl] = (on * (gate * jax.nn.sigmoid(gate))).astype(BF16)
        return carry

    lax.fori_loop(0, n_sub, chunk, 0, unroll=min(CFG["hg_unroll"], n_sub))

    @pl.when(cb == pl.num_programs(2) - 1)
    def _():
        for h in range(hb):
            st_ref[0, h] = stt_ref[h].T


def hgrn_recurrence(proj, lb_logits, gain, s0, batch, t, layer):
    s0, s0_layer = s0
    d = HG_HEADS * HG_DK
    c = min(CFG["hg_chunk"], t)
    rows = min(CFG["hg_rows"], t)
    hb = CFG["hg_heads"] if t > c else HG_HEADS
    w = hb * HG_DK
    ncb = t // rows
    n_layers = lb_logits.shape[0]
    lmat = jnp.asarray(np.tril(np.ones((c, c), np.float32)), BF16)

    def col(sec):
        return lambda b, hg, cb: (b * ncb + cb, sec * (d // w) + hg)

    og, st = pl.pallas_call(
        functools.partial(_hgrn_kernel, c=c, n_sub=rows // c, hb=hb, layer=layer),
        grid=(batch, HG_HEADS // hb, ncb),
        in_specs=[pl.BlockSpec((rows, w), col(0)),
                  pl.BlockSpec((rows, w), col(1)),
                  pl.BlockSpec((rows, w), col(2)),
                  pl.BlockSpec((rows, w), col(3)),
                  pl.BlockSpec((n_layers, w), lambda b, hg, cb: (0, hg)),
                  pl.BlockSpec((1, HG_DK), lambda b, hg, cb: (0, 0)),
                  pl.BlockSpec((None, 1, hb, HG_DK, HG_DK), lambda b, hg, cb: (s0_layer, b, hg, 0, 0)),
                  pl.BlockSpec((c, c), lambda b, hg, cb: (0, 0))],
        out_specs=[pl.BlockSpec((rows, w), lambda b, hg, cb: (b * ncb + cb, hg)),
                   pl.BlockSpec((1, hb, HG_DK, HG_DK), lambda b, hg, cb: (b, hg, 0, 0))],
        out_shape=[jax.ShapeDtypeStruct((batch * t, d), BF16),
                   jax.ShapeDtypeStruct((batch, HG_HEADS, HG_DK, HG_DK), F32)],
        scratch_shapes=[pltpu.VMEM((hb, HG_DK, HG_DK), F32)],
        compiler_params=_cp(("parallel", "parallel", "arbitrary")),
        name="hgrn_recurrence",
    )(proj, proj, proj, proj, lb_logits, gain.reshape(1, HG_DK), s0, lmat)
    return og, st


def _pool_kernel(x_ref, g_ref, hist_ref, w_ref, sc_ref, o_ref, st_ref, buf_ref, *, tm, pos0):
    tb = pl.program_id(1)
    x = x_ref[...]
    xn = _rms(x, g_ref[...])
    hp = POOL_HIST + 1

    @pl.when(tb == 0)
    def _():
        buf_ref[0:hp, :] = hist_ref[0]

    @pl.when(tb > 0)
    def _():
        buf_ref[0:hp, :] = buf_ref[tm:tm + hp, :]

    buf_ref[hp:hp + tm, :] = xn
    st_ref[0] = buf_ref[tm:tm + hp, :]
    pos = pos0 + tb * tm + lax.broadcasted_iota(jnp.int32, (tm, 1), 0)
    gc = x.shape[1] // len(POOL_WINDOWS)
    for gi, win in enumerate(POOL_WINDOWS):
        cs = slice(gi * gc, (gi + 1) * gc)
        ws = xn[:, cs]
        for sft in range(1, win):
            ws = ws + buf_ref[hp - sft:hp - sft + tm, cs]
        cnt = jnp.minimum(win, pos + 1).astype(F32)
        pooled = (ws / cnt - xn[:, cs]).astype(BF16)
        o_ref[:, cs] = x[:, cs] + _dot(pooled, w_ref[gi]) * sc_ref[:, cs]


def pool_mixer(x, g, hist, w, scale, batch, t, pos0):
    d = x.shape[1]
    tm = min(CFG["pool_tm"], t)
    nb = t // tm
    hp = POOL_HIST + 1
    hist16 = jnp.concatenate([jnp.zeros((batch, 1, d), F32), hist], axis=1)
    ng, gc = w.shape[0], w.shape[1]
    return pl.pallas_call(
        functools.partial(_pool_kernel, tm=tm, pos0=pos0),
        grid=(batch, nb),
        in_specs=[pl.BlockSpec((tm, d), lambda b, i: (b * nb + i, 0)),
                  pl.BlockSpec((1, d), lambda b, i: (0, 0)),
                  pl.BlockSpec((1, hp, d), lambda b, i: (b, 0, 0)),
                  pl.BlockSpec((ng, gc, gc), lambda b, i: (0, 0, 0)),
                  pl.BlockSpec((1, d), lambda b, i: (0, 0))],
        out_specs=[pl.BlockSpec((tm, d), lambda b, i: (b * nb + i, 0)),
                   pl.BlockSpec((1, hp, d), lambda b, i: (b, 0, 0))],
        out_shape=[jax.ShapeDtypeStruct(x.shape, F32),
                   jax.ShapeDtypeStruct((batch, hp, d), F32)],
        scratch_shapes=[pltpu.VMEM((tm + hp, d), F32)],
        compiler_params=_cp(("parallel", "arbitrary")),
        name="pool_mixer",
    )(x, g.reshape(1, d), hist16, w, scale.reshape(1, d))


def _rel_bucket_np(rel):
    nb = REL_BUCKETS // 2
    max_exact = nb // 2
    n = np.abs(rel)
    large = max_exact + (np.log(np.maximum(n, max_exact).astype(np.float64) / max_exact)
                         / math.log(REL_MAX_DIST / max_exact) * (nb - max_exact)).astype(np.int64)
    large = np.minimum(large, nb - 1)
    return (np.where(rel > 0, nb, 0) + np.where(n < max_exact, n, large)).astype(np.int32)


def _bucket_tile(q_pos, k_pos):
    q_pos, k_pos = np.asarray(q_pos)[:, None], np.asarray(k_pos)[None, :]
    b = _rel_bucket_np(k_pos - q_pos)
    return np.where((k_pos // ATT_CHUNK) <= (q_pos // ATT_CHUNK), b, -1).astype(np.int32)


def _bias_kernel(tab_ref, bk_ref, o_ref, *, present):
    h = pl.program_id(0)
    rows = bk_ref.shape[0] // len(present)
    for k, values in enumerate(present):
        bk = bk_ref[k * rows:(k + 1) * rows, :]
        bias = jnp.full(bk.shape, MASK_VALUE, F32)
        for b in values:
            bias = jnp.where(bk == b, tab_ref[b, h] * LOG2E, bias)
        o_ref[0, k * rows:(k + 1) * rows, :] = bias


def bias_tiles(table, bucket, bands=1):
    r, c = bucket.shape
    nh = table.shape[1]
    present = tuple(tuple(int(b) for b in np.unique(band) if b >= 0) for band in np.split(bucket, bands, axis=0))
    return pl.pallas_call(
        functools.partial(_bias_kernel, present=present),
        grid=(nh,),
        in_specs=[pl.BlockSpec(memory_space=pltpu.SMEM),
                  pl.BlockSpec((r, c), lambda h: (0, 0))],
        out_specs=pl.BlockSpec((1, r, c), lambda h: (h, 0, 0)),
        out_shape=jax.ShapeDtypeStruct((nh, r, c), F32),
        compiler_params=_cp(("arbitrary",)),
        name="bias_tiles",
    )(table, jnp.asarray(bucket))


def _lambda(lq1, lk1, lq2, lk2, lambda_init):
    return (jnp.exp(jnp.sum(lq1 * lk1, axis=-1, keepdims=True))
            - jnp.exp(jnp.sum(lq2 * lk2, axis=-1, keepdims=True)) + lambda_init)


def _softmax_step(q, k, v, bias, m_ref, l_ref, acc_ref, idx):
    s = _dot_nt(q, k) + bias
    m_prev = m_ref[idx]
    m_new = jnp.maximum(m_prev, jnp.max(s, axis=-1, keepdims=True))
    p = jnp.exp2(s - m_new)
    alpha = jnp.exp2(m_prev - m_new)
    l_ref[idx] = alpha * l_ref[idx] + jnp.sum(p, axis=-1, keepdims=True)
    acc_ref[idx] = alpha * acc_ref[idx] + _dot(p.astype(BF16), v)
    m_ref[idx] = m_new


def _flash_kernel(q_ref, k_ref, vt_ref, bias_ref, lam_ref, gain_ref, o_ref,
                  m_ref, acc_ref, sa_ref, sb_ref, pa_ref, pb_ref, ala_ref, alb_ref, *, tq, tk, lambda_init):
    i = pl.program_id(1)
    dh = DA_DH
    m_ref[...] = jnp.full(m_ref.shape, -jnp.inf, F32)
    acc_ref[...] = jnp.zeros(acc_ref.shape, F32)
    q = q_ref[...]

    nkb = vt_ref.shape[0]

    def scores(j, s_ref):
        jc = jnp.minimum(j, nkb - 1)
        kb = k_ref[pl.ds(pl.multiple_of(jc * tk, tk), tk), :]
        for c in range(2):
            s_ref[c] = _dot_nt(kb[:, c * dh:(c + 1) * dh], q[:, c * dh:(c + 1) * dh])

    def softmax(j, s_ref, p_ref, al_ref):
        bias = bias_ref[0, jnp.clip(j - (i - 2), 0, 3)]
        for c in range(2):
            s = s_ref[c] + bias
            m_prev = m_ref[c]
            m_new = jnp.maximum(m_prev, jnp.max(s, axis=0, keepdims=True))
            p_ref[c] = jnp.exp2(s - m_new).astype(BF16)
            al_ref[c] = jnp.exp2(m_prev - m_new)
            m_ref[c] = m_new

    ones_rows = jnp.ones((ONES_ROWS, tk), BF16)

    def values(j, p_ref, al_ref):
        vt = jnp.concatenate([vt_ref[jnp.minimum(j, nkb - 1)], ones_rows], axis=0)
        for c in range(2):
            acc_ref[c] = al_ref[c] * acc_ref[c] + _dot(vt, p_ref[c])

    scores(0, sa_ref)
    scores(1, sb_ref)
    softmax(0, sa_ref, pa_ref, ala_ref)

    def pair(t, carry):
        j = 2 * t
        scores(j + 2, sa_ref)
        softmax(j + 1, sb_ref, pb_ref, alb_ref)
        values(j, pa_ref, ala_ref)
        scores(j + 3, sb_ref)
        softmax(j + 2, sa_ref, pa_ref, ala_ref)
        values(j + 1, pb_ref, alb_ref)
        return carry

    lax.fori_loop(0, (i + 2) // 2, pair, 0)

    lam = _lambda(lam_ref[0:1, :], lam_ref[1:2, :], lam_ref[2:3, :], lam_ref[3:4, :], lambda_init)
    hw = 2 * dh
    o = (acc_ref[0, 0:hw, :] / acc_ref[0, hw:hw + 1, :]
         - lam * (acc_ref[1, 0:hw, :] / acc_ref[1, hw:hw + 1, :]))
    on = o * lax.rsqrt(jnp.mean(o * o, axis=0, keepdims=True) + EPS) * (1.0 - lambda_init)
    o_ref[...] = (on.T * gain_ref[...]).astype(BF16)


def flash_diff_attention(q, k, vt, table, lam_params, gain, lambda_init):
    t, d = q.shape
    tq, tk = min(CFG["fa_tq"], t), min(CFG["fa_tk"], t)
    assert tq == tk and tk % ATT_CHUNK == 0 and vt.shape == (t // tk, d, tk)
    hw = 2 * DA_DH
    nh = d // hw
    far = _rel_bucket_np(-np.arange(tk + 1, max(t, tk + 2)))
    far_bucket = int(far[0])
    assert (far == far_bucket).all()
    near = _bucket_tile(tk + np.arange(tq), np.arange(2 * tk)).T
    bucket = np.concatenate([np.full((tk, tq), far_bucket, np.int32), near,
                             np.full((tk, tq), -1, np.int32)], axis=0)
    bank = bias_tiles(table, bucket, bands=4).reshape(nh, 4, tk, tq)
    once = pl.Buffered(1)
    return pl.pallas_call(
        functools.partial(_flash_kernel, tq=tq, tk=tk, lambda_init=lambda_init),
        grid=(nh, t // tq),
        in_specs=[pl.BlockSpec((tq, hw), lambda h, i: (i, h)),
                  pl.BlockSpec((t, hw), lambda h, i: (0, h), pipeline_mode=once),
                  pl.BlockSpec((t // tk, hw, tk), lambda h, i: (0, h, 0), pipeline_mode=once),
                  pl.BlockSpec((1, 4, tk, tq), lambda h, i: (h, 0, 0, 0), pipeline_mode=once),
                  pl.BlockSpec((4, DA_DH), lambda h, i: (0, 0)),
                  pl.BlockSpec((1, hw), lambda h, i: (0, 0))],
        out_specs=pl.BlockSpec((tq, hw), lambda h, i: (i, h)),
        out_shape=jax.ShapeDtypeStruct((t, d), BF16),
        scratch_shapes=[pltpu.VMEM((2, 1, tq), F32),
                        pltpu.VMEM((2, hw + ONES_ROWS, tq), F32),
                        pltpu.VMEM((2, tk, tq), F32), pltpu.VMEM((2, tk, tq), F32),
                        pltpu.VMEM((2, tk, tq), BF16), pltpu.VMEM((2, tk, tq), BF16),
                        pltpu.VMEM((2, 1, tq), F32), pltpu.VMEM((2, 1, tq), F32)],
        compiler_params=_cp(("parallel", "arbitrary")),
        name="flash_diff_attention",
    )(q, k, vt, bank, lam_params, gain.reshape(1, hw))


def _sample_attn_kernel(tab_ref, q_ref, ck_ref, cvl_ref, cvh_ref, kn_ref, vn_ref, blast_ref, bnew_ref, lam_ref,
                        gain_ref, o_ref, m_ref, l_ref, acc_ref, *, tk, far_bucket, lambda_init):
    j = pl.program_id(1)
    nj = pl.num_programs(1)
    dh = DA_DH
    hw = 2 * dh
    nh = q_ref.shape[1] // hw
    ts = q_ref.shape[0]

    @pl.when(j == 0)
    def _():
        m_ref[...] = jnp.full(m_ref.shape, -jnp.inf, F32)
        l_ref[...] = jnp.zeros(l_ref.shape, F32)
        acc_ref[...] = jnp.zeros(acc_ref.shape, F32)

    q = q_ref[...]

    def block(k_of, v_of, bias_fn):
        for h in range(nh):
            vb = v_of(h)
            for c in range(2):
                cs = slice((2 * h + c) * dh, (2 * h + c + 1) * dh)
                _softmax_step(q[:, cs], k_of(2 * h + c), vb, bias_fn(h), m_ref, l_ref, acc_ref, 2 * h + c)

    def cache_k_of(hc):
        return ck_ref[0, pl.ds(hc, tk, stride=2 * nh), :].astype(BF16)

    def cache_v_of(h):
        return jnp.concatenate([cvl_ref[0, pl.ds(h, tk, stride=nh), :],
                                cvh_ref[0, pl.ds(h, tk, stride=nh), :]], axis=1).astype(BF16)

    @pl.when(j < nj - 1)
    def _():
        block(cache_k_of, cache_v_of, lambda h: tab_ref[far_bucket, h] * LOG2E)

    @pl.when(j == nj - 1)
    def _():
        block(cache_k_of, cache_v_of, lambda h: blast_ref[h])
        block(lambda hc: kn_ref[:, hc * dh:(hc + 1) * dh], lambda h: vn_ref[:, h * hw:(h + 1) * hw],
              lambda h: bnew_ref[h][:, :ts])
        lam = _lambda(lam_ref[0:1, :], lam_ref[1:2, :], lam_ref[2:3, :], lam_ref[3:4, :], lambda_init)
        for h in range(nh):
            o = acc_ref[2 * h] / l_ref[2 * h] - lam * (acc_ref[2 * h + 1] / l_ref[2 * h + 1])
            o_ref[:, h * hw:(h + 1) * hw] = (_rms(o, gain_ref[...]) * (1.0 - lambda_init)).astype(BF16)


def sample_diff_attention(q, k_new, v_new, cache_k, cache_v, layer, table, lam_params, gain, lambda_init):
    n_layers, batch, past, nhc, dh = cache_k.shape
    d = nhc * dh
    ts = q.shape[0] // batch
    hw = 2 * DA_DH
    nh = d // hw
    tk = min(CFG["sa_tk"], past)
    nkb = past // tk
    q_pos = past + np.arange(ts)
    blast = bias_tiles(table, _bucket_tile(q_pos, past - tk + np.arange(tk)))
    bnew = bias_tiles(table, _bucket_tile(q_pos, past + np.arange(128)))
    if nkb > 1:
        far = _bucket_tile(q_pos, np.arange(past - tk))
        far_bucket = int(far[0, 0])
        assert (far == far_bucket).all()
    else:
        far_bucket = 0
    cache_v_rows = cache_v.reshape(n_layers, batch, past * nh, hw)
    return pl.pallas_call(
        functools.partial(_sample_attn_kernel, tk=tk, far_bucket=far_bucket, lambda_init=lambda_init),
        grid=(batch, nkb),
        in_specs=[pl.BlockSpec(memory_space=pltpu.SMEM),
                  pl.BlockSpec((ts, d), lambda b, j: (b, 0)),
                  pl.BlockSpec((None, 1, tk * nhc, dh), lambda b, j: (layer, b, j, 0)),
                  pl.BlockSpec((None, 1, tk * nh, dh), lambda b, j: (layer, b, j, 0)),
                  pl.BlockSpec((None, 1, tk * nh, dh), lambda b, j: (layer, b, j, 1)),
                  pl.BlockSpec((ts, d), lambda b, j: (b, 0)),
                  pl.BlockSpec((ts, d), lambda b, j: (b, 0)),
                  pl.BlockSpec((nh, ts, tk), lambda b, j: (0, 0, 0)),
                  pl.BlockSpec((nh, ts, 128), lambda b, j: (0, 0, 0)),
                  pl.BlockSpec((4, DA_DH), lambda b, j: (0, 0)),
                  pl.BlockSpec((1, hw), lambda b, j: (0, 0))],
        out_specs=pl.BlockSpec((ts, d), lambda b, j: (b, 0)),
        out_shape=jax.ShapeDtypeStruct((batch * ts, d), BF16),
        scratch_shapes=[pltpu.VMEM((2 * nh, ts, 1), F32), pltpu.VMEM((2 * nh, ts, 1), F32),
                        pltpu.VMEM((2 * nh, ts, hw), F32)],
        compiler_params=_cp(("parallel", "arbitrary")),
        name="sample_diff_attention",
    )(table, q, cache_k.reshape(n_layers, batch, past * nhc, dh), cache_v_rows, cache_v_rows,
      k_new, v_new, blast, bnew, lam_params, gain.reshape(1, hw))


def kernel(x_prompt, x_sample, state_hgrn, state_pool, cache_k, cache_v, norm_mix, norm_ffn, norm_final, hgrn_w_q, hgrn_w_f, hgrn_w_i, hgrn_w_g, hgrn_w_o, hgrn_lb_logits, hgrn_norm_gain, pool_w, pool_scale, attn_w_q, attn_w_k, attn_w_v, attn_w_o, attn_lambda_q1, attn_lambda_k1, attn_lambda_q2, attn_lambda_k2, attn_subln_gain, rel_bias_table, ffn_w_gate, ffn_w_up, ffn_w_down):
    bp, tp, d = x_prompt.shape
    bs, ts, _ = x_sample.shape
    past = cache_k.shape[2]
    depth = norm_mix.shape[0]
    assert bp == 1
    bf = lambda a: a.astype(BF16)
    xs = [x_prompt.reshape(bp * tp, d), x_sample.reshape(bs * ts, d)]
    dims = [(bp, tp), (bs, ts)]
    hg, pool_st, k_out, v_out = [[], []], [[], []], [[], []], [[], []]
    w_g, w_u, w_d = bf(ffn_w_gate), bf(ffn_w_up), bf(ffn_w_down)
    zero_state = jnp.zeros((1, bp, HG_HEADS, HG_DK, HG_DK), F32)

    for i in range(depth):
        m, j = i % N_MIXERS, i // N_MIXERS
        if m == 0:
            w_cat = bf(jnp.concatenate([hgrn_w_q[j], hgrn_w_f[j], hgrn_w_i[j], hgrn_w_g[j]], axis=1))
            w_o = bf(hgrn_w_o[j])
            s0s = [(zero_state, 0), (state_hgrn, j)]
            for r in range(2):
                b, t = dims[r]
                proj = norm_mm(xs[r], norm_mix[i], w_cat)
                og, st = hgrn_recurrence(proj, hgrn_lb_logits, hgrn_norm_gain[j], s0s[r], b, t, j)
                xs[r] = mm_res(og, w_o, xs[r])
                hg[r].append(st)
        elif m == 1:
            w_p = bf(pool_w[j])
            hists = [jnp.zeros((bp, POOL_HIST, d), F32), state_pool[j]]
            for r in range(2):
                b, t = dims[r]
                xs[r], st = pool_mixer(xs[r], norm_mix[i], hists[r], w_p, pool_scale[j], b, t, (0, past)[r])
                pool_st[r].append(st[:, 1:, :])
        else:
            lambda_init = 0.8 - 0.6 * math.exp(-0.3 * i)
            w_qkv = bf(jnp.concatenate([attn_w_q[j], attn_w_k[j], attn_w_v[j]], axis=1))
            w_o = bf(attn_w_o[j])
            lam_params = jnp.stack([attn_lambda_q1[j], attn_lambda_k1[j], attn_lambda_q2[j], attn_lambda_k2[j]])
            for r in range(2):
                b, t = dims[r]
                q, k32, k16, v32, v16 = qkv_proj(xs[r], norm_mix[i], w_qkv, DA_DH ** -0.5 * LOG2E, r == 0,
                                                 CFG["fa_tk"])
                if r == 0:
                    o = flash_diff_attention(q, k16, v16, rel_bias_table, lam_params, attn_subln_gain[j],
                                             lambda_init)
                else:
                    o = sample_diff_attention(q, k16, v16, cache_k, cache_v, j, rel_bias_table, lam_params,
                                              attn_subln_gain[j], lambda_init)
                xs[r] = mm_res(o, w_o, xs[r])
                k_out[r].append(k32.reshape(b, t, 2 * DA_HEADS, DA_DH))
                v_out[r].append(v32.reshape(b, t, DA_HEADS, 2 * DA_DH))
        g_final = norm_final if i == depth - 1 else None
        for r in range(2):
            xs[r] = ffn(xs[r], norm_ffn[i], w_g, w_u, w_d, i, g_final)

    return (xs[0].reshape(bp, tp, d), xs[1].reshape(bs, ts, d),
            jnp.stack(hg[0]), jnp.stack(hg[1]), jnp.stack(pool_st[0]), jnp.stack(pool_st[1]),
            jnp.stack(k_out[0]), jnp.stack(v_out[0]), jnp.stack(k_out[1]), jnp.stack(v_out[1]))
```

```python
import functools
import math

import numpy as np
import jax
import jax.numpy as jnp
from jax import lax
from jax.experimental import pallas as pl
from jax.experimental.pallas import tpu as pltpu

F32 = jnp.float32
BF16 = jnp.bfloat16

EPS = 1e-6
LOG2E = math.log2(math.e)
F_MIN = 1e-6
MASK_VALUE = -1e30
HG_HEADS = 16
HG_DK = 128
POOL_WINDOWS = (2, 4, 8, 16)
POOL_HIST = 15
DA_HEADS = 8
DA_DH = 128
ATT_CHUNK = 64
REL_BUCKETS = 32
REL_MAX_DIST = 128
N_MIXERS = 3

CFG = dict(
    mm_tm=1024, mm_tn=1024,
    ffn_tm=1024, ffn_tf=256,
    pool_tm=512,
    hg_chunk=256, hg_rows=512, hg_heads=4, hg_unroll=2,
    fa_tq=512, fa_tk=512,
    sa_tk=1024,
    vmem=56 * 1024 * 1024,
)


def _cp(sem):
    return pltpu.CompilerParams(dimension_semantics=sem, vmem_limit_bytes=CFG["vmem"])


def _rms(x, g):
    return x * lax.rsqrt(jnp.mean(x * x, axis=-1, keepdims=True) + EPS) * g


def _dot(a, b):
    return jnp.dot(a, b, preferred_element_type=F32)


def _dot_nt(a, b):
    return lax.dot_general(a, b, (((1,), (1,)), ((), ())), preferred_element_type=F32)


def _norm_mm_kernel(x_ref, g_ref, w_ref, o_ref, xn_ref):
    @pl.when(pl.program_id(1) == 0)
    def _():
        xn_ref[...] = _rms(x_ref[...], g_ref[...]).astype(BF16)

    o_ref[...] = _dot(xn_ref[...], w_ref[...])


def norm_mm(x, g, w):
    m, d = x.shape
    n = w.shape[1]
    tm, tn = min(CFG["mm_tm"], m), min(CFG["mm_tn"], n)
    return pl.pallas_call(
        _norm_mm_kernel,
        grid=(m // tm, n // tn),
        in_specs=[pl.BlockSpec((tm, d), lambda i, j: (i, 0)),
                  pl.BlockSpec((1, d), lambda i, j: (0, 0)),
                  pl.BlockSpec((d, tn), lambda i, j: (0, j))],
        out_specs=pl.BlockSpec((tm, tn), lambda i, j: (i, j)),
        out_shape=jax.ShapeDtypeStruct((m, n), F32),
        scratch_shapes=[pltpu.VMEM((tm, d), BF16)],
        compiler_params=_cp(("parallel", "arbitrary")),
        name="norm_mm",
    )(x, g.reshape(1, d), w)


def _qkv_kernel(x_ref, g_ref, w_ref, q_ref, k32_ref, k16_ref, v32_ref, v16_ref, xn_ref, *, nq, q_scale, v_transposed):
    j = pl.program_id(1)

    @pl.when(j == 0)
    def _():
        xn_ref[...] = _rms(x_ref[...], g_ref[...]).astype(BF16)

    y = _dot(xn_ref[...], w_ref[...])

    @pl.when(j < nq)
    def _():
        q_ref[...] = (y * q_scale).astype(BF16)

    @pl.when((j >= nq) & (j < 2 * nq))
    def _():
        k32_ref[...] = y
        k16_ref[...] = y.astype(BF16)

    @pl.when(j >= 2 * nq)
    def _():
        v32_ref[...] = y
        if v_transposed:
            v16_ref[0] = y.T.astype(BF16)
        else:
            v16_ref[...] = y.astype(BF16)


def qkv_proj(x, g, w, q_scale, v_transposed, tm):
    m, d = x.shape
    n = w.shape[1] // 3
    tm, tn = min(tm, m), min(CFG["mm_tn"], n)
    nq = n // tn
    spec = lambda off: pl.BlockSpec((tm, tn), lambda i, j: (i, jnp.clip(j - off, 0, nq - 1)))
    if v_transposed:
        v16_spec = pl.BlockSpec((1, tn, tm), lambda i, j: (i, jnp.clip(j - 2 * nq, 0, nq - 1), 0))
        v16_shape = jax.ShapeDtypeStruct((m // tm, n, tm), BF16)
    else:
        v16_spec, v16_shape = spec(2 * nq), jax.ShapeDtypeStruct((m, n), BF16)
    return pl.pallas_call(
        functools.partial(_qkv_kernel, nq=nq, q_scale=q_scale, v_transposed=v_transposed),
        grid=(m // tm, 3 * nq),
        in_specs=[pl.BlockSpec((tm, d), lambda i, j: (i, 0)),
                  pl.BlockSpec((1, d), lambda i, j: (0, 0)),
                  pl.BlockSpec((d, tn), lambda i, j: (0, j))],
        out_specs=[spec(0), spec(nq), spec(nq), spec(2 * nq), v16_spec],
        out_shape=[jax.ShapeDtypeStruct((m, n), BF16), jax.ShapeDtypeStruct((m, n), F32),
                   jax.ShapeDtypeStruct((m, n), BF16), jax.ShapeDtypeStruct((m, n), F32), v16_shape],
        scratch_shapes=[pltpu.VMEM((tm, d), BF16)],
        compiler_params=_cp(("parallel", "arbitrary")),
        name="qkv_proj",
    )(x, g.reshape(1, d), w)


def _mm_res_kernel(a_ref, w_ref, r_ref, o_ref):
    o_ref[...] = r_ref[...] + _dot(a_ref[...], w_ref[...])


def mm_res(a, w, r):
    m, k = a.shape
    n = w.shape[1]
    tm, tn = min(CFG["mm_tm"], m), min(CFG["mm_tn"], n)
    return pl.pallas_call(
        _mm_res_kernel,
        grid=(m // tm, n // tn),
        in_specs=[pl.BlockSpec((tm, k), lambda i, j: (i, 0)),
                  pl.BlockSpec((k, tn), lambda i, j: (0, j)),
                  pl.BlockSpec((tm, tn), lambda i, j: (i, j))],
        out_specs=pl.BlockSpec((tm, tn), lambda i, j: (i, j)),
        out_shape=jax.ShapeDtypeStruct((m, n), F32),
        compiler_params=_cp(("parallel", "arbitrary")),
        name="mm_res",
    )(a, w, r)


def _ffn_kernel(x_ref, g_ref, wg_ref, wu_ref, wd_ref, gf_ref, o_ref, xn_ref, *, final_norm):
    j = pl.program_id(1)

    @pl.when(j == 0)
    def _():
        x = x_ref[...]
        xn_ref[...] = _rms(x, g_ref[...]).astype(BF16)
        o_ref[...] = x

    xn = xn_ref[...]
    a = _dot(xn, wg_ref[...])
    b = _dot(xn, wu_ref[...])
    h = (a * jax.nn.sigmoid(a) * b).astype(BF16)
    o_ref[...] += _dot(h, wd_ref[...])

    if final_norm:
        @pl.when(j == pl.num_programs(1) - 1)
        def _():
            o_ref[...] = _rms(o_ref[...], gf_ref[...])


def ffn(x, g, wg, wu, wd, layer, g_final=None):
    m, d = x.shape
    f = wg.shape[2]
    tm, tf = min(CFG["ffn_tm"], m), min(CFG["ffn_tf"], f)
    final_norm = g_final is not None
    gf = (g_final if final_norm else g).reshape(1, d)
    return pl.pallas_call(
        functools.partial(_ffn_kernel, final_norm=final_norm),
        grid=(m // tm, f // tf),
        in_specs=[pl.BlockSpec((tm, d), lambda i, j: (i, 0)),
                  pl.BlockSpec((1, d), lambda i, j: (0, 0)),
                  pl.BlockSpec((None, d, tf), lambda i, j: (layer, 0, j)),
                  pl.BlockSpec((None, d, tf), lambda i, j: (layer, 0, j)),
                  pl.BlockSpec((None, tf, d), lambda i, j: (layer, j, 0)),
                  pl.BlockSpec((1, d), lambda i, j: (0, 0))],
        out_specs=pl.BlockSpec((tm, d), lambda i, j: (i, 0)),
        out_shape=jax.ShapeDtypeStruct((m, d), F32),
        scratch_shapes=[pltpu.VMEM((tm, d), BF16)],
        compiler_params=_cp(("parallel", "arbitrary")),
        name="ffn",
    )(x, g.reshape(1, d), wg, wu, wd, gf)


def _hgrn_levels(c):
    lv, h = [], c // 2
    while h >= 8:
        lv.append(h)
        h //= 2
    return lv


def _bcast_rows(g, rows, rep):
    return jnp.concatenate([jnp.broadcast_to(g[r:r + 1, :], (rep, g.shape[1])) for r in rows], axis=0)


def _hgrn_kernel(q_ref, f_ref, v_ref, gt_ref, lbl_ref, gain_ref, s0_ref, l_ref, og_ref, st_ref, stt_ref,
                 *, c, n_sub, hb, layer):
    cb = pl.program_id(2)
    dk = HG_DK

    @pl.when(cb == 0)
    def _():
        for h in range(hb):
            stt_ref[h] = s0_ref[0, h].T

    lg = lbl_ref[...]
    e = jnp.exp(lg - jnp.max(lg, axis=0, keepdims=True))
    p = e / jnp.sum(e, axis=0, keepdims=True)
    lb = p[0:1, :]
    for i in range(1, layer + 1):
        lb = lb + p[i:i + 1, :]
    lb = jnp.maximum(lb - p[0:1, :], 0.0)

    ti = lax.broadcasted_iota(jnp.int32, (c, c), 0)
    si = lax.broadcasted_iota(jnp.int32, (c, c), 1)
    levels = _hgrn_levels(c)
    masks = []
    for hh in levels:
        sh = int(math.log2(2 * hh))
        masks.append(((ti >> sh) == (si >> sh)) & ((ti & (2 * hh - 1)) >= hh) & ((si & (2 * hh - 1)) < hh))
    mask_loc = ((ti >> 3) == (si >> 3)) & (si <= ti)
    lmat = l_ref[...]
    gain = gain_ref[...]

    def chunk(ci, carry):
        r0 = pl.multiple_of(ci * c, c)
        fp = f_ref[pl.ds(r0, c), :]
        f = lb + (1.0 - lb) * jax.nn.sigmoid(fp)
        g = jnp.log2(jnp.maximum(f, F_MIN))
        kk = 1.0 - f
        g_hi = g.astype(BF16)
        r1 = g - g_hi.astype(F32)
        g_mid = r1.astype(BF16)
        g_lo = (r1 - g_mid.astype(F32)).astype(BF16)
        gc = _dot(lmat, g_hi) + _dot(lmat, g_mid) + _dot(lmat, g_lo)
        qa = q_ref[pl.ds(r0, c), :]
        va = v_ref[pl.ds(r0, c), :]
        ga = gt_ref[pl.ds(r0, c), :]
        for h in range(hb):
            sl = slice(h * dk, (h + 1) * dk)
            gh, qh, kh = gc[:, sl], qa[:, sl], kk[:, sl]
            vh = va[:, sl].astype(BF16)
            stt = stt_ref[h]
            inter = _dot_nt((qh * jnp.exp2(gh)).astype(BF16), stt.astype(BF16))
            g_last = gh[c - 1:c, :]
            k_dec = (kh * jnp.exp2(g_last - gh)).astype(BF16)
            stt_ref[h] = stt * jnp.exp2(g_last) + lax.dot_general(
                vh, k_dec, (((0,), (0,)), ((), ())), preferred_element_type=F32)
            q16, k16 = qh.astype(BF16), kh.astype(BF16)
            g_loc = _bcast_rows(gh, [8 * b + 3 for b in range(c // 8)], 8)
            sc = jnp.where(mask_loc,
                           _dot_nt(q16 * jnp.exp2(gh - g_loc).astype(BF16),
                                   k16 * jnp.exp2(g_loc - gh).astype(BF16)), 0.0)
            for hh, mk in zip(levels, masks):
                g_mid_rows = _bcast_rows(gh, [b * 2 * hh + hh - 1 for b in range(c // (2 * hh))], 2 * hh)
                dist = lax.bitcast_convert_type(gh - g_mid_rows, jnp.uint32) | jnp.uint32(0x80000000)
                fac = jnp.exp2(lax.bitcast_convert_type(dist, F32)).astype(BF16)
                sc = jnp.where(mk, _dot_nt(q16 * fac, k16 * fac), sc)
            o = inter + _dot(sc.astype(BF16), vh)
            on = _rms(o, gain)
            gate = ga[:, sl]
            og_ref[pl.ds(r0, c), sl] = (on * (gate * jax.nn.sigmoid(gate))).astype(BF16)
        return carry

    lax.fori_loop(0, n_sub, chunk, 0, unroll=min(CFG["hg_unroll"], n_sub))

    @pl.when(cb == pl.num_programs(2) - 1)
    def _():
        for h in range(hb):
            st_ref[0, h] = stt_ref[h].T


def hgrn_recurrence(proj, lb_logits, gain, s0, batch, t, layer):
    s0, s0_layer = s0
    d = HG_HEADS * HG_DK
    c = min(CFG["hg_chunk"], t)
    rows = min(CFG["hg_rows"], t)
    hb = CFG["hg_heads"] if t > c else HG_HEADS
    w = hb * HG_DK
    ncb = t // rows
    n_layers = lb_logits.shape[0]
    lmat = jnp.asarray(np.tril(np.ones((c, c), np.float32)), BF16)

    def col(sec):
        return lambda b, hg, cb: (b * ncb + cb, sec * (d // w) + hg)

    og, st = pl.pallas_call(
        functools.partial(_hgrn_kernel, c=c, n_sub=rows // c, hb=hb, layer=layer),
        grid=(batch, HG_HEADS // hb, ncb),
        in_specs=[pl.BlockSpec((rows, w), col(0)),
                  pl.BlockSpec((rows, w), col(1)),
                  pl.BlockSpec((rows, w), col(2)),
                  pl.BlockSpec((rows, w), col(3)),
                  pl.BlockSpec((n_layers, w), lambda b, hg, cb: (0, hg)),
                  pl.BlockSpec((1, HG_DK), lambda b, hg, cb: (0, 0)),
                  pl.BlockSpec((None, 1, hb, HG_DK, HG_DK), lambda b, hg, cb: (s0_layer, b, hg, 0, 0)),
                  pl.BlockSpec((c, c), lambda b, hg, cb: (0, 0))],
        out_specs=[pl.BlockSpec((rows, w), lambda b, hg, cb: (b * ncb + cb, hg)),
                   pl.BlockSpec((1, hb, HG_DK, HG_DK), lambda b, hg, cb: (b, hg, 0, 0))],
        out_shape=[jax.ShapeDtypeStruct((batch * t, d), BF16),
                   jax.ShapeDtypeStruct((batch, HG_HEADS, HG_DK, HG_DK), F32)],
        scratch_shapes=[pltpu.VMEM((hb, HG_DK, HG_DK), F32)],
        compiler_params=_cp(("parallel", "parallel", "arbitrary")),
        name="hgrn_recurrence",
    )(proj, proj, proj, proj, lb_logits, gain.reshape(1, HG_DK), s0, lmat)
    return og, st


def _pool_kernel(x_ref, g_ref, hist_ref, w_ref, sc_ref, o_ref, st_ref, buf_ref, *, tm, pos0):
    tb = pl.program_id(1)
    x = x_ref[...]
    xn = _rms(x, g_ref[...])
    hp = POOL_HIST + 1

    @pl.when(tb == 0)
    def _():
        buf_ref[0:hp, :] = hist_ref[0]

    @pl.when(tb > 0)
    def _():
        buf_ref[0:hp, :] = buf_ref[tm:tm + hp, :]

    buf_ref[hp:hp + tm, :] = xn
    st_ref[0] = buf_ref[tm:tm + hp, :]
    pos = pos0 + tb * tm + lax.broadcasted_iota(jnp.int32, (tm, 1), 0)
    gc = x.shape[1] // len(POOL_WINDOWS)
    for gi, win in enumerate(POOL_WINDOWS):
        cs = slice(gi * gc, (gi + 1) * gc)
        ws = xn[:, cs]
        for sft in range(1, win):
            ws = ws + buf_ref[hp - sft:hp - sft + tm, cs]
        cnt = jnp.minimum(win, pos + 1).astype(F32)
        pooled = (ws / cnt - xn[:, cs]).astype(BF16)
        o_ref[:, cs] = x[:, cs] + _dot(pooled, w_ref[gi]) * sc_ref[:, cs]


def pool_mixer(x, g, hist, w, scale, batch, t, pos0):
    d = x.shape[1]
    tm = min(CFG["pool_tm"], t)
    nb = t // tm
    hp = POOL_HIST + 1
    hist16 = jnp.concatenate([jnp.zeros((batch, 1, d), F32), hist], axis=1)
    ng, gc = w.shape[0], w.shape[1]
    return pl.pallas_call(
        functools.partial(_pool_kernel, tm=tm, pos0=pos0),
        grid=(batch, nb),
        in_specs=[pl.BlockSpec((tm, d), lambda b, i: (b * nb + i, 0)),
                  pl.BlockSpec((1, d), lambda b, i: (0, 0)),
                  pl.BlockSpec((1, hp, d), lambda b, i: (b, 0, 0)),
                  pl.BlockSpec((ng, gc, gc), lambda b, i: (0, 0, 0)),
                  pl.BlockSpec((1, d), lambda b, i: (0, 0))],
        out_specs=[pl.BlockSpec((tm, d), lambda b, i: (b * nb + i, 0)),
                   pl.BlockSpec((1, hp, d), lambda b, i: (b, 0, 0))],
        out_shape=[jax.ShapeDtypeStruct(x.shape, F32),
                   jax.ShapeDtypeStruct((batch, hp, d), F32)],
        scratch_shapes=[pltpu.VMEM((tm + hp, d), F32)],
        compiler_params=_cp(("parallel", "arbitrary")),
        name="pool_mixer",
    )(x, g.reshape(1, d), hist16, w, scale.reshape(1, d))


def _rel_bucket_np(rel):
    nb = REL_BUCKETS // 2
    max_exact = nb // 2
    n = np.abs(rel)
    large = max_exact + (np.log(np.maximum(n, max_exact).astype(np.float64) / max_exact)
                         / math.log(REL_MAX_DIST / max_exact) * (nb - max_exact)).astype(np.int64)
    large = np.minimum(large, nb - 1)
    return (np.where(rel > 0, nb, 0) + np.where(n < max_exact, n, large)).astype(np.int32)


def _bucket_tile(q_pos, k_pos):
    q_pos, k_pos = np.asarray(q_pos)[:, None], np.asarray(k_pos)[None, :]
    b = _rel_bucket_np(k_pos - q_pos)
    return np.where((k_pos // ATT_CHUNK) <= (q_pos // ATT_CHUNK), b, -1).astype(np.int32)


def _bias_kernel(tab_ref, bk_ref, o_ref, *, present):
    h = pl.program_id(0)
    rows = bk_ref.shape[0] // len(present)
    for k, values in enumerate(present):
        bk = bk_ref[k * rows:(k + 1) * rows, :]
        bias = jnp.full(bk.shape, MASK_VALUE, F32)
        for b in values:
            bias = jnp.where(bk == b, tab_ref[b, h] * LOG2E, bias)
        o_ref[0, k * rows:(k + 1) * rows, :] = bias


def bias_tiles(table, bucket, bands=1):
    r, c = bucket.shape
    nh = table.shape[1]
    present = tuple(tuple(int(b) for b in np.unique(band) if b >= 0) for band in np.split(bucket, bands, axis=0))
    return pl.pallas_call(
        functools.partial(_bias_kernel, present=present),
        grid=(nh,),
        in_specs=[pl.BlockSpec(memory_space=pltpu.SMEM),
                  pl.BlockSpec((r, c), lambda h: (0, 0))],
        out_specs=pl.BlockSpec((1, r, c), lambda h: (h, 0, 0)),
        out_shape=jax.ShapeDtypeStruct((nh, r, c), F32),
        compiler_params=_cp(("arbitrary",)),
        name="bias_tiles",
    )(table, jnp.asarray(bucket))


def _lambda(lq1, lk1, lq2, lk2, lambda_init):
    return (jnp.exp(jnp.sum(lq1 * lk1, axis=-1, keepdims=True))
            - jnp.exp(jnp.sum(lq2 * lk2, axis=-1, keepdims=True)) + lambda_init)


def _softmax_step(q, k, v, bias, m_ref, l_ref, acc_ref, idx):
    s = _dot_nt(q, k) + bias
    m_prev = m_ref[idx]
    m_new = jnp.maximum(m_prev, jnp.max(s, axis=-1, keepdims=True))
    p = jnp.exp2(s - m_new)
    alpha = jnp.exp2(m_prev - m_new)
    l_ref[idx] = alpha * l_ref[idx] + jnp.sum(p, axis=-1, keepdims=True)
    acc_ref[idx] = alpha * acc_ref[idx] + _dot(p.astype(BF16), v)
    m_ref[idx] = m_new


def _flash_kernel(q_ref, k_ref, vt_ref, bias_ref, lam_ref, gain_ref, o_ref,
                  m_ref, l_ref, acc_ref, sa_ref, sb_ref, pa_ref, pb_ref, ala_ref, alb_ref, *, tq, tk, lambda_init):
    i = pl.program_id(1)
    dh = DA_DH
    m_ref[...] = jnp.full(m_ref.shape, -jnp.inf, F32)
    l_ref[...] = jnp.zeros(l_ref.shape, F32)
    acc_ref[...] = jnp.zeros(acc_ref.shape, F32)
    q = q_ref[...]

    nkb = vt_ref.shape[0]

    def scores(j, s_ref):
        jc = jnp.minimum(j, nkb - 1)
        kb = k_ref[pl.ds(pl.multiple_of(jc * tk, tk), tk), :]
        for c in range(2):
            s_ref[c] = _dot_nt(kb[:, c * dh:(c + 1) * dh], q[:, c * dh:(c + 1) * dh])

    def softmax(j, s_ref, p_ref, al_ref):
        bias = bias_ref[0, jnp.clip(j - (i - 2), 0, 3)]
        for c in range(2):
            s = s_ref[c] + bias
            m_prev = m_ref[c]
            m_new = jnp.maximum(m_prev, jnp.max(s, axis=0, keepdims=True))
            p = jnp.exp2(s - m_new)
            alpha = jnp.exp2(m_prev - m_new)
            l_ref[c] = alpha * l_ref[c] + jnp.sum(p, axis=0, keepdims=True)
            m_ref[c] = m_new
            al_ref[c] = alpha
            p_ref[c] = p.astype(BF16)

    def values(j, p_ref, al_ref):
        vt = vt_ref[jnp.minimum(j, nkb - 1)]
        for c in range(2):
            acc_ref[c] = al_ref[c] * acc_ref[c] + _dot(vt, p_ref[c])

    scores(0, sa_ref)
    scores(1, sb_ref)
    softmax(0, sa_ref, pa_ref, ala_ref)

    def pair(t, carry):
        j = 2 * t
        scores(j + 2, sa_ref)
        softmax(j + 1, sb_ref, pb_ref, alb_ref)
        values(j, pa_ref, ala_ref)
        scores(j + 3, sb_ref)
        softmax(j + 2, sa_ref, pa_ref, ala_ref)
        values(j + 1, pb_ref, alb_ref)
        return carry

    lax.fori_loop(0, (i + 2) // 2, pair, 0)

    lam = _lambda(lam_ref[0:1, :], lam_ref[1:2, :], lam_ref[2:3, :], lam_ref[3:4, :], lambda_init)
    o = acc_ref[0] / l_ref[0] - lam * (acc_ref[1] / l_ref[1])
    on = o * lax.rsqrt(jnp.mean(o * o, axis=0, keepdims=True) + EPS) * (1.0 - lambda_init)
    o_ref[...] = (on.T * gain_ref[...]).astype(BF16)


def flash_diff_attention(q, k, vt, table, lam_params, gain, lambda_init):
    t, d = q.shape
    tq, tk = min(CFG["fa_tq"], t), min(CFG["fa_tk"], t)
    assert tq == tk and tk % ATT_CHUNK == 0 and vt.shape == (t // tk, d, tk)
    hw = 2 * DA_DH
    nh = d // hw
    far = _rel_bucket_np(-np.arange(tk + 1, max(t, tk + 2)))
    far_bucket = int(far[0])
    assert (far == far_bucket).all()
    near = _bucket_tile(tk + np.arange(tq), np.arange(2 * tk)).T
    bucket = np.concatenate([np.full((tk, tq), far_bucket, np.int32), near,
                             np.full((tk, tq), -1, np.int32)], axis=0)
    bank = bias_tiles(table, bucket, bands=4).reshape(nh, 4, tk, tq)
    once = pl.Buffered(1)
    return pl.pallas_call(
        functools.partial(_flash_kernel, tq=tq, tk=tk, lambda_init=lambda_init),
        grid=(nh, t // tq),
        in_specs=[pl.BlockSpec((tq, hw), lambda h, i: (i, h)),
                  pl.BlockSpec((t, hw), lambda h, i: (0, h), pipeline_mode=once),
                  pl.BlockSpec((t // tk, hw, tk), lambda h, i: (0, h, 0), pipeline_mode=once),
                  pl.BlockSpec((1, 4, tk, tq), lambda h, i: (h, 0, 0, 0), pipeline_mode=once),
                  pl.BlockSpec((4, DA_DH), lambda h, i: (0, 0)),
                  pl.BlockSpec((1, hw), lambda h, i: (0, 0))],
        out_specs=pl.BlockSpec((tq, hw), lambda h, i: (i, h)),
        out_shape=jax.ShapeDtypeStruct((t, d), BF16),
        scratch_shapes=[pltpu.VMEM((2, 1, tq), F32), pltpu.VMEM((2, 1, tq), F32),
                        pltpu.VMEM((2, hw, tq), F32),
                        pltpu.VMEM((2, tk, tq), F32), pltpu.VMEM((2, tk, tq), F32),
                        pltpu.VMEM((2, tk, tq), BF16), pltpu.VMEM((2, tk, tq), BF16),
                        pltpu.VMEM((2, 1, tq), F32), pltpu.VMEM((2, 1, tq), F32)],
        compiler_params=_cp(("parallel", "arbitrary")),
        name="flash_diff_attention",
    )(q, k, vt, bank, lam_params, gain.reshape(1, hw))


def _sample_attn_kernel(tab_ref, q_ref, ck_ref, cvl_ref, cvh_ref, kn_ref, vn_ref, blast_ref, bnew_ref, lam_ref,
                        gain_ref, o_ref, m_ref, l_ref, acc_ref, *, tk, far_bucket, lambda_init):
    j = pl.program_id(1)
    nj = pl.num_programs(1)
    dh = DA_DH
    hw = 2 * dh
    nh = q_ref.shape[1] // hw
    ts = q_ref.shape[0]

    @pl.when(j == 0)
    def _():
        m_ref[...] = jnp.full(m_ref.shape, -jnp.inf, F32)
        l_ref[...] = jnp.zeros(l_ref.shape, F32)
        acc_ref[...] = jnp.zeros(acc_ref.shape, F32)

    q = q_ref[...]

    def block(k_of, v_of, bias_fn):
        for h in range(nh):
            vb = v_of(h)
            for c in range(2):
                cs = slice((2 * h + c) * dh, (2 * h + c + 1) * dh)
                _softmax_step(q[:, cs], k_of(2 * h + c), vb, bias_fn(h), m_ref, l_ref, acc_ref, 2 * h + c)

    def cache_k_of(hc):
        return ck_ref[0, pl.ds(hc, tk, stride=2 * nh), :].astype(BF16)

    def cache_v_of(h):
        return jnp.concatenate([cvl_ref[0, pl.ds(h, tk, stride=nh), :],
                                cvh_ref[0, pl.ds(h, tk, stride=nh), :]], axis=1).astype(BF16)

    @pl.when(j < nj - 1)
    def _():
        block(cache_k_of, cache_v_of, lambda h: tab_ref[far_bucket, h] * LOG2E)

    @pl.when(j == nj - 1)
    def _():
        block(cache_k_of, cache_v_of, lambda h: blast_ref[h])
        block(lambda hc: kn_ref[:, hc * dh:(hc + 1) * dh], lambda h: vn_ref[:, h * hw:(h + 1) * hw],
              lambda h: bnew_ref[h][:, :ts])
        lam = _lambda(lam_ref[0:1, :], lam_ref[1:2, :], lam_ref[2:3, :], lam_ref[3:4, :], lambda_init)
        for h in range(nh):
            o = acc_ref[2 * h] / l_ref[2 * h] - lam * (acc_ref[2 * h + 1] / l_ref[2 * h + 1])
            o_ref[:, h * hw:(h + 1) * hw] = (_rms(o, gain_ref[...]) * (1.0 - lambda_init)).astype(BF16)


def sample_diff_attention(q, k_new, v_new, cache_k, cache_v, layer, table, lam_params, gain, lambda_init):
    n_layers, batch, past, nhc, dh = cache_k.shape
    d = nhc * dh
    ts = q.shape[0] // batch
    hw = 2 * DA_DH
    nh = d // hw
    tk = min(CFG["sa_tk"], past)
    nkb = past // tk
    q_pos = past + np.arange(ts)
    blast = bias_tiles(table, _bucket_tile(q_pos, past - tk + np.arange(tk)))
    bnew = bias_tiles(table, _bucket_tile(q_pos, past + np.arange(128)))
    if nkb > 1:
        far = _bucket_tile(q_pos, np.arange(past - tk))
        far_bucket = int(far[0, 0])
        assert (far == far_bucket).all()
    else:
        far_bucket = 0
    cache_v_rows = cache_v.reshape(n_layers, batch, past * nh, hw)
    return pl.pallas_call(
        functools.partial(_sample_attn_kernel, tk=tk, far_bucket=far_bucket, lambda_init=lambda_init),
        grid=(batch, nkb),
        in_specs=[pl.BlockSpec(memory_space=pltpu.SMEM),
                  pl.BlockSpec((ts, d), lambda b, j: (b, 0)),
                  pl.BlockSpec((None, 1, tk * nhc, dh), lambda b, j: (layer, b, j, 0)),
                  pl.BlockSpec((None, 1, tk * nh, dh), lambda b, j: (layer, b, j, 0)),
                  pl.BlockSpec((None, 1, tk * nh, dh), lambda b, j: (layer, b, j, 1)),
                  pl.BlockSpec((ts, d), lambda b, j: (b, 0)),
                  pl.BlockSpec((ts, d), lambda b, j: (b, 0)),
                  pl.BlockSpec((nh, ts, tk), lambda b, j: (0, 0, 0)),
                  pl.BlockSpec((nh, ts, 128), lambda b, j: (0, 0, 0)),
                  pl.BlockSpec((4, DA_DH), lambda b, j: (0, 0)),
                  pl.BlockSpec((1, hw), lambda b, j: (0, 0))],
        out_specs=pl.BlockSpec((ts, d), lambda b, j: (b, 0)),
        out_shape=jax.ShapeDtypeStruct((batch * ts, d), BF16),
        scratch_shapes=[pltpu.VMEM((2 * nh, ts, 1), F32), pltpu.VMEM((2 * nh, ts, 1), F32),
                        pltpu.VMEM((2 * nh, ts, hw), F32)],
        compiler_params=_cp(("parallel", "arbitrary")),
        name="sample_diff_attention",
    )(table, q, cache_k.reshape(n_layers, batch, past * nhc, dh), cache_v_rows, cache_v_rows,
      k_new, v_new, blast, bnew, lam_params, gain.reshape(1, hw))


def kernel(x_prompt, x_sample, state_hgrn, state_pool, cache_k, cache_v, norm_mix, norm_ffn, norm_final, hgrn_w_q, hgrn_w_f, hgrn_w_i, hgrn_w_g, hgrn_w_o, hgrn_lb_logits, hgrn_norm_gain, pool_w, pool_scale, attn_w_q, attn_w_k, attn_w_v, attn_w_o, attn_lambda_q1, attn_lambda_k1, attn_lambda_q2, attn_lambda_k2, attn_subln_gain, rel_bias_table, ffn_w_gate, ffn_w_up, ffn_w_down):
    bp, tp, d = x_prompt.shape
    bs, ts, _ = x_sample.shape
    past = cache_k.shape[2]
    depth = norm_mix.shape[0]
    assert bp == 1
    bf = lambda a: a.astype(BF16)
    xs = [x_prompt.reshape(bp * tp, d), x_sample.reshape(bs * ts, d)]
    dims = [(bp, tp), (bs, ts)]
    hg, pool_st, k_out, v_out = [[], []], [[], []], [[], []], [[], []]
    w_g, w_u, w_d = bf(ffn_w_gate), bf(ffn_w_up), bf(ffn_w_down)
    zero_state = jnp.zeros((1, bp, HG_HEADS, HG_DK, HG_DK), F32)

    for i in range(depth):
        m, j = i % N_MIXERS, i // N_MIXERS
        if m == 0:
            w_cat = bf(jnp.concatenate([hgrn_w_q[j], hgrn_w_f[j], hgrn_w_i[j], hgrn_w_g[j]], axis=1))
            w_o = bf(hgrn_w_o[j])
            s0s = [(zero_state, 0), (state_hgrn, j)]
            for r in range(2):
                b, t = dims[r]
                proj = norm_mm(xs[r], norm_mix[i], w_cat)
                og, st = hgrn_recurrence(proj, hgrn_lb_logits, hgrn_norm_gain[j], s0s[r], b, t, j)
                xs[r] = mm_res(og, w_o, xs[r])
                hg[r].append(st)
        elif m == 1:
            w_p = bf(pool_w[j])
            hists = [jnp.zeros((bp, POOL_HIST, d), F32), state_pool[j]]
            for r in range(2):
                b, t = dims[r]
                xs[r], st = pool_mixer(xs[r], norm_mix[i], hists[r], w_p, pool_scale[j], b, t, (0, past)[r])
                pool_st[r].append(st[:, 1:, :])
        else:
            lambda_init = 0.8 - 0.6 * math.exp(-0.3 * i)
            w_qkv = bf(jnp.concatenate([attn_w_q[j], attn_w_k[j], attn_w_v[j]], axis=1))
            w_o = bf(attn_w_o[j])
            lam_params = jnp.stack([attn_lambda_q1[j], attn_lambda_k1[j], attn_lambda_q2[j], attn_lambda_k2[j]])
            for r in range(2):
                b, t = dims[r]
                q, k32, k16, v32, v16 = qkv_proj(xs[r], norm_mix[i], w_qkv, DA_DH ** -0.5 * LOG2E, r == 0,
                                                 CFG["fa_tk"])
                if r == 0:
                    o = flash_diff_attention(q, k16, v16, rel_bias_table, lam_params, attn_subln_gain[j],
                                             lambda_init)
                else:
                    o = sample_diff_attention(q, k16, v16, cache_k, cache_v, j, rel_bias_table, lam_params,
                                              attn_subln_gain[j], lambda_init)
                xs[r] = mm_res(o, w_o, xs[r])
                k_out[r].append(k32.reshape(b, t, 2 * DA_HEADS, DA_DH))
                v_out[r].append(v32.reshape(b, t, DA_HEADS, 2 * DA_DH))
        g_final = norm_final if i == depth - 1 else None
        for r in range(2):
            xs[r] = ffn(xs[r], norm_ffn[i], w_g, w_u, w_d, i, g_final)

    return (xs[0].reshape(bp, tp, d), xs[1].reshape(bs, ts, d),
            jnp.stack(hg[0]), jnp.stack(hg[1]), jnp.stack(pool_st[0]), jnp.stack(pool_st[1]),
            jnp.stack(k_out[0]), jnp.stack(v_out[0]), jnp.stack(k_out[1]), jnp.stack(v_out[1]))
```

```python
import functools
import math

import numpy as np
import jax
import jax.numpy as jnp
from jax import lax
from jax.experimental import pallas as pl
from jax.experimental.pallas import tpu as pltpu

F32 = jnp.float32
BF16 = jnp.bfloat16

EPS = 1e-6
LOG2E = math.log2(math.e)
F_MIN = 1e-6
MASK_VALUE = -1e30
HG_HEADS = 16
HG_DK = 128
POOL_WINDOWS = (2, 4, 8, 16)
POOL_HIST = 15
DA_HEADS = 8
DA_DH = 128
ATT_CHUNK = 64
REL_BUCKETS = 32
REL_MAX_DIST = 128
N_MIXERS = 3

CFG = dict(
    mm_tm=1024, mm_tn=1024,
    qkv_tn=512,
    ffn_tm=1024, ffn_tf=256,
    pool_tm=512,
    hg_chunk=256, hg_rows=512, hg_heads=8, hg_unroll=2,
    fa_tq=512, fa_tk=256,
    sa_tk=1024,
    vmem=56 * 1024 * 1024,
)


def _cp(sem):
    return pltpu.CompilerParams(dimension_semantics=sem, vmem_limit_bytes=CFG["vmem"])


def _rms(x, g):
    return x * lax.rsqrt(jnp.mean(x * x, axis=-1, keepdims=True) + EPS) * g


def _dot(a, b):
    return jnp.dot(a, b, preferred_element_type=F32)


def _dot_nt(a, b):
    return lax.dot_general(a, b, (((1,), (1,)), ((), ())), preferred_element_type=F32)


def _norm_mm_kernel(x_ref, g_ref, w_ref, o_ref, xn_ref):
    @pl.when(pl.program_id(1) == 0)
    def _():
        xn_ref[...] = _rms(x_ref[...], g_ref[...]).astype(BF16)

    o_ref[...] = _dot(xn_ref[...], w_ref[...])


def norm_mm(x, g, w):
    m, d = x.shape
    n = w.shape[1]
    tm, tn = min(CFG["mm_tm"], m), min(CFG["mm_tn"], n)
    return pl.pallas_call(
        _norm_mm_kernel,
        grid=(m // tm, n // tn),
        in_specs=[pl.BlockSpec((tm, d), lambda i, j: (i, 0)),
                  pl.BlockSpec((1, d), lambda i, j: (0, 0)),
                  pl.BlockSpec((d, tn), lambda i, j: (0, j))],
        out_specs=pl.BlockSpec((tm, tn), lambda i, j: (i, j)),
        out_shape=jax.ShapeDtypeStruct((m, n), F32),
        scratch_shapes=[pltpu.VMEM((tm, d), BF16)],
        compiler_params=_cp(("parallel", "arbitrary")),
        name="norm_mm",
    )(x, g.reshape(1, d), w)


def _qkv_kernel(x_ref, g_ref, w_ref, q_ref, k32_ref, k16_ref, v32_ref, v16_ref, xn_ref, *, nq, q_scale, v_transposed):
    j = pl.program_id(1)

    @pl.when(j == 0)
    def _():
        xn_ref[...] = _rms(x_ref[...], g_ref[...]).astype(BF16)

    y = _dot(xn_ref[...], w_ref[...])

    @pl.when(j < nq)
    def _():
        q_ref[...] = (y * q_scale).astype(BF16)

    @pl.when((j >= nq) & (j < 2 * nq))
    def _():
        k32_ref[...] = y
        k16_ref[...] = y.astype(BF16)

    @pl.when(j >= 2 * nq)
    def _():
        v32_ref[...] = y
        if v_transposed:
            tkv = v16_ref.shape[2]
            for s in range(v16_ref.shape[0]):
                v16_ref[s] = y[s * tkv:(s + 1) * tkv, :].T.astype(BF16)
        else:
            v16_ref[...] = y.astype(BF16)


def qkv_proj(x, g, w, q_scale, v_transposed_rows):
    m, d = x.shape
    n = w.shape[1] // 3
    tm, tn = min(CFG["mm_tm"], m), min(CFG["qkv_tn"], n)
    nq = n // tn
    v_transposed = v_transposed_rows is not None
    spec = lambda off: pl.BlockSpec((tm, tn), lambda i, j: (i, jnp.clip(j - off, 0, nq - 1)))
    if v_transposed:
        tkv = v_transposed_rows
        v16_spec = pl.BlockSpec((tm // tkv, tn, tkv), lambda i, j: (i, jnp.clip(j - 2 * nq, 0, nq - 1), 0))
        v16_shape = jax.ShapeDtypeStruct((m // tkv, n, tkv), BF16)
    else:
        v16_spec, v16_shape = spec(2 * nq), jax.ShapeDtypeStruct((m, n), BF16)
    return pl.pallas_call(
        functools.partial(_qkv_kernel, nq=nq, q_scale=q_scale, v_transposed=v_transposed),
        grid=(m // tm, 3 * nq),
        in_specs=[pl.BlockSpec((tm, d), lambda i, j: (i, 0)),
                  pl.BlockSpec((1, d), lambda i, j: (0, 0)),
                  pl.BlockSpec((d, tn), lambda i, j: (0, j))],
        out_specs=[spec(0), spec(nq), spec(nq), spec(2 * nq), v16_spec],
        out_shape=[jax.ShapeDtypeStruct((m, n), BF16), jax.ShapeDtypeStruct((m, n), F32),
                   jax.ShapeDtypeStruct((m, n), BF16), jax.ShapeDtypeStruct((m, n), F32), v16_shape],
        scratch_shapes=[pltpu.VMEM((tm, d), BF16)],
        compiler_params=_cp(("parallel", "arbitrary")),
        name="qkv_proj",
    )(x, g.reshape(1, d), w)


def _mm_res_kernel(a_ref, w_ref, r_ref, o_ref):
    o_ref[...] = r_ref[...] + _dot(a_ref[...], w_ref[...])


def mm_res(a, w, r):
    m, k = a.shape
    n = w.shape[1]
    tm, tn = min(CFG["mm_tm"], m), min(CFG["mm_tn"], n)
    return pl.pallas_call(
        _mm_res_kernel,
        grid=(m // tm, n // tn),
        in_specs=[pl.BlockSpec((tm, k), lambda i, j: (i, 0)),
                  pl.BlockSpec((k, tn), lambda i, j: (0, j)),
                  pl.BlockSpec((tm, tn), lambda i, j: (i, j))],
        out_specs=pl.BlockSpec((tm, tn), lambda i, j: (i, j)),
        out_shape=jax.ShapeDtypeStruct((m, n), F32),
        compiler_params=_cp(("parallel", "arbitrary")),
        name="mm_res",
    )(a, w, r)


def _ffn_kernel(x_ref, g_ref, wg_ref, wu_ref, wd_ref, gf_ref, o_ref, xn_ref, *, final_norm):
    j = pl.program_id(1)

    @pl.when(j == 0)
    def _():
        x = x_ref[...]
        xn_ref[...] = _rms(x, g_ref[...]).astype(BF16)
        o_ref[...] = x

    xn = xn_ref[...]
    a = _dot(xn, wg_ref[...])
    b = _dot(xn, wu_ref[...])
    h = (a * jax.nn.sigmoid(a) * b).astype(BF16)
    o_ref[...] += _dot(h, wd_ref[...])

    if final_norm:
        @pl.when(j == pl.num_programs(1) - 1)
        def _():
            o_ref[...] = _rms(o_ref[...], gf_ref[...])


def ffn(x, g, wg, wu, wd, layer, g_final=None):
    m, d = x.shape
    f = wg.shape[2]
    tm = min(CFG["ffn_tm"], m)
    tf = min(CFG["ffn_tf"] * min(CFG["ffn_tm"] // tm, 2), f)
    final_norm = g_final is not None
    gf = (g_final if final_norm else g).reshape(1, d)
    return pl.pallas_call(
        functools.partial(_ffn_kernel, final_norm=final_norm),
        grid=(m // tm, f // tf),
        in_specs=[pl.BlockSpec((tm, d), lambda i, j: (i, 0)),
                  pl.BlockSpec((1, d), lambda i, j: (0, 0)),
                  pl.BlockSpec((None, d, tf), lambda i, j: (layer, 0, j)),
                  pl.BlockSpec((None, d, tf), lambda i, j: (layer, 0, j)),
                  pl.BlockSpec((None, tf, d), lambda i, j: (layer, j, 0)),
                  pl.BlockSpec((1, d), lambda i, j: (0, 0))],
        out_specs=pl.BlockSpec((tm, d), lambda i, j: (i, 0)),
        out_shape=jax.ShapeDtypeStruct((m, d), F32),
        scratch_shapes=[pltpu.VMEM((tm, d), BF16)],
        compiler_params=_cp(("parallel", "arbitrary")),
        name="ffn",
    )(x, g.reshape(1, d), wg, wu, wd, gf)


def _hgrn_levels(c):
    lv, h = [], c // 2
    while h >= 8:
        lv.append(h)
        h //= 2
    return lv


def _bcast_rows(g, rows, rep):
    return jnp.concatenate([jnp.broadcast_to(g[r:r + 1, :], (rep, g.shape[1])) for r in rows], axis=0)


def _hgrn_kernel(q_ref, f_ref, v_ref, gt_ref, lbl_ref, gain_ref, s0_ref, l_ref, og_ref, st_ref, stt_ref,
                 *, c, n_sub, hb, layer):
    cb = pl.program_id(2)
    dk = HG_DK

    @pl.when(cb == 0)
    def _():
        for h in range(hb):
            stt_ref[h] = s0_ref[0, h].T

    lg = lbl_ref[...]
    e = jnp.exp(lg - jnp.max(lg, axis=0, keepdims=True))
    p = e / jnp.sum(e, axis=0, keepdims=True)
    lb = p[0:1, :]
    for i in range(1, layer + 1):
        lb = lb + p[i:i + 1, :]
    lb = jnp.maximum(lb - p[0:1, :], 0.0)

    ti = lax.broadcasted_iota(jnp.int32, (c, c), 0)
    si = lax.broadcasted_iota(jnp.int32, (c, c), 1)
    levels = _hgrn_levels(c)
    masks = []
    for hh in levels:
        sh = int(math.log2(2 * hh))
        masks.append(((ti >> sh) == (si >> sh)) & ((ti & (2 * hh - 1)) >= hh) & ((si & (2 * hh - 1)) < hh))
    mask_loc = ((ti >> 3) == (si >> 3)) & (si <= ti)
    lmat = l_ref[...]
    gain = gain_ref[...]

    def chunk(ci, carry):
        r0 = pl.multiple_of(ci * c, c)
        fp = f_ref[pl.ds(r0, c), :]
        f = lb + (1.0 - lb) * jax.nn.sigmoid(fp)
        g = jnp.log2(jnp.maximum(f, F_MIN))
        kk = 1.0 - f
        g_hi = g.astype(BF16)
        r1 = g - g_hi.astype(F32)
        g_mid = r1.astype(BF16)
        g_lo = (r1 - g_mid.astype(F32)).astype(BF16)
        gc = _dot(lmat, g_hi) + _dot(lmat, g_mid) + _dot(lmat, g_lo)
        qa = q_ref[pl.ds(r0, c), :]
        va = v_ref[pl.ds(r0, c), :]
        ga = gt_ref[pl.ds(r0, c), :]
        for h in range(hb):
            sl = slice(h * dk, (h + 1) * dk)
            gh, qh, kh = gc[:, sl], qa[:, sl], kk[:, sl]
            vh = va[:, sl].astype(BF16)
            stt = stt_ref[h]
            inter = _dot_nt((qh * jnp.exp2(gh)).astype(BF16), stt.astype(BF16))
            g_last = gh[c - 1:c, :]
            k_dec = (kh * jnp.exp2(g_last - gh)).astype(BF16)
            stt_ref[h] = stt * jnp.exp2(g_last) + lax.dot_general(
                vh, k_dec, (((0,), (0,)), ((), ())), preferred_element_type=F32)
            q16, k16 = qh.astype(BF16), kh.astype(BF16)
            g_loc = _bcast_rows(gh, [8 * b + 3 for b in range(c // 8)], 8)
            sc = jnp.where(mask_loc,
                           _dot_nt(q16 * jnp.exp2(gh - g_loc).astype(BF16),
                                   k16 * jnp.exp2(g_loc - gh).astype(BF16)), 0.0)
            for hh, mk in zip(levels, masks):
                g_mid_rows = _bcast_rows(gh, [b * 2 * hh + hh - 1 for b in range(c // (2 * hh))], 2 * hh)
                dist = lax.bitcast_convert_type(gh - g_mid_rows, jnp.uint32) | jnp.uint32(0x80000000)
                fac = jnp.exp2(lax.bitcast_convert_type(dist, F32)).astype(BF16)
                sc = jnp.where(mk, _dot_nt(q16 * fac, k16 * fac), sc)
            o = inter + _dot(sc.astype(BF16), vh)
            on = _rms(o, gain)
            gate = ga[:, sl]
            og_ref[pl.ds(r0, c), sl] = (on * (gate * jax.nn.sigmoid(gate))).astype(BF16)
        return carry

    lax.fori_loop(0, n_sub, chunk, 0, unroll=min(CFG["hg_unroll"], n_sub))

    @pl.when(cb == pl.num_programs(2) - 1)
    def _():
        for h in range(hb):
            st_ref[0, h] = stt_ref[h].T


def hgrn_recurrence(proj, lb_logits, gain, s0, batch, t, layer):
    s0, s0_layer = s0
    d = HG_HEADS * HG_DK
    c = min(CFG["hg_chunk"], t)
    rows = min(CFG["hg_rows"], t)
    hb = CFG["hg_heads"] if t > c else HG_HEADS
    w = hb * HG_DK
    ncb = t // rows
    n_layers = lb_logits.shape[0]
    lmat = jnp.asarray(np.tril(np.ones((c, c), np.float32)), BF16)

    def col(sec):
        return lambda b, hg, cb: (b * ncb + cb, sec * (d // w) + hg)

    og, st = pl.pallas_call(
        functools.partial(_hgrn_kernel, c=c, n_sub=rows // c, hb=hb, layer=layer),
        grid=(batch, HG_HEADS // hb, ncb),
        in_specs=[pl.BlockSpec((rows, w), col(0)),
                  pl.BlockSpec((rows, w), col(1)),
                  pl.BlockSpec((rows, w), col(2)),
                  pl.BlockSpec((rows, w), col(3)),
                  pl.BlockSpec((n_layers, w), lambda b, hg, cb: (0, hg)),
                  pl.BlockSpec((1, HG_DK), lambda b, hg, cb: (0, 0)),
                  pl.BlockSpec((None, 1, hb, HG_DK, HG_DK), lambda b, hg, cb: (s0_layer, b, hg, 0, 0)),
                  pl.BlockSpec((c, c), lambda b, hg, cb: (0, 0))],
        out_specs=[pl.BlockSpec((rows, w), lambda b, hg, cb: (b * ncb + cb, hg)),
                   pl.BlockSpec((1, hb, HG_DK, HG_DK), lambda b, hg, cb: (b, hg, 0, 0))],
        out_shape=[jax.ShapeDtypeStruct((batch * t, d), BF16),
                   jax.ShapeDtypeStruct((batch, HG_HEADS, HG_DK, HG_DK), F32)],
        scratch_shapes=[pltpu.VMEM((hb, HG_DK, HG_DK), F32)],
        compiler_params=_cp(("parallel", "parallel", "arbitrary")),
        name="hgrn_recurrence",
    )(proj, proj, proj, proj, lb_logits, gain.reshape(1, HG_DK), s0, lmat)
    return og, st


def _pool_kernel(x_ref, g_ref, hist_ref, w_ref, sc_ref, o_ref, st_ref, buf_ref, *, tm, pos0):
    tb = pl.program_id(1)
    x = x_ref[...]
    xn = _rms(x, g_ref[...])
    hp = POOL_HIST + 1

    @pl.when(tb == 0)
    def _():
        buf_ref[0:hp, :] = hist_ref[0]

    @pl.when(tb > 0)
    def _():
        buf_ref[0:hp, :] = buf_ref[tm:tm + hp, :]

    buf_ref[hp:hp + tm, :] = xn
    st_ref[0] = buf_ref[tm:tm + hp, :]
    pos = pos0 + tb * tm + lax.broadcasted_iota(jnp.int32, (tm, 1), 0)
    gc = x.shape[1] // len(POOL_WINDOWS)
    for gi, win in enumerate(POOL_WINDOWS):
        cs = slice(gi * gc, (gi + 1) * gc)
        ws = xn[:, cs]
        for sft in range(1, win):
            ws = ws + buf_ref[hp - sft:hp - sft + tm, cs]
        cnt = jnp.minimum(win, pos + 1).astype(F32)
        pooled = (ws / cnt - xn[:, cs]).astype(BF16)
        o_ref[:, cs] = x[:, cs] + _dot(pooled, w_ref[gi]) * sc_ref[:, cs]


def pool_mixer(x, g, hist, w, scale, batch, t, pos0):
    d = x.shape[1]
    tm = min(CFG["pool_tm"], t)
    nb = t // tm
    hp = POOL_HIST + 1
    hist16 = jnp.concatenate([jnp.zeros((batch, 1, d), F32), hist], axis=1)
    ng, gc = w.shape[0], w.shape[1]
    return pl.pallas_call(
        functools.partial(_pool_kernel, tm=tm, pos0=pos0),
        grid=(batch, nb),
        in_specs=[pl.BlockSpec((tm, d), lambda b, i: (b * nb + i, 0)),
                  pl.BlockSpec((1, d), lambda b, i: (0, 0)),
                  pl.BlockSpec((1, hp, d), lambda b, i: (b, 0, 0)),
                  pl.BlockSpec((ng, gc, gc), lambda b, i: (0, 0, 0)),
                  pl.BlockSpec((1, d), lambda b, i: (0, 0))],
        out_specs=[pl.BlockSpec((tm, d), lambda b, i: (b * nb + i, 0)),
                   pl.BlockSpec((1, hp, d), lambda b, i: (b, 0, 0))],
        out_shape=[jax.ShapeDtypeStruct(x.shape, F32),
                   jax.ShapeDtypeStruct((batch, hp, d), F32)],
        scratch_shapes=[pltpu.VMEM((tm + hp, d), F32)],
        compiler_params=_cp(("parallel", "arbitrary")),
        name="pool_mixer",
    )(x, g.reshape(1, d), hist16, w, scale.reshape(1, d))


def _rel_bucket_np(rel):
    nb = REL_BUCKETS // 2
    max_exact = nb // 2
    n = np.abs(rel)
    large = max_exact + (np.log(np.maximum(n, max_exact).astype(np.float64) / max_exact)
                         / math.log(REL_MAX_DIST / max_exact) * (nb - max_exact)).astype(np.int64)
    large = np.minimum(large, nb - 1)
    return (np.where(rel > 0, nb, 0) + np.where(n < max_exact, n, large)).astype(np.int32)


def _bucket_tile(q_pos, k_pos):
    q_pos, k_pos = np.asarray(q_pos)[:, None], np.asarray(k_pos)[None, :]
    b = _rel_bucket_np(k_pos - q_pos)
    return np.where((k_pos // ATT_CHUNK) <= (q_pos // ATT_CHUNK), b, -1).astype(np.int32)


def _bias_kernel(tab_ref, bk_ref, o_ref, *, present):
    h = pl.program_id(0)
    rows = bk_ref.shape[0] // len(present)
    for k, values in enumerate(present):
        bk = bk_ref[k * rows:(k + 1) * rows, :]
        bias = jnp.full(bk.shape, MASK_VALUE, F32)
        for b in values:
            bias = jnp.where(bk == b, tab_ref[b, h] * LOG2E, bias)
        o_ref[0, k * rows:(k + 1) * rows, :] = bias


def bias_tiles(table, bucket, bands=1):
    r, c = bucket.shape
    nh = table.shape[1]
    present = tuple(tuple(int(b) for b in np.unique(band) if b >= 0) for band in np.split(bucket, bands, axis=0))
    return pl.pallas_call(
        functools.partial(_bias_kernel, present=present),
        grid=(nh,),
        in_specs=[pl.BlockSpec(memory_space=pltpu.SMEM),
                  pl.BlockSpec((r, c), lambda h: (0, 0))],
        out_specs=pl.BlockSpec((1, r, c), lambda h: (h, 0, 0)),
        out_shape=jax.ShapeDtypeStruct((nh, r, c), F32),
        compiler_params=_cp(("arbitrary",)),
        name="bias_tiles",
    )(table, jnp.asarray(bucket))


def _lambda(lq1, lk1, lq2, lk2, lambda_init):
    return (jnp.exp(jnp.sum(lq1 * lk1, axis=-1, keepdims=True))
            - jnp.exp(jnp.sum(lq2 * lk2, axis=-1, keepdims=True)) + lambda_init)


def _softmax_step(q, k, v, bias, m_ref, l_ref, acc_ref, idx):
    s = _dot_nt(q, k) + bias
    m_prev = m_ref[idx]
    m_new = jnp.maximum(m_prev, jnp.max(s, axis=-1, keepdims=True))
    p = jnp.exp2(s - m_new)
    alpha = jnp.exp2(m_prev - m_new)
    l_ref[idx] = alpha * l_ref[idx] + jnp.sum(p, axis=-1, keepdims=True)
    acc_ref[idx] = alpha * acc_ref[idx] + _dot(p.astype(BF16), v)
    m_ref[idx] = m_new


def _flash_kernel(q_ref, k_ref, vt_ref, bias_ref, lam_ref, gain_ref, o_ref,
                  m_ref, l_ref, acc_ref, sa_ref, sb_ref, pa_ref, pb_ref, ala_ref, alb_ref, *, tq, tk, lambda_init):
    i = pl.program_id(1)
    dh = DA_DH
    m_ref[...] = jnp.full(m_ref.shape, -jnp.inf, F32)
    l_ref[...] = jnp.zeros(l_ref.shape, F32)
    acc_ref[...] = jnp.zeros(acc_ref.shape, F32)
    q = q_ref[...]

    nkb = vt_ref.shape[0]

    def scores(j, s_ref):
        jc = jnp.minimum(j, nkb - 1)
        kb = k_ref[pl.ds(pl.multiple_of(jc * tk, tk), tk), :]
        for c in range(2):
            s_ref[c] = _dot_nt(kb[:, c * dh:(c + 1) * dh], q[:, c * dh:(c + 1) * dh])

    r = bias_ref.shape[1] - 3

    def softmax(j, s_ref, p_ref, al_ref):
        bias = bias_ref[0, jnp.clip(j - (r * i - 2), 0, r + 2)]
        for c in range(2):
            s = s_ref[c] + bias
            m_prev = m_ref[c]
            m_new = jnp.maximum(m_prev, jnp.max(s, axis=0, keepdims=True))
            p = jnp.exp2(s - m_new)
            alpha = jnp.exp2(m_prev - m_new)
            l_ref[c] = alpha * l_ref[c] + jnp.sum(p, axis=0, keepdims=True)
            m_ref[c] = m_new
            al_ref[c] = alpha
            p_ref[c] = p.astype(BF16)

    def values(j, p_ref, al_ref):
        vt = vt_ref[jnp.minimum(j, nkb - 1)]
        for c in range(2):
            acc_ref[c] = al_ref[c] * acc_ref[c] + _dot(vt, p_ref[c])

    scores(0, sa_ref)
    scores(1, sb_ref)
    softmax(0, sa_ref, pa_ref, ala_ref)

    def pair(t, carry):
        j = 2 * t
        scores(j + 2, sa_ref)
        softmax(j + 1, sb_ref, pb_ref, alb_ref)
        values(j, pa_ref, ala_ref)
        scores(j + 3, sb_ref)
        softmax(j + 2, sa_ref, pa_ref, ala_ref)
        values(j + 1, pb_ref, alb_ref)
        return carry

    lax.fori_loop(0, (r * i + r + 1) // 2, pair, 0)

    lam = _lambda(lam_ref[0:1, :], lam_ref[1:2, :], lam_ref[2:3, :], lam_ref[3:4, :], lambda_init)
    o = acc_ref[0] / l_ref[0] - lam * (acc_ref[1] / l_ref[1])
    on = o * lax.rsqrt(jnp.mean(o * o, axis=0, keepdims=True) + EPS) * (1.0 - lambda_init)
    o_ref[...] = (on.T * gain_ref[...]).astype(BF16)


def flash_diff_attention(q, k, vt, table, lam_params, gain, lambda_init):
    t, d = q.shape
    tq, tk = min(CFG["fa_tq"], t), min(CFG["fa_tk"], t)
    assert tq % tk == 0 and tk % ATT_CHUNK == 0 and vt.shape == (t // tk, d, tk)
    r = tq // tk
    hw = 2 * DA_DH
    nh = d // hw
    far = _rel_bucket_np(-np.arange(tk + 1, max(t, tk + 2)))
    far_bucket = int(far[0])
    assert (far == far_bucket).all()
    near = _bucket_tile(tk + np.arange(tq), np.arange((r + 1) * tk)).T
    bucket = np.concatenate([np.full((tk, tq), far_bucket, np.int32), near,
                             np.full((tk, tq), -1, np.int32)], axis=0)
    nb = r + 3
    bank = bias_tiles(table, bucket, bands=nb).reshape(nh, nb, tk, tq)
    once = pl.Buffered(1)
    return pl.pallas_call(
        functools.partial(_flash_kernel, tq=tq, tk=tk, lambda_init=lambda_init),
        grid=(nh, t // tq),
        in_specs=[pl.BlockSpec((tq, hw), lambda h, i: (i, h)),
                  pl.BlockSpec((t, hw), lambda h, i: (0, h), pipeline_mode=once),
                  pl.BlockSpec((t // tk, hw, tk), lambda h, i: (0, h, 0), pipeline_mode=once),
                  pl.BlockSpec((1, nb, tk, tq), lambda h, i: (h, 0, 0, 0), pipeline_mode=once),
                  pl.BlockSpec((4, DA_DH), lambda h, i: (0, 0)),
                  pl.BlockSpec((1, hw), lambda h, i: (0, 0))],
        out_specs=pl.BlockSpec((tq, hw), lambda h, i: (i, h)),
        out_shape=jax.ShapeDtypeStruct((t, d), BF16),
        scratch_shapes=[pltpu.VMEM((2, 1, tq), F32), pltpu.VMEM((2, 1, tq), F32),
                        pltpu.VMEM((2, hw, tq), F32),
                        pltpu.VMEM((2, tk, tq), F32), pltpu.VMEM((2, tk, tq), F32),
                        pltpu.VMEM((2, tk, tq), BF16), pltpu.VMEM((2, tk, tq), BF16),
                        pltpu.VMEM((2, 1, tq), F32), pltpu.VMEM((2, 1, tq), F32)],
        compiler_params=_cp(("parallel", "arbitrary")),
        name="flash_diff_attention",
    )(q, k, vt, bank, lam_params, gain.reshape(1, hw))


def _sample_attn_kernel(tab_ref, q_ref, ck_ref, cvl_ref, cvh_ref, kn_ref, vn_ref, blast_ref, bnew_ref, lam_ref,
                        gain_ref, o_ref, m_ref, l_ref, acc_ref, *, tk, far_bucket, lambda_init):
    j = pl.program_id(1)
    nj = pl.num_programs(1)
    dh = DA_DH
    hw = 2 * dh
    nh = q_ref.shape[1] // hw
    ts = q_ref.shape[0]

    @pl.when(j == 0)
    def _():
        m_ref[...] = jnp.full(m_ref.shape, -jnp.inf, F32)
        l_ref[...] = jnp.zeros(l_ref.shape, F32)
        acc_ref[...] = jnp.zeros(acc_ref.shape, F32)

    q = q_ref[...]

    def block(k_of, v_of, bias_fn):
        for h in range(nh):
            vb = v_of(h)
            for c in range(2):
                cs = slice((2 * h + c) * dh, (2 * h + c + 1) * dh)
                _softmax_step(q[:, cs], k_of(2 * h + c), vb, bias_fn(h), m_ref, l_ref, acc_ref, 2 * h + c)

    def cache_k_of(hc):
        return ck_ref[0, pl.ds(hc, tk, stride=2 * nh), :].astype(BF16)

    def cache_v_of(h):
        return jnp.concatenate([cvl_ref[0, pl.ds(h, tk, stride=nh), :],
                                cvh_ref[0, pl.ds(h, tk, stride=nh), :]], axis=1).astype(BF16)

    @pl.when(j < nj - 1)
    def _():
        block(cache_k_of, cache_v_of, lambda h: tab_ref[far_bucket, h] * LOG2E)

    @pl.when(j == nj - 1)
    def _():
        block(cache_k_of, cache_v_of, lambda h: blast_ref[h])
        block(lambda hc: kn_ref[:, hc * dh:(hc + 1) * dh], lambda h: vn_ref[:, h * hw:(h + 1) * hw],
              lambda h: bnew_ref[h][:, :ts])
        lam = _lambda(lam_ref[0:1, :], lam_ref[1:2, :], lam_ref[2:3, :], lam_ref[3:4, :], lambda_init)
        for h in range(nh):
            o = acc_ref[2 * h] / l_ref[2 * h] - lam * (acc_ref[2 * h + 1] / l_ref[2 * h + 1])
            o_ref[:, h * hw:(h + 1) * hw] = (_rms(o, gain_ref[...]) * (1.0 - lambda_init)).astype(BF16)


def sample_diff_attention(q, k_new, v_new, cache_k, cache_v, layer, table, lam_params, gain, lambda_init):
    n_layers, batch, past, nhc, dh = cache_k.shape
    d = nhc * dh
    ts = q.shape[0] // batch
    hw = 2 * DA_DH
    nh = d // hw
    tk = min(CFG["sa_tk"], past)
    nkb = past // tk
    q_pos = past + np.arange(ts)
    blast = bias_tiles(table, _bucket_tile(q_pos, past - tk + np.arange(tk)))
    bnew = bias_tiles(table, _bucket_tile(q_pos, past + np.arange(128)))
    if nkb > 1:
        far = _bucket_tile(q_pos, np.arange(past - tk))
        far_bucket = int(far[0, 0])
        assert (far == far_bucket).all()
    else:
        far_bucket = 0
    cache_v_rows = cache_v.reshape(n_layers, batch, past * nh, hw)
    return pl.pallas_call(
        functools.partial(_sample_attn_kernel, tk=tk, far_bucket=far_bucket, lambda_init=lambda_init),
        grid=(batch, nkb),
        in_specs=[pl.BlockSpec(memory_space=pltpu.SMEM),
                  pl.BlockSpec((ts, d), lambda b, j: (b, 0)),
                  pl.BlockSpec((None, 1, tk * nhc, dh), lambda b, j: (layer, b, j, 0)),
                  pl.BlockSpec((None, 1, tk * nh, dh), lambda b, j: (layer, b, j, 0)),
                  pl.BlockSpec((None, 1, tk * nh, dh), lambda b, j: (layer, b, j, 1)),
                  pl.BlockSpec((ts, d), lambda b, j: (b, 0)),
                  pl.BlockSpec((ts, d), lambda b, j: (b, 0)),
                  pl.BlockSpec((nh, ts, tk), lambda b, j: (0, 0, 0)),
                  pl.BlockSpec((nh, ts, 128), lambda b, j: (0, 0, 0)),
                  pl.BlockSpec((4, DA_DH), lambda b, j: (0, 0)),
                  pl.BlockSpec((1, hw), lambda b, j: (0, 0))],
        out_specs=pl.BlockSpec((ts, d), lambda b, j: (b, 0)),
        out_shape=jax.ShapeDtypeStruct((batch * ts, d), BF16),
        scratch_shapes=[pltpu.VMEM((2 * nh, ts, 1), F32), pltpu.VMEM((2 * nh, ts, 1), F32),
                        pltpu.VMEM((2 * nh, ts, hw), F32)],
        compiler_params=_cp(("parallel", "arbitrary")),
        name="sample_diff_attention",
    )(table, q, cache_k.reshape(n_layers, batch, past * nhc, dh), cache_v_rows, cache_v_rows,
      k_new, v_new, blast, bnew, lam_params, gain.reshape(1, hw))


def kernel(x_prompt, x_sample, state_hgrn, state_pool, cache_k, cache_v, norm_mix, norm_ffn, norm_final, hgrn_w_q, hgrn_w_f, hgrn_w_i, hgrn_w_g, hgrn_w_o, hgrn_lb_logits, hgrn_norm_gain, pool_w, pool_scale, attn_w_q, attn_w_k, attn_w_v, attn_w_o, attn_lambda_q1, attn_lambda_k1, attn_lambda_q2, attn_lambda_k2, attn_subln_gain, rel_bias_table, ffn_w_gate, ffn_w_up, ffn_w_down):
    bp, tp, d = x_prompt.shape
    bs, ts, _ = x_sample.shape
    past = cache_k.shape[2]
    depth = norm_mix.shape[0]
    assert bp == 1
    bf = lambda a: a.astype(BF16)
    xs = [x_prompt.reshape(bp * tp, d), x_sample.reshape(bs * ts, d)]
    dims = [(bp, tp), (bs, ts)]
    hg, pool_st, k_out, v_out = [[], []], [[], []], [[], []], [[], []]
    w_g, w_u, w_d = bf(ffn_w_gate), bf(ffn_w_up), bf(ffn_w_down)
    zero_state = jnp.zeros((1, bp, HG_HEADS, HG_DK, HG_DK), F32)

    for i in range(depth):
        m, j = i % N_MIXERS, i // N_MIXERS
        if m == 0:
            w_cat = bf(jnp.concatenate([hgrn_w_q[j], hgrn_w_f[j], hgrn_w_i[j], hgrn_w_g[j]], axis=1))
            w_o = bf(hgrn_w_o[j])
            s0s = [(zero_state, 0), (state_hgrn, j)]
            for r in range(2):
                b, t = dims[r]
                proj = norm_mm(xs[r], norm_mix[i], w_cat)
                og, st = hgrn_recurrence(proj, hgrn_lb_logits, hgrn_norm_gain[j], s0s[r], b, t, j)
                xs[r] = mm_res(og, w_o, xs[r])
                hg[r].append(st)
        elif m == 1:
            w_p = bf(pool_w[j])
            hists = [jnp.zeros((bp, POOL_HIST, d), F32), state_pool[j]]
            for r in range(2):
                b, t = dims[r]
                xs[r], st = pool_mixer(xs[r], norm_mix[i], hists[r], w_p, pool_scale[j], b, t, (0, past)[r])
                pool_st[r].append(st[:, 1:, :])
        else:
            lambda_init = 0.8 - 0.6 * math.exp(-0.3 * i)
            w_qkv = bf(jnp.concatenate([attn_w_q[j], attn_w_k[j], attn_w_v[j]], axis=1))
            w_o = bf(attn_w_o[j])
            lam_params = jnp.stack([attn_lambda_q1[j], attn_lambda_k1[j], attn_lambda_q2[j], attn_lambda_k2[j]])
            for r in range(2):
                b, t = dims[r]
                q, k32, k16, v32, v16 = qkv_proj(xs[r], norm_mix[i], w_qkv, DA_DH ** -0.5 * LOG2E,
                                                 min(CFG["fa_tk"], t) if r == 0 else None)
                if r == 0:
                    o = flash_diff_attention(q, k16, v16, rel_bias_table, lam_params, attn_subln_gain[j],
                                             lambda_init)
                else:
                    o = sample_diff_attention(q, k16, v16, cache_k, cache_v, j, rel_bias_table, lam_params,
                                              attn_subln_gain[j], lambda_init)
                xs[r] = mm_res(o, w_o, xs[r])
                k_out[r].append(k32.reshape(b, t, 2 * DA_HEADS, DA_DH))
                v_out[r].append(v32.reshape(b, t, DA_HEADS, 2 * DA_DH))
        g_final = norm_final if i == depth - 1 else None
        for r in range(2):
            xs[r] = ffn(xs[r], norm_ffn[i], w_g, w_u, w_d, i, g_final)

    return (xs[0].reshape(bp, tp, d), xs[1].reshape(bs, ts, d),
            jnp.stack(hg[0]), jnp.stack(hg[1]), jnp.stack(pool_st[0]), jnp.stack(pool_st[1]),
            jnp.stack(k_out[0]), jnp.stack(v_out[0]), jnp.stack(k_out[1]), jnp.stack(v_out[1]))
```

```python
import functools
import math

import numpy as np
import jax
import jax.numpy as jnp
from jax import lax
from jax.experimental import pallas as pl
from jax.experimental.pallas import tpu as pltpu

F32 = jnp.float32
BF16 = jnp.bfloat16

EPS = 1e-6
LOG2E = math.log2(math.e)
F_MIN = 1e-6
MASK_VALUE = -1e30
HG_HEADS = 16
HG_DK = 128
POOL_WINDOWS = (2, 4, 8, 16)
POOL_HIST = 15
DA_HEADS = 8
DA_DH = 128
ATT_CHUNK = 64
REL_BUCKETS = 32
REL_MAX_DIST = 128
N_MIXERS = 3

CFG = dict(
    mm_tm=1024, mm_tn=1024,
    qkv_tn=512,
    ffn_tm=1024, ffn_tf=256,
    pool_tm=512,
    hg_chunk=256, hg_rows=512, hg_heads=8, hg_unroll=2,
    fa_tq=1024, fa_tk=512,
    sa_tk=1024,
    vmem=56 * 1024 * 1024,
)


def _cp(sem):
    return pltpu.CompilerParams(dimension_semantics=sem, vmem_limit_bytes=CFG["vmem"])


def _rms(x, g):
    return x * lax.rsqrt(jnp.mean(x * x, axis=-1, keepdims=True) + EPS) * g


def _dot(a, b):
    return jnp.dot(a, b, preferred_element_type=F32)


def _dot_nt(a, b):
    return lax.dot_general(a, b, (((1,), (1,)), ((), ())), preferred_element_type=F32)


def _norm_mm_kernel(x_ref, g_ref, w_ref, o_ref, xn_ref):
    @pl.when(pl.program_id(1) == 0)
    def _():
        xn_ref[...] = _rms(x_ref[...], g_ref[...]).astype(BF16)

    o_ref[...] = _dot(xn_ref[...], w_ref[...])


def norm_mm(x, g, w):
    m, d = x.shape
    n = w.shape[1]
    tm, tn = min(CFG["mm_tm"], m), min(CFG["mm_tn"], n)
    return pl.pallas_call(
        _norm_mm_kernel,
        grid=(m // tm, n // tn),
        in_specs=[pl.BlockSpec((tm, d), lambda i, j: (i, 0)),
                  pl.BlockSpec((1, d), lambda i, j: (0, 0)),
                  pl.BlockSpec((d, tn), lambda i, j: (0, j))],
        out_specs=pl.BlockSpec((tm, tn), lambda i, j: (i, j)),
        out_shape=jax.ShapeDtypeStruct((m, n), F32),
        scratch_shapes=[pltpu.VMEM((tm, d), BF16)],
        compiler_params=_cp(("parallel", "arbitrary")),
        name="norm_mm",
    )(x, g.reshape(1, d), w)


def _qkv_kernel(x_ref, g_ref, w_ref, q_ref, k32_ref, k16_ref, v32_ref, v16_ref, xn_ref, *, nq, q_scale, v_transposed):
    j = pl.program_id(1)

    @pl.when(j == 0)
    def _():
        xn_ref[...] = _rms(x_ref[...], g_ref[...]).astype(BF16)

    y = _dot(xn_ref[...], w_ref[...])

    @pl.when(j < nq)
    def _():
        q_ref[...] = (y * q_scale).astype(BF16)

    @pl.when((j >= nq) & (j < 2 * nq))
    def _():
        k32_ref[...] = y
        k16_ref[...] = y.astype(BF16)

    @pl.when(j >= 2 * nq)
    def _():
        v32_ref[...] = y
        if v_transposed:
            tkv = v16_ref.shape[2]
            for s in range(v16_ref.shape[0]):
                v16_ref[s] = y[s * tkv:(s + 1) * tkv, :].T.astype(BF16)
        else:
            v16_ref[...] = y.astype(BF16)


def qkv_proj(x, g, w, q_scale, v_transposed_rows):
    m, d = x.shape
    n = w.shape[1] // 3
    tm, tn = min(CFG["mm_tm"], m), min(CFG["qkv_tn"], n)
    nq = n // tn
    v_transposed = v_transposed_rows is not None
    spec = lambda off: pl.BlockSpec((tm, tn), lambda i, j: (i, jnp.clip(j - off, 0, nq - 1)))
    if v_transposed:
        tkv = v_transposed_rows
        v16_spec = pl.BlockSpec((tm // tkv, tn, tkv), lambda i, j: (i, jnp.clip(j - 2 * nq, 0, nq - 1), 0))
        v16_shape = jax.ShapeDtypeStruct((m // tkv, n, tkv), BF16)
    else:
        v16_spec, v16_shape = spec(2 * nq), jax.ShapeDtypeStruct((m, n), BF16)
    return pl.pallas_call(
        functools.partial(_qkv_kernel, nq=nq, q_scale=q_scale, v_transposed=v_transposed),
        grid=(m // tm, 3 * nq),
        in_specs=[pl.BlockSpec((tm, d), lambda i, j: (i, 0)),
                  pl.BlockSpec((1, d), lambda i, j: (0, 0)),
                  pl.BlockSpec((d, tn), lambda i, j: (0, j))],
        out_specs=[spec(0), spec(nq), spec(nq), spec(2 * nq), v16_spec],
        out_shape=[jax.ShapeDtypeStruct((m, n), BF16), jax.ShapeDtypeStruct((m, n), F32),
                   jax.ShapeDtypeStruct((m, n), BF16), jax.ShapeDtypeStruct((m, n), F32), v16_shape],
        scratch_shapes=[pltpu.VMEM((tm, d), BF16)],
        compiler_params=_cp(("parallel", "arbitrary")),
        name="qkv_proj",
    )(x, g.reshape(1, d), w)


def _mm_res_kernel(a_ref, w_ref, r_ref, o_ref):
    o_ref[...] = r_ref[...] + _dot(a_ref[...], w_ref[...])


def mm_res(a, w, r):
    m, k = a.shape
    n = w.shape[1]
    tm, tn = min(CFG["mm_tm"], m), min(CFG["mm_tn"], n)
    return pl.pallas_call(
        _mm_res_kernel,
        grid=(m // tm, n // tn),
        in_specs=[pl.BlockSpec((tm, k), lambda i, j: (i, 0)),
                  pl.BlockSpec((k, tn), lambda i, j: (0, j)),
                  pl.BlockSpec((tm, tn), lambda i, j: (i, j))],
        out_specs=pl.BlockSpec((tm, tn), lambda i, j: (i, j)),
        out_shape=jax.ShapeDtypeStruct((m, n), F32),
        compiler_params=_cp(("parallel", "arbitrary")),
        name="mm_res",
    )(a, w, r)


def _ffn_kernel(x_ref, g_ref, wg_ref, wu_ref, wd_ref, gf_ref, o_ref, xn_ref, *, final_norm):
    j = pl.program_id(1)

    @pl.when(j == 0)
    def _():
        x = x_ref[...]
        xn_ref[...] = _rms(x, g_ref[...]).astype(BF16)
        o_ref[...] = x

    xn = xn_ref[...]
    a = _dot(xn, wg_ref[...])
    b = _dot(xn, wu_ref[...])
    h = (a * jax.nn.sigmoid(a) * b).astype(BF16)
    o_ref[...] += _dot(h, wd_ref[...])

    if final_norm:
        @pl.when(j == pl.num_programs(1) - 1)
        def _():
            o_ref[...] = _rms(o_ref[...], gf_ref[...])


def ffn(x, g, wg, wu, wd, layer, g_final=None):
    m, d = x.shape
    f = wg.shape[2]
    tm = min(CFG["ffn_tm"], m)
    tf = min(CFG["ffn_tf"] * min(CFG["ffn_tm"] // tm, 2), f)
    final_norm = g_final is not None
    gf = (g_final if final_norm else g).reshape(1, d)
    return pl.pallas_call(
        functools.partial(_ffn_kernel, final_norm=final_norm),
        grid=(m // tm, f // tf),
        in_specs=[pl.BlockSpec((tm, d), lambda i, j: (i, 0)),
                  pl.BlockSpec((1, d), lambda i, j: (0, 0)),
                  pl.BlockSpec((None, d, tf), lambda i, j: (layer, 0, j)),
                  pl.BlockSpec((None, d, tf), lambda i, j: (layer, 0, j)),
                  pl.BlockSpec((None, tf, d), lambda i, j: (layer, j, 0)),
                  pl.BlockSpec((1, d), lambda i, j: (0, 0))],
        out_specs=pl.BlockSpec((tm, d), lambda i, j: (i, 0)),
        out_shape=jax.ShapeDtypeStruct((m, d), F32),
        scratch_shapes=[pltpu.VMEM((tm, d), BF16)],
        compiler_params=_cp(("parallel", "arbitrary")),
        name="ffn",
    )(x, g.reshape(1, d), wg, wu, wd, gf)


def _hgrn_levels(c):
    lv, h = [], c // 2
    while h >= 8:
        lv.append(h)
        h //= 2
    return lv


def _bcast_rows(g, rows, rep):
    return jnp.concatenate([jnp.broadcast_to(g[r:r + 1, :], (rep, g.shape[1])) for r in rows], axis=0)


def _hgrn_kernel(q_ref, f_ref, v_ref, gt_ref, lbl_ref, gain_ref, s0_ref, l_ref, og_ref, st_ref, stt_ref,
                 *, c, n_sub, hb, layer):
    cb = pl.program_id(2)
    dk = HG_DK

    @pl.when(cb == 0)
    def _():
        for h in range(hb):
            stt_ref[h] = s0_ref[0, h].T

    lg = lbl_ref[...]
    e = jnp.exp(lg - jnp.max(lg, axis=0, keepdims=True))
    p = e / jnp.sum(e, axis=0, keepdims=True)
    lb = p[0:1, :]
    for i in range(1, layer + 1):
        lb = lb + p[i:i + 1, :]
    lb = jnp.maximum(lb - p[0:1, :], 0.0)

    ti = lax.broadcasted_iota(jnp.int32, (c, c), 0)
    si = lax.broadcasted_iota(jnp.int32, (c, c), 1)
    levels = _hgrn_levels(c)
    masks = []
    for hh in levels:
        sh = int(math.log2(2 * hh))
        masks.append(((ti >> sh) == (si >> sh)) & ((ti & (2 * hh - 1)) >= hh) & ((si & (2 * hh - 1)) < hh))
    mask_loc = ((ti >> 3) == (si >> 3)) & (si <= ti)
    lmat = l_ref[...]
    gain = gain_ref[...]

    def chunk(ci, carry):
        r0 = pl.multiple_of(ci * c, c)
        fp = f_ref[pl.ds(r0, c), :]
        f = lb + (1.0 - lb) * jax.nn.sigmoid(fp)
        g = jnp.log2(jnp.maximum(f, F_MIN))
        kk = 1.0 - f
        g_hi = g.astype(BF16)
        r1 = g - g_hi.astype(F32)
        g_mid = r1.astype(BF16)
        g_lo = (r1 - g_mid.astype(F32)).astype(BF16)
        gc = _dot(lmat, g_hi) + _dot(lmat, g_mid) + _dot(lmat, g_lo)
        qa = q_ref[pl.ds(r0, c), :]
        va = v_ref[pl.ds(r0, c), :]
        ga = gt_ref[pl.ds(r0, c), :]
        for h in range(hb):
            sl = slice(h * dk, (h + 1) * dk)
            gh, qh, kh = gc[:, sl], qa[:, sl], kk[:, sl]
            vh = va[:, sl].astype(BF16)
            stt = stt_ref[h]
            inter = _dot_nt((qh * jnp.exp2(gh)).astype(BF16), stt.astype(BF16))
            g_last = gh[c - 1:c, :]
            k_dec = (kh * jnp.exp2(g_last - gh)).astype(BF16)
            stt_ref[h] = stt * jnp.exp2(g_last) + lax.dot_general(
                vh, k_dec, (((0,), (0,)), ((), ())), preferred_element_type=F32)
            q16, k16 = qh.astype(BF16), kh.astype(BF16)
            g_loc = _bcast_rows(gh, [8 * b + 3 for b in range(c // 8)], 8)
            sc = jnp.where(mask_loc,
                           _dot_nt(q16 * jnp.exp2(gh - g_loc).astype(BF16),
                                   k16 * jnp.exp2(g_loc - gh).astype(BF16)), 0.0)
            for hh, mk in zip(levels, masks):
                g_mid_rows = _bcast_rows(gh, [b * 2 * hh + hh - 1 for b in range(c // (2 * hh))], 2 * hh)
                dist = lax.bitcast_convert_type(gh - g_mid_rows, jnp.uint32) | jnp.uint32(0x80000000)
                fac = jnp.exp2(lax.bitcast_convert_type(dist, F32)).astype(BF16)
                sc = jnp.where(mk, _dot_nt(q16 * fac, k16 * fac), sc)
            o = inter + _dot(sc.astype(BF16), vh)
            on = _rms(o, gain)
            gate = ga[:, sl]
            og_ref[pl.ds(r0, c), sl] = (on * (gate * jax.nn.sigmoid(gate))).astype(BF16)
        return carry

    lax.fori_loop(0, n_sub, chunk, 0, unroll=min(CFG["hg_unroll"], n_sub))

    @pl.when(cb == pl.num_programs(2) - 1)
    def _():
        for h in range(hb):
            st_ref[0, h] = stt_ref[h].T


def hgrn_recurrence(proj, lb_logits, gain, s0, batch, t, layer):
    s0, s0_layer = s0
    d = HG_HEADS * HG_DK
    c = min(CFG["hg_chunk"], t)
    rows = min(CFG["hg_rows"], t)
    hb = CFG["hg_heads"] if t > c else HG_HEADS
    w = hb * HG_DK
    ncb = t // rows
    n_layers = lb_logits.shape[0]
    lmat = jnp.asarray(np.tril(np.ones((c, c), np.float32)), BF16)

    def col(sec):
        return lambda b, hg, cb: (b * ncb + cb, sec * (d // w) + hg)

    og, st = pl.pallas_call(
        functools.partial(_hgrn_kernel, c=c, n_sub=rows // c, hb=hb, layer=layer),
        grid=(batch, HG_HEADS // hb, ncb),
        in_specs=[pl.BlockSpec((rows, w), col(0)),
                  pl.BlockSpec((rows, w), col(1)),
                  pl.BlockSpec((rows, w), col(2)),
                  pl.BlockSpec((rows, w), col(3)),
                  pl.BlockSpec((n_layers, w), lambda b, hg, cb: (0, hg)),
                  pl.BlockSpec((1, HG_DK), lambda b, hg, cb: (0, 0)),
                  pl.BlockSpec((None, 1, hb, HG_DK, HG_DK), lambda b, hg, cb: (s0_layer, b, hg, 0, 0)),
                  pl.BlockSpec((c, c), lambda b, hg, cb: (0, 0))],
        out_specs=[pl.BlockSpec((rows, w), lambda b, hg, cb: (b * ncb + cb, hg)),
                   pl.BlockSpec((1, hb, HG_DK, HG_DK), lambda b, hg, cb: (b, hg, 0, 0))],
        out_shape=[jax.ShapeDtypeStruct((batch * t, d), BF16),
                   jax.ShapeDtypeStruct((batch, HG_HEADS, HG_DK, HG_DK), F32)],
        scratch_shapes=[pltpu.VMEM((hb, HG_DK, HG_DK), F32)],
        compiler_params=_cp(("parallel", "parallel", "arbitrary")),
        name="hgrn_recurrence",
    )(proj, proj, proj, proj, lb_logits, gain.reshape(1, HG_DK), s0, lmat)
    return og, st


def _pool_kernel(x_ref, g_ref, hist_ref, w_ref, sc_ref, o_ref, st_ref, buf_ref, *, tm, pos0):
    tb = pl.program_id(1)
    x = x_ref[...]
    xn = _rms(x, g_ref[...])
    hp = POOL_HIST + 1

    @pl.when(tb == 0)
    def _():
        buf_ref[0:hp, :] = hist_ref[0]

    @pl.when(tb > 0)
    def _():
        buf_ref[0:hp, :] = buf_ref[tm:tm + hp, :]

    buf_ref[hp:hp + tm, :] = xn
    st_ref[0] = buf_ref[tm:tm + hp, :]
    pos = pos0 + tb * tm + lax.broadcasted_iota(jnp.int32, (tm, 1), 0)
    gc = x.shape[1] // len(POOL_WINDOWS)
    for gi, win in enumerate(POOL_WINDOWS):
        cs = slice(gi * gc, (gi + 1) * gc)
        ws = xn[:, cs]
        for sft in range(1, win):
            ws = ws + buf_ref[hp - sft:hp - sft + tm, cs]
        cnt = jnp.minimum(win, pos + 1).astype(F32)
        pooled = (ws / cnt - xn[:, cs]).astype(BF16)
        o_ref[:, cs] = x[:, cs] + _dot(pooled, w_ref[gi]) * sc_ref[:, cs]


def pool_mixer(x, g, hist, w, scale, batch, t, pos0):
    d = x.shape[1]
    tm = min(CFG["pool_tm"], t)
    nb = t // tm
    hp = POOL_HIST + 1
    hist16 = jnp.concatenate([jnp.zeros((batch, 1, d), F32), hist], axis=1)
    ng, gc = w.shape[0], w.shape[1]
    return pl.pallas_call(
        functools.partial(_pool_kernel, tm=tm, pos0=pos0),
        grid=(batch, nb),
        in_specs=[pl.BlockSpec((tm, d), lambda b, i: (b * nb + i, 0)),
                  pl.BlockSpec((1, d), lambda b, i: (0, 0)),
                  pl.BlockSpec((1, hp, d), lambda b, i: (b, 0, 0)),
                  pl.BlockSpec((ng, gc, gc), lambda b, i: (0, 0, 0)),
                  pl.BlockSpec((1, d), lambda b, i: (0, 0))],
        out_specs=[pl.BlockSpec((tm, d), lambda b, i: (b * nb + i, 0)),
                   pl.BlockSpec((1, hp, d), lambda b, i: (b, 0, 0))],
        out_shape=[jax.ShapeDtypeStruct(x.shape, F32),
                   jax.ShapeDtypeStruct((batch, hp, d), F32)],
        scratch_shapes=[pltpu.VMEM((tm + hp, d), F32)],
        compiler_params=_cp(("parallel", "arbitrary")),
        name="pool_mixer",
    )(x, g.reshape(1, d), hist16, w, scale.reshape(1, d))


def _rel_bucket_np(rel):
    nb = REL_BUCKETS // 2
    max_exact = nb // 2
    n = np.abs(rel)
    large = max_exact + (np.log(np.maximum(n, max_exact).astype(np.float64) / max_exact)
                         / math.log(REL_MAX_DIST / max_exact) * (nb - max_exact)).astype(np.int64)
    large = np.minimum(large, nb - 1)
    return (np.where(rel > 0, nb, 0) + np.where(n < max_exact, n, large)).astype(np.int32)


def _bucket_tile(q_pos, k_pos):
    q_pos, k_pos = np.asarray(q_pos)[:, None], np.asarray(k_pos)[None, :]
    b = _rel_bucket_np(k_pos - q_pos)
    return np.where((k_pos // ATT_CHUNK) <= (q_pos // ATT_CHUNK), b, -1).astype(np.int32)


def _bias_kernel(tab_ref, bk_ref, o_ref, *, present):
    h = pl.program_id(0)
    rows = bk_ref.shape[0] // len(present)
    for k, values in enumerate(present):
        bk = bk_ref[k * rows:(k + 1) * rows, :]
        bias = jnp.full(bk.shape, MASK_VALUE, F32)
        for b in values:
            bias = jnp.where(bk == b, tab_ref[b, h] * LOG2E, bias)
        o_ref[0, k * rows:(k + 1) * rows, :] = bias


def bias_tiles(table, bucket, bands=1):
    r, c = bucket.shape
    nh = table.shape[1]
    present = tuple(tuple(int(b) for b in np.unique(band) if b >= 0) for band in np.split(bucket, bands, axis=0))
    return pl.pallas_call(
        functools.partial(_bias_kernel, present=present),
        grid=(nh,),
        in_specs=[pl.BlockSpec(memory_space=pltpu.SMEM),
                  pl.BlockSpec((r, c), lambda h: (0, 0))],
        out_specs=pl.BlockSpec((1, r, c), lambda h: (h, 0, 0)),
        out_shape=jax.ShapeDtypeStruct((nh, r, c), F32),
        compiler_params=_cp(("arbitrary",)),
        name="bias_tiles",
    )(table, jnp.asarray(bucket))


def _lambda(lq1, lk1, lq2, lk2, lambda_init):
    return (jnp.exp(jnp.sum(lq1 * lk1, axis=-1, keepdims=True))
            - jnp.exp(jnp.sum(lq2 * lk2, axis=-1, keepdims=True)) + lambda_init)


def _softmax_step(q, k, v, bias, m_ref, l_ref, acc_ref, idx):
    s = _dot_nt(q, k) + bias
    m_prev = m_ref[idx]
    m_new = jnp.maximum(m_prev, jnp.max(s, axis=-1, keepdims=True))
    p = jnp.exp2(s - m_new)
    alpha = jnp.exp2(m_prev - m_new)
    l_ref[idx] = alpha * l_ref[idx] + jnp.sum(p, axis=-1, keepdims=True)
    acc_ref[idx] = alpha * acc_ref[idx] + _dot(p.astype(BF16), v)
    m_ref[idx] = m_new


def _flash_kernel(q_ref, k_ref, vt_ref, bias_ref, lam_ref, gain_ref, o_ref,
                  m_ref, l_ref, acc_ref, sa_ref, sb_ref, pa_ref, pb_ref, ala_ref, alb_ref, *, tq, tk, lambda_init):
    i = pl.program_id(1)
    dh = DA_DH
    m_ref[...] = jnp.full(m_ref.shape, -jnp.inf, F32)
    l_ref[...] = jnp.zeros(l_ref.shape, F32)
    acc_ref[...] = jnp.zeros(acc_ref.shape, F32)
    q = q_ref[...]

    nkb = vt_ref.shape[0]

    def scores(j, s_ref):
        jc = jnp.minimum(j, nkb - 1)
        kb = k_ref[pl.ds(pl.multiple_of(jc * tk, tk), tk), :]
        for c in range(2):
            s_ref[c] = _dot_nt(kb[:, c * dh:(c + 1) * dh], q[:, c * dh:(c + 1) * dh])

    r = bias_ref.shape[1] - 3

    def softmax(j, s_ref, p_ref, al_ref):
        bias = bias_ref[0, jnp.clip(j - (r * i - 2), 0, r + 2)]
        for c in range(2):
            s = s_ref[c] + bias
            m_prev = m_ref[c]
            m_new = jnp.maximum(m_prev, jnp.max(s, axis=0, keepdims=True))
            p = jnp.exp2(s - m_new)
            alpha = jnp.exp2(m_prev - m_new)
            l_ref[c] = alpha * l_ref[c] + jnp.sum(p, axis=0, keepdims=True)
            m_ref[c] = m_new
            al_ref[c] = alpha
            p_ref[c] = p.astype(BF16)

    def values(j, p_ref, al_ref):
        vt = vt_ref[jnp.minimum(j, nkb - 1)]
        for c in range(2):
            acc_ref[c] = al_ref[c] * acc_ref[c] + _dot(vt, p_ref[c])

    scores(0, sa_ref)
    scores(1, sb_ref)
    softmax(0, sa_ref, pa_ref, ala_ref)

    def pair(t, carry):
        j = 2 * t
        scores(j + 2, sa_ref)
        softmax(j + 1, sb_ref, pb_ref, alb_ref)
        values(j, pa_ref, ala_ref)
        scores(j + 3, sb_ref)
        softmax(j + 2, sa_ref, pa_ref, ala_ref)
        values(j + 1, pb_ref, alb_ref)
        return carry

    lax.fori_loop(0, (r * i + r + 1) // 2, pair, 0)

    lam = _lambda(lam_ref[0:1, :], lam_ref[1:2, :], lam_ref[2:3, :], lam_ref[3:4, :], lambda_init)
    o = acc_ref[0] / l_ref[0] - lam * (acc_ref[1] / l_ref[1])
    on = o * lax.rsqrt(jnp.mean(o * o, axis=0, keepdims=True) + EPS) * (1.0 - lambda_init)
    o_ref[...] = (on.T * gain_ref[...]).astype(BF16)


def flash_diff_attention(q, k, vt, table, lam_params, gain, lambda_init):
    t, d = q.shape
    tq, tk = min(CFG["fa_tq"], t), min(CFG["fa_tk"], t)
    assert tq % tk == 0 and tk % ATT_CHUNK == 0 and vt.shape == (t // tk, d, tk)
    r = tq // tk
    hw = 2 * DA_DH
    nh = d // hw
    far = _rel_bucket_np(-np.arange(tk + 1, max(t, tk + 2)))
    far_bucket = int(far[0])
    assert (far == far_bucket).all()
    near = _bucket_tile(tk + np.arange(tq), np.arange((r + 1) * tk)).T
    bucket = np.concatenate([np.full((tk, tq), far_bucket, np.int32), near,
                             np.full((tk, tq), -1, np.int32)], axis=0)
    nb = r + 3
    bank = bias_tiles(table, bucket, bands=nb).reshape(nh, nb, tk, tq)
    once = pl.Buffered(1)
    return pl.pallas_call(
        functools.partial(_flash_kernel, tq=tq, tk=tk, lambda_init=lambda_init),
        grid=(nh, t // tq),
        in_specs=[pl.BlockSpec((tq, hw), lambda h, i: (i, h)),
                  pl.BlockSpec((t, hw), lambda h, i: (0, h), pipeline_mode=once),
                  pl.BlockSpec((t // tk, hw, tk), lambda h, i: (0, h, 0), pipeline_mode=once),
                  pl.BlockSpec((1, nb, tk, tq), lambda h, i: (h, 0, 0, 0), pipeline_mode=once),
                  pl.BlockSpec((4, DA_DH), lambda h, i: (0, 0)),
                  pl.BlockSpec((1, hw), lambda h, i: (0, 0))],
        out_specs=pl.BlockSpec((tq, hw), lambda h, i: (i, h)),
        out_shape=jax.ShapeDtypeStruct((t, d), BF16),
        scratch_shapes=[pltpu.VMEM((2, 1, tq), F32), pltpu.VMEM((2, 1, tq), F32),
                        pltpu.VMEM((2, hw, tq), F32),
                        pltpu.VMEM((2, tk, tq), F32), pltpu.VMEM((2, tk, tq), F32),
                        pltpu.VMEM((2, tk, tq), BF16), pltpu.VMEM((2, tk, tq), BF16),
                        pltpu.VMEM((2, 1, tq), F32), pltpu.VMEM((2, 1, tq), F32)],
        compiler_params=_cp(("parallel", "arbitrary")),
        name="flash_diff_attention",
    )(q, k, vt, bank, lam_params, gain.reshape(1, hw))


def _sample_attn_kernel(tab_ref, q_ref, ck_ref, cvl_ref, cvh_ref, kn_ref, vn_ref, blast_ref, bnew_ref, lam_ref,
                        gain_ref, o_ref, m_ref, l_ref, acc_ref, *, tk, far_bucket, lambda_init):
    j = pl.program_id(1)
    nj = pl.num_programs(1)
    dh = DA_DH
    hw = 2 * dh
    nh = q_ref.shape[1] // hw
    ts = q_ref.shape[0]

    @pl.when(j == 0)
    def _():
        m_ref[...] = jnp.full(m_ref.shape, -jnp.inf, F32)
        l_ref[...] = jnp.zeros(l_ref.shape, F32)
        acc_ref[...] = jnp.zeros(acc_ref.shape, F32)

    q = q_ref[...]

    def block(k_of, v_of, bias_fn):
        for h in range(nh):
            vb = v_of(h)
            for c in range(2):
                cs = slice((2 * h + c) * dh, (2 * h + c + 1) * dh)
                _softmax_step(q[:, cs], k_of(2 * h + c), vb, bias_fn(h), m_ref, l_ref, acc_ref, 2 * h + c)

    def cache_k_of(hc):
        return ck_ref[0, pl.ds(hc, tk, stride=2 * nh), :].astype(BF16)

    def cache_v_of(h):
        return jnp.concatenate([cvl_ref[0, pl.ds(h, tk, stride=nh), :],
                                cvh_ref[0, pl.ds(h, tk, stride=nh), :]], axis=1).astype(BF16)

    @pl.when(j < nj - 1)
    def _():
        block(cache_k_of, cache_v_of, lambda h: tab_ref[far_bucket, h] * LOG2E)

    @pl.when(j == nj - 1)
    def _():
        block(cache_k_of, cache_v_of, lambda h: blast_ref[h])
        block(lambda hc: kn_ref[:, hc * dh:(hc + 1) * dh], lambda h: vn_ref[:, h * hw:(h + 1) * hw],
              lambda h: bnew_ref[h][:, :ts])
        lam = _lambda(lam_ref[0:1, :], lam_ref[1:2, :], lam_ref[2:3, :], lam_ref[3:4, :], lambda_init)
        for h in range(nh):
            o = acc_ref[2 * h] / l_ref[2 * h] - lam * (acc_ref[2 * h + 1] / l_ref[2 * h + 1])
            o_ref[:, h * hw:(h + 1) * hw] = (_rms(o, gain_ref[...]) * (1.0 - lambda_init)).astype(BF16)


def sample_diff_attention(q, k_new, v_new, cache_k, cache_v, layer, table, lam_params, gain, lambda_init):
    n_layers, batch, past, nhc, dh = cache_k.shape
    d = nhc * dh
    ts = q.shape[0] // batch
    hw = 2 * DA_DH
    nh = d // hw
    tk = min(CFG["sa_tk"], past)
    nkb = past // tk
    q_pos = past + np.arange(ts)
    blast = bias_tiles(table, _bucket_tile(q_pos, past - tk + np.arange(tk)))
    bnew = bias_tiles(table, _bucket_tile(q_pos, past + np.arange(128)))
    if nkb > 1:
        far = _bucket_tile(q_pos, np.arange(past - tk))
        far_bucket = int(far[0, 0])
        assert (far == far_bucket).all()
    else:
        far_bucket = 0
    cache_v_rows = cache_v.reshape(n_layers, batch, past * nh, hw)
    return pl.pallas_call(
        functools.partial(_sample_attn_kernel, tk=tk, far_bucket=far_bucket, lambda_init=lambda_init),
        grid=(batch, nkb),
        in_specs=[pl.BlockSpec(memory_space=pltpu.SMEM),
                  pl.BlockSpec((ts, d), lambda b, j: (b, 0)),
                  pl.BlockSpec((None, 1, tk * nhc, dh), lambda b, j: (layer, b, j, 0)),
                  pl.BlockSpec((None, 1, tk * nh, dh), lambda b, j: (layer, b, j, 0)),
                  pl.BlockSpec((None, 1, tk * nh, dh), lambda b, j: (layer, b, j, 1)),
                  pl.BlockSpec((ts, d), lambda b, j: (b, 0)),
                  pl.BlockSpec((ts, d), lambda b, j: (b, 0)),
                  pl.BlockSpec((nh, ts, tk), lambda b, j: (0, 0, 0)),
                  pl.BlockSpec((nh, ts, 128), lambda b, j: (0, 0, 0)),
                  pl.BlockSpec((4, DA_DH), lambda b, j: (0, 0)),
                  pl.BlockSpec((1, hw), lambda b, j: (0, 0))],
        out_specs=pl.BlockSpec((ts, d), lambda b, j: (b, 0)),
        out_shape=jax.ShapeDtypeStruct((batch * ts, d), BF16),
        scratch_shapes=[pltpu.VMEM((2 * nh, ts, 1), F32), pltpu.VMEM((2 * nh, ts, 1), F32),
                        pltpu.VMEM((2 * nh, ts, hw), F32)],
        compiler_params=_cp(("parallel", "arbitrary")),
        name="sample_diff_attention",
    )(table, q, cache_k.reshape(n_layers, batch, past * nhc, dh), cache_v_rows, cache_v_rows,
      k_new, v_new, blast, bnew, lam_params, gain.reshape(1, hw))


def kernel(x_prompt, x_sample, state_hgrn, state_pool, cache_k, cache_v, norm_mix, norm_ffn, norm_final, hgrn_w_q, hgrn_w_f, hgrn_w_i, hgrn_w_g, hgrn_w_o, hgrn_lb_logits, hgrn_norm_gain, pool_w, pool_scale, attn_w_q, attn_w_k, attn_w_v, attn_w_o, attn_lambda_q1, attn_lambda_k1, attn_lambda_q2, attn_lambda_k2, attn_subln_gain, rel_bias_table, ffn_w_gate, ffn_w_up, ffn_w_down):
    bp, tp, d = x_prompt.shape
    bs, ts, _ = x_sample.shape
    past = cache_k.shape[2]
    depth = norm_mix.shape[0]
    assert bp == 1
    bf = lambda a: a.astype(BF16)
    xs = [x_prompt.reshape(bp * tp, d), x_sample.reshape(bs * ts, d)]
    dims = [(bp, tp), (bs, ts)]
    hg, pool_st, k_out, v_out = [[], []], [[], []], [[], []], [[], []]
    w_g, w_u, w_d = bf(ffn_w_gate), bf(ffn_w_up), bf(ffn_w_down)
    zero_state = jnp.zeros((1, bp, HG_HEADS, HG_DK, HG_DK), F32)

    for i in range(depth):
        m, j = i % N_MIXERS, i // N_MIXERS
        if m == 0:
            w_cat = bf(jnp.concatenate([hgrn_w_q[j], hgrn_w_f[j], hgrn_w_i[j], hgrn_w_g[j]], axis=1))
            w_o = bf(hgrn_w_o[j])
            s0s = [(zero_state, 0), (state_hgrn, j)]
            for r in range(2):
                b, t = dims[r]
                proj = norm_mm(xs[r], norm_mix[i], w_cat)
                og, st = hgrn_recurrence(proj, hgrn_lb_logits, hgrn_norm_gain[j], s0s[r], b, t, j)
                xs[r] = mm_res(og, w_o, xs[r])
                hg[r].append(st)
        elif m == 1:
            w_p = bf(pool_w[j])
            hists = [jnp.zeros((bp, POOL_HIST, d), F32), state_pool[j]]
            for r in range(2):
                b, t = dims[r]
                xs[r], st = pool_mixer(xs[r], norm_mix[i], hists[r], w_p, pool_scale[j], b, t, (0, past)[r])
                pool_st[r].append(st[:, 1:, :])
        else:
            lambda_init = 0.8 - 0.6 * math.exp(-0.3 * i)
            w_qkv = bf(jnp.concatenate([attn_w_q[j], attn_w_k[j], attn_w_v[j]], axis=1))
            w_o = bf(attn_w_o[j])
            lam_params = jnp.stack([attn_lambda_q1[j], attn_lambda_k1[j], attn_lambda_q2[j], attn_lambda_k2[j]])
            for r in range(2):
                b, t = dims[r]
                q, k32, k16, v32, v16 = qkv_proj(xs[r], norm_mix[i], w_qkv, DA_DH ** -0.5 * LOG2E,
                                                 min(CFG["fa_tk"], t) if r == 0 else None)
                if r == 0:
                    o = flash_diff_attention(q, k16, v16, rel_bias_table, lam_params, attn_subln_gain[j],
                                             lambda_init)
                else:
                    o = sample_diff_attention(q, k16, v16, cache_k, cache_v, j, rel_bias_table, lam_params,
                                              attn_subln_gain[j], lambda_init)
                xs[r] = mm_res(o, w_o, xs[r])
                k_out[r].append(k32.reshape(b, t, 2 * DA_HEADS, DA_DH))
                v_out[r].append(v32.reshape(b, t, DA_HEADS, 2 * DA_DH))
        g_final = norm_final if i == depth - 1 else None
        for r in range(2):
            xs[r] = ffn(xs[r], norm_ffn[i], w_g, w_u, w_d, i, g_final)

    return (xs[0].reshape(bp, tp, d), xs[1].reshape(bs, ts, d),
            jnp.stack(hg[0]), jnp.stack(hg[1]), jnp.stack(pool_st[0]), jnp.stack(pool_st[1]),
            jnp.stack(k_out[0]), jnp.stack(v_out[0]), jnp.stack(k_out[1]), jnp.stack(v_out[1]))
```

```python
import functools
import math

import numpy as np
import jax
import jax.numpy as jnp
from jax import lax
from jax.experimental import pallas as pl
from jax.experimental.pallas import tpu as pltpu

F32 = jnp.float32
BF16 = jnp.bfloat16

EPS = 1e-6
LOG2E = math.log2(math.e)
F_MIN = 1e-6
MASK_VALUE = -1e30
HG_HEADS = 16
HG_DK = 128
POOL_WINDOWS = (2, 4, 8, 16)
POOL_HIST = 15
DA_HEADS = 8
DA_DH = 128
ATT_CHUNK = 64
REL_BUCKETS = 32
REL_MAX_DIST = 128
N_MIXERS = 3

CFG = dict(
    mm_tm=1024, mm_tn=1024,
    qkv_tn=512,
    ffn_tm=1024, ffn_tf=256,
    pool_tm=512,
    hg_chunk=256, hg_rows=512, hg_heads=8, hg_unroll=2,
    fa_tq=512, fa_tk=512,
    sa_tk=1024,
    vmem=56 * 1024 * 1024,
)


def _cp(sem):
    return pltpu.CompilerParams(dimension_semantics=sem, vmem_limit_bytes=CFG["vmem"])


def _rms(x, g):
    return x * lax.rsqrt(jnp.mean(x * x, axis=-1, keepdims=True) + EPS) * g


def _dot(a, b):
    return jnp.dot(a, b, preferred_element_type=F32)


def _dot_nt(a, b):
    return lax.dot_general(a, b, (((1,), (1,)), ((), ())), preferred_element_type=F32)


def _norm_mm_kernel(x_ref, g_ref, w_ref, o_ref, xn_ref):
    @pl.when(pl.program_id(1) == 0)
    def _():
        xn_ref[...] = _rms(x_ref[...], g_ref[...]).astype(BF16)

    o_ref[...] = _dot(xn_ref[...], w_ref[...])


def norm_mm(x, g, w):
    m, d = x.shape
    n = w.shape[1]
    tm, tn = min(CFG["mm_tm"], m), min(CFG["mm_tn"], n)
    return pl.pallas_call(
        _norm_mm_kernel,
        grid=(m // tm, n // tn),
        in_specs=[pl.BlockSpec((tm, d), lambda i, j: (i, 0)),
                  pl.BlockSpec((1, d), lambda i, j: (0, 0)),
                  pl.BlockSpec((d, tn), lambda i, j: (0, j))],
        out_specs=pl.BlockSpec((tm, tn), lambda i, j: (i, j)),
        out_shape=jax.ShapeDtypeStruct((m, n), F32),
        scratch_shapes=[pltpu.VMEM((tm, d), BF16)],
        compiler_params=_cp(("parallel", "arbitrary")),
        name="norm_mm",
    )(x, g.reshape(1, d), w)


def _qkv_kernel(x_ref, g_ref, w_ref, q_ref, k32_ref, k16_ref, v32_ref, v16_ref, xn_ref, *, nq, q_scale, v_transposed):
    j = pl.program_id(1)

    @pl.when(j == 0)
    def _():
        xn_ref[...] = _rms(x_ref[...], g_ref[...]).astype(BF16)

    y = _dot(xn_ref[...], w_ref[...])

    @pl.when(j < nq)
    def _():
        q_ref[...] = (y * q_scale).astype(BF16)

    @pl.when((j >= nq) & (j < 2 * nq))
    def _():
        k32_ref[...] = y
        k16_ref[...] = y.astype(BF16)

    @pl.when(j >= 2 * nq)
    def _():
        v32_ref[...] = y
        if v_transposed:
            tkv = v16_ref.shape[2]
            for s in range(v16_ref.shape[0]):
                v16_ref[s] = y[s * tkv:(s + 1) * tkv, :].T.astype(BF16)
        else:
            v16_ref[...] = y.astype(BF16)


def qkv_proj(x, g, w, q_scale, v_transposed_rows):
    m, d = x.shape
    n = w.shape[1] // 3
    tm, tn = min(CFG["mm_tm"], m), min(CFG["qkv_tn"], n)
    nq = n // tn
    v_transposed = v_transposed_rows is not None
    spec = lambda off: pl.BlockSpec((tm, tn), lambda i, j: (i, jnp.clip(j - off, 0, nq - 1)))
    if v_transposed:
        tkv = v_transposed_rows
        v16_spec = pl.BlockSpec((tm // tkv, tn, tkv), lambda i, j: (i, jnp.clip(j - 2 * nq, 0, nq - 1), 0))
        v16_shape = jax.ShapeDtypeStruct((m // tkv, n, tkv), BF16)
    else:
        v16_spec, v16_shape = spec(2 * nq), jax.ShapeDtypeStruct((m, n), BF16)
    return pl.pallas_call(
        functools.partial(_qkv_kernel, nq=nq, q_scale=q_scale, v_transposed=v_transposed),
        grid=(m // tm, 3 * nq),
        in_specs=[pl.BlockSpec((tm, d), lambda i, j: (i, 0)),
                  pl.BlockSpec((1, d), lambda i, j: (0, 0)),
                  pl.BlockSpec((d, tn), lambda i, j: (0, j))],
        out_specs=[spec(0), spec(nq), spec(nq), spec(2 * nq), v16_spec],
        out_shape=[jax.ShapeDtypeStruct((m, n), BF16), jax.ShapeDtypeStruct((m, n), F32),
                   jax.ShapeDtypeStruct((m, n), BF16), jax.ShapeDtypeStruct((m, n), F32), v16_shape],
        scratch_shapes=[pltpu.VMEM((tm, d), BF16)],
        compiler_params=_cp(("parallel", "arbitrary")),
        name="qkv_proj",
    )(x, g.reshape(1, d), w)


def _mm_res_kernel(a_ref, w_ref, r_ref, o_ref):
    o_ref[...] = r_ref[...] + _dot(a_ref[...], w_ref[...])


def mm_res(a, w, r):
    m, k = a.shape
    n = w.shape[1]
    tm, tn = min(CFG["mm_tm"], m), min(CFG["mm_tn"], n)
    return pl.pallas_call(
        _mm_res_kernel,
        grid=(m // tm, n // tn),
        in_specs=[pl.BlockSpec((tm, k), lambda i, j: (i, 0)),
                  pl.BlockSpec((k, tn), lambda i, j: (0, j)),
                  pl.BlockSpec((tm, tn), lambda i, j: (i, j))],
        out_specs=pl.BlockSpec((tm, tn), lambda i, j: (i, j)),
        out_shape=jax.ShapeDtypeStruct((m, n), F32),
        compiler_params=_cp(("parallel", "arbitrary")),
        name="mm_res",
    )(a, w, r)


def _ffn_kernel(x_ref, g_ref, wg_ref, wu_ref, wd_ref, gf_ref, o_ref, xn_ref, *, final_norm):
    j = pl.program_id(1)

    @pl.when(j == 0)
    def _():
        x = x_ref[...]
        xn_ref[...] = _rms(x, g_ref[...]).astype(BF16)
        o_ref[...] = x

    xn = xn_ref[...]
    a = _dot(xn, wg_ref[...])
    b = _dot(xn, wu_ref[...])
    h = (a * jax.nn.sigmoid(a) * b).astype(BF16)
    o_ref[...] += _dot(h, wd_ref[...])

    if final_norm:
        @pl.when(j == pl.num_programs(1) - 1)
        def _():
            o_ref[...] = _rms(o_ref[...], gf_ref[...])


def ffn(x, g, wg, wu, wd, layer, g_final=None):
    m, d = x.shape
    f = wg.shape[2]
    tm = min(CFG["ffn_tm"], m)
    tf = min(CFG["ffn_tf"] * min(CFG["ffn_tm"] // tm, 2), f)
    final_norm = g_final is not None
    gf = (g_final if final_norm else g).reshape(1, d)
    return pl.pallas_call(
        functools.partial(_ffn_kernel, final_norm=final_norm),
        grid=(m // tm, f // tf),
        in_specs=[pl.BlockSpec((tm, d), lambda i, j: (i, 0)),
                  pl.BlockSpec((1, d), lambda i, j: (0, 0)),
                  pl.BlockSpec((None, d, tf), lambda i, j: (layer, 0, j)),
                  pl.BlockSpec((None, d, tf), lambda i, j: (layer, 0, j)),
                  pl.BlockSpec((None, tf, d), lambda i, j: (layer, j, 0)),
                  pl.BlockSpec((1, d), lambda i, j: (0, 0))],
        out_specs=pl.BlockSpec((tm, d), lambda i, j: (i, 0)),
        out_shape=jax.ShapeDtypeStruct((m, d), F32),
        scratch_shapes=[pltpu.VMEM((tm, d), BF16)],
        compiler_params=_cp(("parallel", "arbitrary")),
        name="ffn",
    )(x, g.reshape(1, d), wg, wu, wd, gf)


def _hgrn_levels(c):
    lv, h = [], c // 2
    while h >= 8:
        lv.append(h)
        h //= 2
    return lv


def _bcast_rows(g, rows, rep):
    return jnp.concatenate([jnp.broadcast_to(g[r:r + 1, :], (rep, g.shape[1])) for r in rows], axis=0)


def _hgrn_kernel(q_ref, f_ref, v_ref, gt_ref, lbl_ref, gain_ref, s0_ref, l_ref, og_ref, st_ref, stt_ref,
                 *, c, n_sub, hb, layer):
    cb = pl.program_id(2)
    dk = HG_DK

    @pl.when(cb == 0)
    def _():
        for h in range(hb):
            stt_ref[h] = s0_ref[0, h].T

    lg = lbl_ref[...]
    e = jnp.exp(lg - jnp.max(lg, axis=0, keepdims=True))
    p = e / jnp.sum(e, axis=0, keepdims=True)
    lb = p[0:1, :]
    for i in range(1, layer + 1):
        lb = lb + p[i:i + 1, :]
    lb = jnp.maximum(lb - p[0:1, :], 0.0)

    ti = lax.broadcasted_iota(jnp.int32, (c, c), 0)
    si = lax.broadcasted_iota(jnp.int32, (c, c), 1)
    levels = _hgrn_levels(c)
    masks = []
    for hh in levels:
        sh = int(math.log2(2 * hh))
        masks.append(((ti >> sh) == (si >> sh)) & ((ti & (2 * hh - 1)) >= hh) & ((si & (2 * hh - 1)) < hh))
    mask_loc = ((ti >> 3) == (si >> 3)) & (si <= ti)
    lmat = l_ref[...]
    gain = gain_ref[...]

    def chunk(ci, carry):
        r0 = pl.multiple_of(ci * c, c)
        fp = f_ref[pl.ds(r0, c), :]
        f = lb + (1.0 - lb) * jax.nn.sigmoid(fp)
        g = jnp.log2(jnp.maximum(f, F_MIN))
        kk = 1.0 - f
        g_hi = g.astype(BF16)
        r1 = g - g_hi.astype(F32)
        g_mid = r1.astype(BF16)
        g_lo = (r1 - g_mid.astype(F32)).astype(BF16)
        gc = _dot(lmat, g_hi) + _dot(lmat, g_mid) + _dot(lmat, g_lo)
        qa = q_ref[pl.ds(r0, c), :]
        va = v_ref[pl.ds(r0, c), :]
        ga = gt_ref[pl.ds(r0, c), :]
        for h in range(hb):
            sl = slice(h * dk, (h + 1) * dk)
            gh, qh, kh = gc[:, sl], qa[:, sl], kk[:, sl]
            vh = va[:, sl].astype(BF16)
            stt = stt_ref[h]
            inter = _dot_nt((qh * jnp.exp2(gh)).astype(BF16), stt.astype(BF16))
            g_last = gh[c - 1:c, :]
            k_dec = (kh * jnp.exp2(g_last - gh)).astype(BF16)
            stt_ref[h] = stt * jnp.exp2(g_last) + lax.dot_general(
                vh, k_dec, (((0,), (0,)), ((), ())), preferred_element_type=F32)
            q16, k16 = qh.astype(BF16), kh.astype(BF16)
            g_loc = _bcast_rows(gh, [8 * b + 3 for b in range(c // 8)], 8)
            sc = jnp.where(mask_loc,
                           _dot_nt(q16 * jnp.exp2(gh - g_loc).astype(BF16),
                                   k16 * jnp.exp2(g_loc - gh).astype(BF16)), 0.0)
            for hh, mk in zip(levels, masks):
                g_mid_rows = _bcast_rows(gh, [b * 2 * hh + hh - 1 for b in range(c // (2 * hh))], 2 * hh)
                dist = lax.bitcast_convert_type(gh - g_mid_rows, jnp.uint32) | jnp.uint32(0x80000000)
                fac = jnp.exp2(lax.bitcast_convert_type(dist, F32)).astype(BF16)
                sc = jnp.where(mk, _dot_nt(q16 * fac, k16 * fac), sc)
            o = inter + _dot(sc.astype(BF16), vh)
            on = _rms(o, gain)
            gate = ga[:, sl]
            og_ref[pl.ds(r0, c), sl] = (on * (gate * jax.nn.sigmoid(gate))).astype(BF16)
        return carry

    lax.fori_loop(0, n_sub, chunk, 0, unroll=min(CFG["hg_unroll"], n_sub))

    @pl.when(cb == pl.num_programs(2) - 1)
    def _():
        for h in range(hb):
            st_ref[0, h] = stt_ref[h].T


def hgrn_recurrence(proj, lb_logits, gain, s0, batch, t, layer):
    s0, s0_layer = s0
    d = HG_HEADS * HG_DK
    c = min(CFG["hg_chunk"], t)
    rows = min(CFG["hg_rows"], t)
    hb = CFG["hg_heads"] if t > c else HG_HEADS
    w = hb * HG_DK
    ncb = t // rows
    n_layers = lb_logits.shape[0]
    lmat = jnp.asarray(np.tril(np.ones((c, c), np.float32)), BF16)

    def col(sec):
        return lambda b, hg, cb: (b * ncb + cb, sec * (d // w) + hg)

    og, st = pl.pallas_call(
        functools.partial(_hgrn_kernel, c=c, n_sub=rows // c, hb=hb, layer=layer),
        grid=(batch, HG_HEADS // hb, ncb),
        in_specs=[pl.BlockSpec((rows, w), col(0)),
                  pl.BlockSpec((rows, w), col(1)),
                  pl.BlockSpec((rows, w), col(2)),
                  pl.BlockSpec((rows, w), col(3)),
                  pl.BlockSpec((n_layers, w), lambda b, hg, cb: (0, hg)),
                  pl.BlockSpec((1, HG_DK), lambda b, hg, cb: (0, 0)),
                  pl.BlockSpec((None, 1, hb, HG_DK, HG_DK), lambda b, hg, cb: (s0_layer, b, hg, 0, 0)),
                  pl.BlockSpec((c, c), lambda b, hg, cb: (0, 0))],
        out_specs=[pl.BlockSpec((rows, w), lambda b, hg, cb: (b * ncb + cb, hg)),
                   pl.BlockSpec((1, hb, HG_DK, HG_DK), lambda b, hg, cb: (b, hg, 0, 0))],
        out_shape=[jax.ShapeDtypeStruct((batch * t, d), BF16),
                   jax.ShapeDtypeStruct((batch, HG_HEADS, HG_DK, HG_DK), F32)],
        scratch_shapes=[pltpu.VMEM((hb, HG_DK, HG_DK), F32)],
        compiler_params=_cp(("parallel", "parallel", "arbitrary")),
        name="hgrn_recurrence",
    )(proj, proj, proj, proj, lb_logits, gain.reshape(1, HG_DK), s0, lmat)
    return og, st


def _pool_kernel(x_ref, g_ref, hist_ref, w_ref, sc_ref, o_ref, st_ref, buf_ref, *, tm, pos0):
    tb = pl.program_id(1)
    x = x_ref[...]
    xn = _rms(x, g_ref[...])
    hp = POOL_HIST + 1

    @pl.when(tb == 0)
    def _():
        buf_ref[0:hp, :] = hist_ref[0]

    @pl.when(tb > 0)
    def _():
        buf_ref[0:hp, :] = buf_ref[tm:tm + hp, :]

    buf_ref[hp:hp + tm, :] = xn
    st_ref[0] = buf_ref[tm:tm + hp, :]
    pos = pos0 + tb * tm + lax.broadcasted_iota(jnp.int32, (tm, 1), 0)
    gc = x.shape[1] // len(POOL_WINDOWS)
    for gi, win in enumerate(POOL_WINDOWS):
        cs = slice(gi * gc, (gi + 1) * gc)
        ws = xn[:, cs]
        for sft in range(1, win):
            ws = ws + buf_ref[hp - sft:hp - sft + tm, cs]
        cnt = jnp.minimum(win, pos + 1).astype(F32)
        pooled = (ws / cnt - xn[:, cs]).astype(BF16)
        o_ref[:, cs] = x[:, cs] + _dot(pooled, w_ref[gi]) * sc_ref[:, cs]


def pool_mixer(x, g, hist, w, scale, batch, t, pos0):
    d = x.shape[1]
    tm = min(CFG["pool_tm"], t)
    nb = t // tm
    hp = POOL_HIST + 1
    hist16 = jnp.concatenate([jnp.zeros((batch, 1, d), F32), hist], axis=1)
    ng, gc = w.shape[0], w.shape[1]
    return pl.pallas_call(
        functools.partial(_pool_kernel, tm=tm, pos0=pos0),
        grid=(batch, nb),
        in_specs=[pl.BlockSpec((tm, d), lambda b, i: (b * nb + i, 0)),
                  pl.BlockSpec((1, d), lambda b, i: (0, 0)),
                  pl.BlockSpec((1, hp, d), lambda b, i: (b, 0, 0)),
                  pl.BlockSpec((ng, gc, gc), lambda b, i: (0, 0, 0)),
                  pl.BlockSpec((1, d), lambda b, i: (0, 0))],
        out_specs=[pl.BlockSpec((tm, d), lambda b, i: (b * nb + i, 0)),
                   pl.BlockSpec((1, hp, d), lambda b, i: (b, 0, 0))],
        out_shape=[jax.ShapeDtypeStruct(x.shape, F32),
                   jax.ShapeDtypeStruct((batch, hp, d), F32)],
        scratch_shapes=[pltpu.VMEM((tm + hp, d), F32)],
        compiler_params=_cp(("parallel", "arbitrary")),
        name="pool_mixer",
    )(x, g.reshape(1, d), hist16, w, scale.reshape(1, d))


def _rel_bucket_np(rel):
    nb = REL_BUCKETS // 2
    max_exact = nb // 2
    n = np.abs(rel)
    large = max_exact + (np.log(np.maximum(n, max_exact).astype(np.float64) / max_exact)
                         / math.log(REL_MAX_DIST / max_exact) * (nb - max_exact)).astype(np.int64)
    large = np.minimum(large, nb - 1)
    return (np.where(rel > 0, nb, 0) + np.where(n < max_exact, n, large)).astype(np.int32)


def _bucket_tile(q_pos, k_pos):
    q_pos, k_pos = np.asarray(q_pos)[:, None], np.asarray(k_pos)[None, :]
    b = _rel_bucket_np(k_pos - q_pos)
    return np.where((k_pos // ATT_CHUNK) <= (q_pos // ATT_CHUNK), b, -1).astype(np.int32)


def _bias_kernel(tab_ref, bk_ref, o_ref, *, present):
    h = pl.program_id(0)
    rows = bk_ref.shape[0] // len(present)
    for k, values in enumerate(present):
        bk = bk_ref[k * rows:(k + 1) * rows, :]
        bias = jnp.full(bk.shape, MASK_VALUE, F32)
        for b in values:
            bias = jnp.where(bk == b, tab_ref[b, h] * LOG2E, bias)
        o_ref[0, k * rows:(k + 1) * rows, :] = bias


def bias_tiles(table, bucket, bands=1):
    r, c = bucket.shape
    nh = table.shape[1]
    present = tuple(tuple(int(b) for b in np.unique(band) if b >= 0) for band in np.split(bucket, bands, axis=0))
    return pl.pallas_call(
        functools.partial(_bias_kernel, present=present),
        grid=(nh,),
        in_specs=[pl.BlockSpec(memory_space=pltpu.SMEM),
                  pl.BlockSpec((r, c), lambda h: (0, 0))],
        out_specs=pl.BlockSpec((1, r, c), lambda h: (h, 0, 0)),
        out_shape=jax.ShapeDtypeStruct((nh, r, c), F32),
        compiler_params=_cp(("arbitrary",)),
        name="bias_tiles",
    )(table, jnp.asarray(bucket))


def _lambda(lq1, lk1, lq2, lk2, lambda_init):
    return (jnp.exp(jnp.sum(lq1 * lk1, axis=-1, keepdims=True))
            - jnp.exp(jnp.sum(lq2 * lk2, axis=-1, keepdims=True)) + lambda_init)


def _softmax_step(q, k, v, bias, m_ref, l_ref, acc_ref, idx):
    s = _dot_nt(q, k) + bias
    m_prev = m_ref[idx]
    m_new = jnp.maximum(m_prev, jnp.max(s, axis=-1, keepdims=True))
    p = jnp.exp2(s - m_new)
    alpha = jnp.exp2(m_prev - m_new)
    l_ref[idx] = alpha * l_ref[idx] + jnp.sum(p, axis=-1, keepdims=True)
    acc_ref[idx] = alpha * acc_ref[idx] + _dot(p.astype(BF16), v)
    m_ref[idx] = m_new


def _flash_kernel(q_ref, k_ref, vt_ref, bias_ref, lam_ref, gain_ref, o_ref,
                  m_ref, l_ref, acc_ref, sa_ref, sb_ref, pa_ref, pb_ref, ala_ref, alb_ref, *, tq, tk, lambda_init):
    i = pl.program_id(1)
    dh = DA_DH
    m_ref[...] = jnp.full(m_ref.shape, -jnp.inf, F32)
    l_ref[...] = jnp.zeros(l_ref.shape, F32)
    acc_ref[...] = jnp.zeros(acc_ref.shape, F32)
    q = q_ref[...]

    nkb = vt_ref.shape[0]

    def scores(j, s_ref):
        jc = jnp.minimum(j, nkb - 1)
        kb = k_ref[pl.ds(pl.multiple_of(jc * tk, tk), tk), :]
        for c in range(2):
            s_ref[c] = _dot_nt(kb[:, c * dh:(c + 1) * dh], q[:, c * dh:(c + 1) * dh])

    r = bias_ref.shape[1] - 3

    def softmax(j, s_ref, p_ref, al_ref):
        bias = bias_ref[0, jnp.clip(j - (r * i - 2), 0, r + 2)]
        for c in range(2):
            s = s_ref[c] + bias
            m_prev = m_ref[c]
            m_new = jnp.maximum(m_prev, jnp.max(s, axis=0, keepdims=True))
            p = jnp.exp2(s - m_new)
            alpha = jnp.exp2(m_prev - m_new)
            l_ref[c] = alpha * l_ref[c] + jnp.sum(p, axis=0, keepdims=True)
            m_ref[c] = m_new
            al_ref[c] = alpha
            p_ref[c] = p.astype(BF16)

    def values(j, p_ref, al_ref):
        vt = vt_ref[jnp.minimum(j, nkb - 1)]
        for c in range(2):
            acc_ref[c] = al_ref[c] * acc_ref[c] + _dot(vt, p_ref[c])

    scores(0, sa_ref)
    scores(1, sb_ref)
    softmax(0, sa_ref, pa_ref, ala_ref)

    def pair(t, carry):
        j = 2 * t
        scores(j + 2, sa_ref)
        softmax(j + 1, sb_ref, pb_ref, alb_ref)
        values(j, pa_ref, ala_ref)
        scores(j + 3, sb_ref)
        softmax(j + 2, sa_ref, pa_ref, ala_ref)
        values(j + 1, pb_ref, alb_ref)
        return carry

    lax.fori_loop(0, (r * i + r + 1) // 2, pair, 0)

    lam = _lambda(lam_ref[0:1, :], lam_ref[1:2, :], lam_ref[2:3, :], lam_ref[3:4, :], lambda_init)
    o = acc_ref[0] / l_ref[0] - lam * (acc_ref[1] / l_ref[1])
    on = o * lax.rsqrt(jnp.mean(o * o, axis=0, keepdims=True) + EPS) * (1.0 - lambda_init)
    o_ref[...] = (on.T * gain_ref[...]).astype(BF16)


def flash_diff_attention(q, k, vt, table, lam_params, gain, lambda_init):
    t, d = q.shape
    tq, tk = min(CFG["fa_tq"], t), min(CFG["fa_tk"], t)
    assert tq % tk == 0 and tk % ATT_CHUNK == 0 and vt.shape == (t // tk, d, tk)
    r = tq // tk
    hw = 2 * DA_DH
    nh = d // hw
    far = _rel_bucket_np(-np.arange(tk + 1, max(t, tk + 2)))
    far_bucket = int(far[0])
    assert (far == far_bucket).all()
    near = _bucket_tile(tk + np.arange(tq), np.arange((r + 1) * tk)).T
    bucket = np.concatenate([np.full((tk, tq), far_bucket, np.int32), near,
                             np.full((tk, tq), -1, np.int32)], axis=0)
    nb = r + 3
    bank = bias_tiles(table, bucket, bands=nb).reshape(nh, nb, tk, tq)
    once = pl.Buffered(1)
    return pl.pallas_call(
        functools.partial(_flash_kernel, tq=tq, tk=tk, lambda_init=lambda_init),
        grid=(nh, t // tq),
        in_specs=[pl.BlockSpec((tq, hw), lambda h, i: (i, h)),
                  pl.BlockSpec((t, hw), lambda h, i: (0, h), pipeline_mode=once),
                  pl.BlockSpec((t // tk, hw, tk), lambda h, i: (0, h, 0), pipeline_mode=once),
                  pl.BlockSpec((1, nb, tk, tq), lambda h, i: (h, 0, 0, 0), pipeline_mode=once),
                  pl.BlockSpec((4, DA_DH), lambda h, i: (0, 0)),
                  pl.BlockSpec((1, hw), lambda h, i: (0, 0))],
        out_specs=pl.BlockSpec((tq, hw), lambda h, i: (i, h)),
        out_shape=jax.ShapeDtypeStruct((t, d), BF16),
        scratch_shapes=[pltpu.VMEM((2, 1, tq), F32), pltpu.VMEM((2, 1, tq), F32),
                        pltpu.VMEM((2, hw, tq), F32),
                        pltpu.VMEM((2, tk, tq), F32), pltpu.VMEM((2, tk, tq), F32),
                        pltpu.VMEM((2, tk, tq), BF16), pltpu.VMEM((2, tk, tq), BF16),
                        pltpu.VMEM((2, 1, tq), F32), pltpu.VMEM((2, 1, tq), F32)],
        compiler_params=_cp(("parallel", "arbitrary")),
        name="flash_diff_attention",
    )(q, k, vt, bank, lam_params, gain.reshape(1, hw))


def _sample_attn_kernel(tab_ref, q_ref, ck_ref, cvl_ref, cvh_ref, kn_ref, vn_ref, blast_ref, bnew_ref, lam_ref,
                        gain_ref, o_ref, m_ref, l_ref, acc_ref, *, tk, far_bucket, lambda_init):
    j = pl.program_id(1)
    nj = pl.num_programs(1)
    dh = DA_DH
    hw = 2 * dh
    nh = q_ref.shape[1] // hw
    ts = q_ref.shape[0]

    @pl.when(j == 0)
    def _():
        m_ref[...] = jnp.full(m_ref.shape, -jnp.inf, F32)
        l_ref[...] = jnp.zeros(l_ref.shape, F32)
        acc_ref[...] = jnp.zeros(acc_ref.shape, F32)

    q = q_ref[...]

    def block(k_of, v_of, bias_fn):
        for h in range(nh):
            vb = v_of(h)
            for c in range(2):
                cs = slice((2 * h + c) * dh, (2 * h + c + 1) * dh)
                _softmax_step(q[:, cs], k_of(2 * h + c), vb, bias_fn(h), m_ref, l_ref, acc_ref, 2 * h + c)

    def cache_k_of(hc):
        return ck_ref[0, pl.ds(hc, tk, stride=2 * nh), :].astype(BF16)

    def cache_v_of(h):
        return jnp.concatenate([cvl_ref[0, pl.ds(h, tk, stride=nh), :],
                                cvh_ref[0, pl.ds(h, tk, stride=nh), :]], axis=1).astype(BF16)

    @pl.when(j < nj - 1)
    def _():
        block(cache_k_of, cache_v_of, lambda h: tab_ref[far_bucket, h] * LOG2E)

    @pl.when(j == nj - 1)
    def _():
        block(cache_k_of, cache_v_of, lambda h: blast_ref[h])
        block(lambda hc: kn_ref[:, hc * dh:(hc + 1) * dh], lambda h: vn_ref[:, h * hw:(h + 1) * hw],
              lambda h: bnew_ref[h][:, :ts])
        lam = _lambda(lam_ref[0:1, :], lam_ref[1:2, :], lam_ref[2:3, :], lam_ref[3:4, :], lambda_init)
        for h in range(nh):
            o = acc_ref[2 * h] / l_ref[2 * h] - lam * (acc_ref[2 * h + 1] / l_ref[2 * h + 1])
            o_ref[:, h * hw:(h + 1) * hw] = (_rms(o, gain_ref[...]) * (1.0 - lambda_init)).astype(BF16)


def sample_diff_attention(q, k_new, v_new, cache_k, cache_v, layer, table, lam_params, gain, lambda_init):
    n_layers, batch, past, nhc, dh = cache_k.shape
    d = nhc * dh
    ts = q.shape[0] // batch
    hw = 2 * DA_DH
    nh = d // hw
    tk = min(CFG["sa_tk"], past)
    nkb = past // tk
    q_pos = past + np.arange(ts)
    blast = bias_tiles(table, _bucket_tile(q_pos, past - tk + np.arange(tk)))
    bnew = bias_tiles(table, _bucket_tile(q_pos, past + np.arange(128)))
    if nkb > 1:
        far = _bucket_tile(q_pos, np.arange(past - tk))
        far_bucket = int(far[0, 0])
        assert (far == far_bucket).all()
    else:
        far_bucket = 0
    cache_v_rows = cache_v.reshape(n_layers, batch, past * nh, hw)
    return pl.pallas_call(
        functools.partial(_sample_attn_kernel, tk=tk, far_bucket=far_bucket, lambda_init=lambda_init),
        grid=(batch, nkb),
        in_specs=[pl.BlockSpec(memory_space=pltpu.SMEM),
                  pl.BlockSpec((ts, d), lambda b, j: (b, 0)),
                  pl.BlockSpec((None, 1, tk * nhc, dh), lambda b, j: (layer, b, j, 0)),
                  pl.BlockSpec((None, 1, tk * nh, dh), lambda b, j: (layer, b, j, 0)),
                  pl.BlockSpec((None, 1, tk * nh, dh), lambda b, j: (layer, b, j, 1)),
                  pl.BlockSpec((ts, d), lambda b, j: (b, 0)),
                  pl.BlockSpec((ts, d), lambda b, j: (b, 0)),
                  pl.BlockSpec((nh, ts, tk), lambda b, j: (0, 0, 0)),
                  pl.BlockSpec((nh, ts, 128), lambda b, j: (0, 0, 0)),
                  pl.BlockSpec((4, DA_DH), lambda b, j: (0, 0)),
                  pl.BlockSpec((1, hw), lambda b, j: (0, 0))],
        out_specs=pl.BlockSpec((ts, d), lambda b, j: (b, 0)),
        out_shape=jax.ShapeDtypeStruct((batch * ts, d), BF16),
        scratch_shapes=[pltpu.VMEM((2 * nh, ts, 1), F32), pltpu.VMEM((2 * nh, ts, 1), F32),
                        pltpu.VMEM((2 * nh, ts, hw), F32)],
        compiler_params=_cp(("parallel", "arbitrary")),
        name="sample_diff_attention",
    )(table, q, cache_k.reshape(n_layers, batch, past * nhc, dh), cache_v_rows, cache_v_rows,
      k_new, v_new, blast, bnew, lam_params, gain.reshape(1, hw))


def kernel(x_prompt, x_sample, state_hgrn, state_pool, cache_k, cache_v, norm_mix, norm_ffn, norm_final, hgrn_w_q, hgrn_w_f, hgrn_w_i, hgrn_w_g, hgrn_w_o, hgrn_lb_logits, hgrn_norm_gain, pool_w, pool_scale, attn_w_q, attn_w_k, attn_w_v, attn_w_o, attn_lambda_q1, attn_lambda_k1, attn_lambda_q2, attn_lambda_k2, attn_subln_gain, rel_bias_table, ffn_w_gate, ffn_w_up, ffn_w_down):
    bp, tp, d = x_prompt.shape
    bs, ts, _ = x_sample.shape
    past = cache_k.shape[2]
    depth = norm_mix.shape[0]
    assert bp == 1
    bf = lambda a: a.astype(BF16)
    xs = [x_prompt.reshape(bp * tp, d), x_sample.reshape(bs * ts, d)]
    dims = [(bp, tp), (bs, ts)]
    hg, pool_st, k_out, v_out = [[], []], [[], []], [[], []], [[], []]
    w_g, w_u, w_d = bf(ffn_w_gate), bf(ffn_w_up), bf(ffn_w_down)
    zero_state = jnp.zeros((1, bp, HG_HEADS, HG_DK, HG_DK), F32)

    for i in range(depth):
        m, j = i % N_MIXERS, i // N_MIXERS
        if m == 0:
            w_cat = bf(jnp.concatenate([hgrn_w_q[j], hgrn_w_f[j], hgrn_w_i[j], hgrn_w_g[j]], axis=1))
            w_o = bf(hgrn_w_o[j])
            s0s = [(zero_state, 0), (state_hgrn, j)]
            for r in range(2):
                b, t = dims[r]
                proj = norm_mm(xs[r], norm_mix[i], w_cat)
                og, st = hgrn_recurrence(proj, hgrn_lb_logits, hgrn_norm_gain[j], s0s[r], b, t, j)
                xs[r] = mm_res(og, w_o, xs[r])
                hg[r].append(st)
        elif m == 1:
            w_p = bf(pool_w[j])
            hists = [jnp.zeros((bp, POOL_HIST, d), F32), state_pool[j]]
            for r in range(2):
                b, t = dims[r]
                xs[r], st = pool_mixer(xs[r], norm_mix[i], hists[r], w_p, pool_scale[j], b, t, (0, past)[r])
                pool_st[r].append(st[:, 1:, :])
        else:
            lambda_init = 0.8 - 0.6 * math.exp(-0.3 * i)
            w_qkv = bf(jnp.concatenate([attn_w_q[j], attn_w_k[j], attn_w_v[j]], axis=1))
            w_o = bf(attn_w_o[j])
            lam_params = jnp.stack([attn_lambda_q1[j], attn_lambda_k1[j], attn_lambda_q2[j], attn_lambda_k2[j]])
            for r in range(2):
                b, t = dims[r]
                q, k32, k16, v32, v16 = qkv_proj(xs[r], norm_mix[i], w_qkv, DA_DH ** -0.5 * LOG2E,
                                                 min(CFG["fa_tk"], t) if r == 0 else None)
                if r == 0:
                    o = flash_diff_attention(q, k16, v16, rel_bias_table, lam_params, attn_subln_gain[j],
                                             lambda_init)
                else:
                    o = sample_diff_attention(q, k16, v16, cache_k, cache_v, j, rel_bias_table, lam_params,
                                              attn_subln_gain[j], lambda_init)
                xs[r] = mm_res(o, w_o, xs[r])
                k_out[r].append(k32.reshape(b, t, 2 * DA_HEADS, DA_DH))
                v_out[r].append(v32.reshape(b, t, DA_HEADS, 2 * DA_DH))
        g_final = norm_final if i == depth - 1 else None
        for r in range(2):
            xs[r] = ffn(xs[r], norm_ffn[i], w_g, w_u, w_d, i, g_final)

    return (xs[0].reshape(bp, tp, d), xs[1].reshape(bs, ts, d),
            jnp.stack(hg[0]), jnp.stack(hg[1]), jnp.stack(pool_st[0]), jnp.stack(pool_st[1]),
            jnp.stack(k_out[0]), jnp.stack(v_out[0]), jnp.stack(k_out[1]), jnp.stack(v_out[1]))
```

```python
import functools
import math

import numpy as np
import jax
import jax.numpy as jnp
from jax import lax
from jax.experimental import pallas as pl
from jax.experimental.pallas import tpu as pltpu

F32 = jnp.float32
BF16 = jnp.bfloat16

EPS = 1e-6
LOG2E = math.log2(math.e)
F_MIN = 1e-6
MASK_VALUE = -1e30
HG_HEADS = 16
HG_DK = 128
POOL_WINDOWS = (2, 4, 8, 16)
POOL_HIST = 15
DA_HEADS = 8
DA_DH = 128
ATT_CHUNK = 64
REL_BUCKETS = 32
REL_MAX_DIST = 128
N_MIXERS = 3

CFG = dict(
    mm_tm=1024, mm_tn=1024,
    qkv_tn=512,
    ffn_tm=1024, ffn_tf=256,
    pool_tm=512,
    hg_chunk=256, hg_rows=512, hg_heads=8, hg_unroll=2,
    fa_tq=512, fa_tk=512,
    sa_tk=1024,
    vmem=56 * 1024 * 1024,
)


def _cp(sem):
    return pltpu.CompilerParams(dimension_semantics=sem, vmem_limit_bytes=CFG["vmem"])


def _rms(x, g):
    return x * lax.rsqrt(jnp.mean(x * x, axis=-1, keepdims=True) + EPS) * g


def _dot(a, b):
    return jnp.dot(a, b, preferred_element_type=F32)


def _dot_nt(a, b):
    return lax.dot_general(a, b, (((1,), (1,)), ((), ())), preferred_element_type=F32)


def _norm_mm_kernel(x_ref, g_ref, w_ref, o_ref, xn_ref):
    @pl.when(pl.program_id(1) == 0)
    def _():
        xn_ref[...] = _rms(x_ref[...], g_ref[...]).astype(BF16)

    o_ref[...] = _dot(xn_ref[...], w_ref[...])


def norm_mm(x, g, w):
    m, d = x.shape
    n = w.shape[1]
    tm, tn = min(CFG["mm_tm"], m), min(CFG["mm_tn"], n)
    return pl.pallas_call(
        _norm_mm_kernel,
        grid=(m // tm, n // tn),
        in_specs=[pl.BlockSpec((tm, d), lambda i, j: (i, 0)),
                  pl.BlockSpec((1, d), lambda i, j: (0, 0)),
                  pl.BlockSpec((d, tn), lambda i, j: (0, j))],
        out_specs=pl.BlockSpec((tm, tn), lambda i, j: (i, j)),
        out_shape=jax.ShapeDtypeStruct((m, n), F32),
        scratch_shapes=[pltpu.VMEM((tm, d), BF16)],
        compiler_params=_cp(("parallel", "arbitrary")),
        name="norm_mm",
    )(x, g.reshape(1, d), w)


def _qkv_kernel(x_ref, g_ref, w_ref, q_ref, k32_ref, k16_ref, v32_ref, v16_ref, xn_ref, *, nq, q_scale, v_transposed):
    j = pl.program_id(1)

    @pl.when(j == 0)
    def _():
        xn_ref[...] = _rms(x_ref[...], g_ref[...]).astype(BF16)

    y = _dot(xn_ref[...], w_ref[...])

    @pl.when(j < nq)
    def _():
        q_ref[...] = (y * q_scale).astype(BF16)

    @pl.when((j >= nq) & (j < 2 * nq))
    def _():
        k32_ref[...] = y
        k16_ref[...] = y.astype(BF16)

    @pl.when(j >= 2 * nq)
    def _():
        v32_ref[...] = y
        if v_transposed:
            tkv = v16_ref.shape[2]
            for s in range(v16_ref.shape[0]):
                v16_ref[s] = y[s * tkv:(s + 1) * tkv, :].T.astype(BF16)
        else:
            v16_ref[...] = y.astype(BF16)


def qkv_proj(x, g, w, q_scale, v_transposed_rows):
    m, d = x.shape
    n = w.shape[1] // 3
    tm, tn = min(CFG["mm_tm"], m), min(CFG["qkv_tn"], n)
    nq = n // tn
    v_transposed = v_transposed_rows is not None
    spec = lambda off: pl.BlockSpec((tm, tn), lambda i, j: (i, jnp.clip(j - off, 0, nq - 1)))
    if v_transposed:
        tkv = v_transposed_rows
        v16_spec = pl.BlockSpec((tm // tkv, tn, tkv), lambda i, j: (i, jnp.clip(j - 2 * nq, 0, nq - 1), 0))
        v16_shape = jax.ShapeDtypeStruct((m // tkv, n, tkv), BF16)
    else:
        v16_spec, v16_shape = spec(2 * nq), jax.ShapeDtypeStruct((m, n), BF16)
    return pl.pallas_call(
        functools.partial(_qkv_kernel, nq=nq, q_scale=q_scale, v_transposed=v_transposed),
        grid=(m // tm, 3 * nq),
        in_specs=[pl.BlockSpec((tm, d), lambda i, j: (i, 0)),
                  pl.BlockSpec((1, d), lambda i, j: (0, 0)),
                  pl.BlockSpec((d, tn), lambda i, j: (0, j))],
        out_specs=[spec(0), spec(nq), spec(nq), spec(2 * nq), v16_spec],
        out_shape=[jax.ShapeDtypeStruct((m, n), BF16), jax.ShapeDtypeStruct((m, n), F32),
                   jax.ShapeDtypeStruct((m, n), BF16), jax.ShapeDtypeStruct((m, n), F32), v16_shape],
        scratch_shapes=[pltpu.VMEM((tm, d), BF16)],
        compiler_params=_cp(("parallel", "arbitrary")),
        name="qkv_proj",
    )(x, g.reshape(1, d), w)


def _mm_res_kernel(a_ref, w_ref, r_ref, o_ref):
    o_ref[...] = r_ref[...] + _dot(a_ref[...], w_ref[...])


def mm_res(a, w, r):
    m, k = a.shape
    n = w.shape[1]
    tm, tn = min(CFG["mm_tm"], m), min(CFG["mm_tn"], n)
    return pl.pallas_call(
        _mm_res_kernel,
        grid=(m // tm, n // tn),
        in_specs=[pl.BlockSpec((tm, k), lambda i, j: (i, 0)),
                  pl.BlockSpec((k, tn), lambda i, j: (0, j)),
                  pl.BlockSpec((tm, tn), lambda i, j: (i, j))],
        out_specs=pl.BlockSpec((tm, tn), lambda i, j: (i, j)),
        out_shape=jax.ShapeDtypeStruct((m, n), F32),
        compiler_params=_cp(("parallel", "arbitrary")),
        name="mm_res",
    )(a, w, r)


def _ffn_kernel(x_ref, g_ref, wg_ref, wu_ref, wd_ref, gf_ref, o_ref, xn_ref, *, final_norm):
    j = pl.program_id(1)

    @pl.when(j == 0)
    def _():
        x = x_ref[...]
        xn_ref[...] = _rms(x, g_ref[...]).astype(BF16)
        o_ref[...] = x

    xn = xn_ref[...]
    a = _dot(xn, wg_ref[...])
    b = _dot(xn, wu_ref[...])
    h = (a * jax.nn.sigmoid(a) * b).astype(BF16)
    o_ref[...] += _dot(h, wd_ref[...])

    if final_norm:
        @pl.when(j == pl.num_programs(1) - 1)
        def _():
            o_ref[...] = _rms(o_ref[...], gf_ref[...])


def ffn(x, g, wg, wu, wd, layer, g_final=None):
    m, d = x.shape
    f = wg.shape[2]
    tm = min(CFG["ffn_tm"], m)
    tf = min(CFG["ffn_tf"] * min(CFG["ffn_tm"] // tm, 2), f)
    final_norm = g_final is not None
    gf = (g_final if final_norm else g).reshape(1, d)
    return pl.pallas_call(
        functools.partial(_ffn_kernel, final_norm=final_norm),
        grid=(m // tm, f // tf),
        in_specs=[pl.BlockSpec((tm, d), lambda i, j: (i, 0)),
                  pl.BlockSpec((1, d), lambda i, j: (0, 0)),
                  pl.BlockSpec((None, d, tf), lambda i, j: (layer, 0, j)),
                  pl.BlockSpec((None, d, tf), lambda i, j: (layer, 0, j)),
                  pl.BlockSpec((None, tf, d), lambda i, j: (layer, j, 0)),
                  pl.BlockSpec((1, d), lambda i, j: (0, 0))],
        out_specs=pl.BlockSpec((tm, d), lambda i, j: (i, 0)),
        out_shape=jax.ShapeDtypeStruct((m, d), F32),
        scratch_shapes=[pltpu.VMEM((tm, d), BF16)],
        compiler_params=_cp(("parallel", "arbitrary")),
        name="ffn",
    )(x, g.reshape(1, d), wg, wu, wd, gf)


def _hgrn_levels(c):
    lv, h = [], c // 2
    while h >= 8:
        lv.append(h)
        h //= 2
    return lv


def _bcast_rows(g, rows, rep):
    return jnp.concatenate([jnp.broadcast_to(g[r:r + 1, :], (rep, g.shape[1])) for r in rows], axis=0)


def _hgrn_kernel(q_ref, f_ref, v_ref, gt_ref, lbl_ref, gain_ref, s0_ref, l_ref, og_ref, st_ref, stt_ref,
                 *, c, n_sub, hb, layer):
    cb = pl.program_id(2)
    dk = HG_DK

    @pl.when(cb == 0)
    def _():
        for h in range(hb):
            stt_ref[h] = s0_ref[0, h].T

    lg = lbl_ref[...]
    e = jnp.exp(lg - jnp.max(lg, axis=0, keepdims=True))
    p = e / jnp.sum(e, axis=0, keepdims=True)
    lb = p[0:1, :]
    for i in range(1, layer + 1):
        lb = lb + p[i:i + 1, :]
    lb = jnp.maximum(lb - p[0:1, :], 0.0)

    ti = lax.broadcasted_iota(jnp.int32, (c, c), 0)
    si = lax.broadcasted_iota(jnp.int32, (c, c), 1)
    levels = _hgrn_levels(c)
    masks = []
    for hh in levels:
        sh = int(math.log2(2 * hh))
        masks.append(((ti >> sh) == (si >> sh)) & ((ti & (2 * hh - 1)) >= hh) & ((si & (2 * hh - 1)) < hh))
    mask_loc = ((ti >> 3) == (si >> 3)) & (si <= ti)
    lmat = l_ref[...]
    gain = gain_ref[...]

    def chunk(ci, carry):
        r0 = pl.multiple_of(ci * c, c)
        fp = f_ref[pl.ds(r0, c), :]
        f = lb + (1.0 - lb) * jax.nn.sigmoid(fp)
        g = jnp.log2(jnp.maximum(f, F_MIN))
        kk = 1.0 - f
        g_hi = g.astype(BF16)
        r1 = g - g_hi.astype(F32)
        g_mid = r1.astype(BF16)
        g_lo = (r1 - g_mid.astype(F32)).astype(BF16)
        gc = _dot(lmat, g_hi) + _dot(lmat, g_mid) + _dot(lmat, g_lo)
        qa = q_ref[pl.ds(r0, c), :]
        va = v_ref[pl.ds(r0, c), :]
        ga = gt_ref[pl.ds(r0, c), :]
        for h in range(hb):
            sl = slice(h * dk, (h + 1) * dk)
            gh, qh, kh = gc[:, sl], qa[:, sl], kk[:, sl]
            vh = va[:, sl].astype(BF16)
            stt = stt_ref[h]
            inter = _dot_nt((qh * jnp.exp2(gh)).astype(BF16), stt.astype(BF16))
            g_last = gh[c - 1:c, :]
            k_dec = (kh * jnp.exp2(g_last - gh)).astype(BF16)
            stt_ref[h] = stt * jnp.exp2(g_last) + lax.dot_general(
                vh, k_dec, (((0,), (0,)), ((), ())), preferred_element_type=F32)
            q16, k16 = qh.astype(BF16), kh.astype(BF16)
            g_loc = _bcast_rows(gh, [8 * b + 3 for b in range(c // 8)], 8)
            sc = jnp.where(mask_loc,
                           _dot_nt(q16 * jnp.exp2(gh - g_loc).astype(BF16),
                                   k16 * jnp.exp2(g_loc - gh).astype(BF16)), 0.0)
            for hh, mk in zip(levels, masks):
                g_mid_rows = _bcast_rows(gh, [b * 2 * hh + hh - 1 for b in range(c // (2 * hh))], 2 * hh)
                dist = lax.bitcast_convert_type(gh - g_mid_rows, jnp.uint32) | jnp.uint32(0x80000000)
                fac = jnp.exp2(lax.bitcast_convert_type(dist, F32)).astype(BF16)
                sc = jnp.where(mk, _dot_nt(q16 * fac, k16 * fac), sc)
            o = inter + _dot(sc.astype(BF16), vh)
            on = _rms(o, gain)
            gate = ga[:, sl]
            og_ref[pl.ds(r0, c), sl] = (on * (gate * jax.nn.sigmoid(gate))).astype(BF16)
        return carry

    lax.fori_loop(0, n_sub, chunk, 0, unroll=min(CFG["hg_unroll"], n_sub))

    @pl.when(cb == pl.num_programs(2) - 1)
    def _():
        for h in range(hb):
            st_ref[0, h] = stt_ref[h].T


def hgrn_recurrence(proj, lb_logits, gain, s0, batch, t, layer):
    s0, s0_layer = s0
    d = HG_HEADS * HG_DK
    c = min(CFG["hg_chunk"], t)
    rows = min(CFG["hg_rows"], t)
    hb = CFG["hg_heads"] if t > c else HG_HEADS
    w = hb * HG_DK
    ncb = t // rows
    n_layers = lb_logits.shape[0]
    lmat = jnp.asarray(np.tril(np.ones((c, c), np.float32)), BF16)

    def col(sec):
        return lambda b, hg, cb: (b * ncb + cb, sec * (d // w) + hg)

    og, st = pl.pallas_call(
        functools.partial(_hgrn_kernel, c=c, n_sub=rows // c, hb=hb, layer=layer),
        grid=(batch, HG_HEADS // hb, ncb),
        in_specs=[pl.BlockSpec((rows, w), col(0)),
                  pl.BlockSpec((rows, w), col(1)),
                  pl.BlockSpec((rows, w), col(2)),
                  pl.BlockSpec((rows, w), col(3)),
                  pl.BlockSpec((n_layers, w), lambda b, hg, cb: (0, hg)),
                  pl.BlockSpec((1, HG_DK), lambda b, hg, cb: (0, 0)),
                  pl.BlockSpec((None, 1, hb, HG_DK, HG_DK), lambda b, hg, cb: (s0_layer, b, hg, 0, 0)),
                  pl.BlockSpec((c, c), lambda b, hg, cb: (0, 0))],
        out_specs=[pl.BlockSpec((rows, w), lambda b, hg, cb: (b * ncb + cb, hg)),
                   pl.BlockSpec((1, hb, HG_DK, HG_DK), lambda b, hg, cb: (b, hg, 0, 0))],
        out_shape=[jax.ShapeDtypeStruct((batch * t, d), BF16),
                   jax.ShapeDtypeStruct((batch, HG_HEADS, HG_DK, HG_DK), F32)],
        scratch_shapes=[pltpu.VMEM((hb, HG_DK, HG_DK), F32)],
        compiler_params=_cp(("parallel", "parallel", "arbitrary")),
        name="hgrn_recurrence",
    )(proj, proj, proj, proj, lb_logits, gain.reshape(1, HG_DK), s0, lmat)
    return og, st


def _pool_kernel(x_ref, g_ref, hist_ref, w_ref, sc_ref, o_ref, st_ref, buf_ref, *, tm, pos0):
    tb = pl.program_id(1)
    x = x_ref[...]
    xn = _rms(x, g_ref[...])
    hp = POOL_HIST + 1

    @pl.when(tb == 0)
    def _():
        buf_ref[0:hp, :] = hist_ref[0]

    @pl.when(tb > 0)
    def _():
        buf_ref[0:hp, :] = buf_ref[tm:tm + hp, :]

    buf_ref[hp:hp + tm, :] = xn
    st_ref[0] = buf_ref[tm:tm + hp, :]
    pos = pos0 + tb * tm + lax.broadcasted_iota(jnp.int32, (tm, 1), 0)
    gc = x.shape[1] // len(POOL_WINDOWS)
    for gi, win in enumerate(POOL_WINDOWS):
        cs = slice(gi * gc, (gi + 1) * gc)
        ws = xn[:, cs]
        for sft in range(1, win):
            ws = ws + buf_ref[hp - sft:hp - sft + tm, cs]
        cnt = jnp.minimum(win, pos + 1).astype(F32)
        pooled = (ws / cnt - xn[:, cs]).astype(BF16)
        o_ref[:, cs] = x[:, cs] + _dot(pooled, w_ref[gi]) * sc_ref[:, cs]


def pool_mixer(x, g, hist, w, scale, batch, t, pos0):
    d = x.shape[1]
    tm = min(CFG["pool_tm"], t)
    nb = t // tm
    hp = POOL_HIST + 1
    hist16 = jnp.concatenate([jnp.zeros((batch, 1, d), F32), hist], axis=1)
    ng, gc = w.shape[0], w.shape[1]
    return pl.pallas_call(
        functools.partial(_pool_kernel, tm=tm, pos0=pos0),
        grid=(batch, nb),
        in_specs=[pl.BlockSpec((tm, d), lambda b, i: (b * nb + i, 0)),
                  pl.BlockSpec((1, d), lambda b, i: (0, 0)),
                  pl.BlockSpec((1, hp, d), lambda b, i: (b, 0, 0)),
                  pl.BlockSpec((ng, gc, gc), lambda b, i: (0, 0, 0)),
                  pl.BlockSpec((1, d), lambda b, i: (0, 0))],
        out_specs=[pl.BlockSpec((tm, d), lambda b, i: (b * nb + i, 0)),
                   pl.BlockSpec((1, hp, d), lambda b, i: (b, 0, 0))],
        out_shape=[jax.ShapeDtypeStruct(x.shape, F32),
                   jax.ShapeDtypeStruct((batch, hp, d), F32)],
        scratch_shapes=[pltpu.VMEM((tm + hp, d), F32)],
        compiler_params=_cp(("parallel", "arbitrary")),
        name="pool_mixer",
    )(x, g.reshape(1, d), hist16, w, scale.reshape(1, d))


def _rel_bucket_np(rel):
    nb = REL_BUCKETS // 2
    max_exact = nb // 2
    n = np.abs(rel)
    large = max_exact + (np.log(np.maximum(n, max_exact).astype(np.float64) / max_exact)
                         / math.log(REL_MAX_DIST / max_exact) * (nb - max_exact)).astype(np.int64)
    large = np.minimum(large, nb - 1)
    return (np.where(rel > 0, nb, 0) + np.where(n < max_exact, n, large)).astype(np.int32)


def _bucket_tile(q_pos, k_pos):
    q_pos, k_pos = np.asarray(q_pos)[:, None], np.asarray(k_pos)[None, :]
    b = _rel_bucket_np(k_pos - q_pos)
    return np.where((k_pos // ATT_CHUNK) <= (q_pos // ATT_CHUNK), b, -1).astype(np.int32)


def _bias_kernel(tab_ref, bk_ref, o_ref, *, present):
    h = pl.program_id(0)
    rows = bk_ref.shape[0] // len(present)
    for k, values in enumerate(present):
        bk = bk_ref[k * rows:(k + 1) * rows, :]
        bias = jnp.full(bk.shape, MASK_VALUE, F32)
        for b in values:
            bias = jnp.where(bk == b, tab_ref[b, h] * LOG2E, bias)
        o_ref[0, k * rows:(k + 1) * rows, :] = bias


def bias_tiles(table, bucket, bands=1):
    r, c = bucket.shape
    nh = table.shape[1]
    present = tuple(tuple(int(b) for b in np.unique(band) if b >= 0) for band in np.split(bucket, bands, axis=0))
    return pl.pallas_call(
        functools.partial(_bias_kernel, present=present),
        grid=(nh,),
        in_specs=[pl.BlockSpec(memory_space=pltpu.SMEM),
                  pl.BlockSpec((r, c), lambda h: (0, 0))],
        out_specs=pl.BlockSpec((1, r, c), lambda h: (h, 0, 0)),
        out_shape=jax.ShapeDtypeStruct((nh, r, c), F32),
        compiler_params=_cp(("arbitrary",)),
        name="bias_tiles",
    )(table, jnp.asarray(bucket))


def _lambda(lq1, lk1, lq2, lk2, lambda_init):
    return (jnp.exp(jnp.sum(lq1 * lk1, axis=-1, keepdims=True))
            - jnp.exp(jnp.sum(lq2 * lk2, axis=-1, keepdims=True)) + lambda_init)


def _softmax_step(q, k, v, bias, m_ref, l_ref, acc_ref, idx):
    s = _dot_nt(q, k) + bias
    m_prev = m_ref[idx]
    m_new = jnp.maximum(m_prev, jnp.max(s, axis=-1, keepdims=True))
    p = jnp.exp2(s - m_new)
    alpha = jnp.exp2(m_prev - m_new)
    l_ref[idx] = alpha * l_ref[idx] + jnp.sum(p, axis=-1, keepdims=True)
    acc_ref[idx] = alpha * acc_ref[idx] + _dot(p.astype(BF16), v)
    m_ref[idx] = m_new


def _flash_kernel(tab_ref, q_ref, k_ref, vt_ref, bias_ref, lam_ref, gain_ref, o_ref,
                  m_ref, l_ref, acc_ref, sa_ref, sb_ref, pa_ref, pb_ref, ala_ref, alb_ref,
                  *, tq, tk, far_bucket, lambda_init):
    i = pl.program_id(1)
    dh = DA_DH
    m_ref[...] = jnp.full(m_ref.shape, -jnp.inf, F32)
    l_ref[...] = jnp.zeros(l_ref.shape, F32)
    acc_ref[...] = jnp.zeros(acc_ref.shape, F32)
    q = q_ref[...]

    nkb = vt_ref.shape[0]

    def scores(j, s_ref):
        jc = jnp.minimum(j, nkb - 1)
        kb = k_ref[pl.ds(pl.multiple_of(jc * tk, tk), tk), :]
        for c in range(2):
            s_ref[c] = _dot_nt(kb[:, c * dh:(c + 1) * dh], q[:, c * dh:(c + 1) * dh])

    r = bias_ref.shape[1] - 3
    c_far = tab_ref[far_bucket, pl.program_id(0)] * LOG2E

    def softmax(j, s_ref, p_ref, al_ref, far_only):
        cst = c_far if far_only else 0.0
        for c in range(2):
            s = s_ref[c]
            if not far_only:
                s = s + bias_ref[0, jnp.clip(j - (r * i - 2), 0, r + 2)]
            m_prev = m_ref[c]
            m_new = jnp.maximum(m_prev, jnp.max(s, axis=0, keepdims=True) + cst)
            p = jnp.exp2(s - (m_new - cst))
            alpha = jnp.exp2(m_prev - m_new)
            l_ref[c] = alpha * l_ref[c] + jnp.sum(p, axis=0, keepdims=True)
            m_ref[c] = m_new
            al_ref[c] = alpha
            p_ref[c] = p.astype(BF16)

    def values(j, p_ref, al_ref):
        vt = vt_ref[jnp.minimum(j, nkb - 1)]
        for c in range(2):
            acc_ref[c] = al_ref[c] * acc_ref[c] + _dot(vt, p_ref[c])

    scores(0, sa_ref)
    scores(1, sb_ref)
    softmax(0, sa_ref, pa_ref, ala_ref, False)

    def pair(t, carry, far_only):
        j = 2 * t
        scores(j + 2, sa_ref)
        softmax(j + 1, sb_ref, pb_ref, alb_ref, far_only)
        values(j, pa_ref, ala_ref)
        scores(j + 3, sb_ref)
        softmax(j + 2, sa_ref, pa_ref, ala_ref, far_only)
        values(j + 1, pb_ref, alb_ref)
        return carry

    n_far = jnp.maximum(r * i - 2, 0) // 2
    lax.fori_loop(0, n_far, functools.partial(pair, far_only=True), 0)
    lax.fori_loop(n_far, (r * i + r + 1) // 2, functools.partial(pair, far_only=False), 0)

    lam = _lambda(lam_ref[0:1, :], lam_ref[1:2, :], lam_ref[2:3, :], lam_ref[3:4, :], lambda_init)
    o = acc_ref[0] / l_ref[0] - lam * (acc_ref[1] / l_ref[1])
    on = o * lax.rsqrt(jnp.mean(o * o, axis=0, keepdims=True) + EPS) * (1.0 - lambda_init)
    o_ref[...] = (on.T * gain_ref[...]).astype(BF16)


def flash_diff_attention(q, k, vt, table, lam_params, gain, lambda_init):
    t, d = q.shape
    tq, tk = min(CFG["fa_tq"], t), min(CFG["fa_tk"], t)
    assert tq % tk == 0 and tk % ATT_CHUNK == 0 and vt.shape == (t // tk, d, tk)
    r = tq // tk
    hw = 2 * DA_DH
    nh = d // hw
    far = _rel_bucket_np(-np.arange(tk + 1, max(t, tk + 2)))
    far_bucket = int(far[0])
    assert (far == far_bucket).all()
    near = _bucket_tile(tk + np.arange(tq), np.arange((r + 1) * tk)).T
    bucket = np.concatenate([np.full((tk, tq), far_bucket, np.int32), near,
                             np.full((tk, tq), -1, np.int32)], axis=0)
    nb = r + 3
    bank = bias_tiles(table, bucket, bands=nb).reshape(nh, nb, tk, tq)
    once = pl.Buffered(1)
    return pl.pallas_call(
        functools.partial(_flash_kernel, tq=tq, tk=tk, far_bucket=far_bucket, lambda_init=lambda_init),
        grid=(nh, t // tq),
        in_specs=[pl.BlockSpec(memory_space=pltpu.SMEM),
                  pl.BlockSpec((tq, hw), lambda h, i: (i, h)),
                  pl.BlockSpec((t, hw), lambda h, i: (0, h), pipeline_mode=once),
                  pl.BlockSpec((t // tk, hw, tk), lambda h, i: (0, h, 0), pipeline_mode=once),
                  pl.BlockSpec((1, nb, tk, tq), lambda h, i: (h, 0, 0, 0), pipeline_mode=once),
                  pl.BlockSpec((4, DA_DH), lambda h, i: (0, 0)),
                  pl.BlockSpec((1, hw), lambda h, i: (0, 0))],
        out_specs=pl.BlockSpec((tq, hw), lambda h, i: (i, h)),
        out_shape=jax.ShapeDtypeStruct((t, d), BF16),
        scratch_shapes=[pltpu.VMEM((2, 1, tq), F32), pltpu.VMEM((2, 1, tq), F32),
                        pltpu.VMEM((2, hw, tq), F32),
                        pltpu.VMEM((2, tk, tq), F32), pltpu.VMEM((2, tk, tq), F32),
                        pltpu.VMEM((2, tk, tq), BF16), pltpu.VMEM((2, tk, tq), BF16),
                        pltpu.VMEM((2, 1, tq), F32), pltpu.VMEM((2, 1, tq), F32)],
        compiler_params=_cp(("parallel", "arbitrary")),
        name="flash_diff_attention",
    )(table, q, k, vt, bank, lam_params, gain.reshape(1, hw))


def _sample_attn_kernel(tab_ref, q_ref, ck_ref, cvl_ref, cvh_ref, kn_ref, vn_ref, blast_ref, bnew_ref, lam_ref,
                        gain_ref, o_ref, m_ref, l_ref, acc_ref, *, tk, far_bucket, lambda_init):
    j = pl.program_id(1)
    nj = pl.num_programs(1)
    dh = DA_DH
    hw = 2 * dh
    nh = q_ref.shape[1] // hw
    ts = q_ref.shape[0]

    @pl.when(j == 0)
    def _():
        m_ref[...] = jnp.full(m_ref.shape, -jnp.inf, F32)
        l_ref[...] = jnp.zeros(l_ref.shape, F32)
        acc_ref[...] = jnp.zeros(acc_ref.shape, F32)

    q = q_ref[...]

    def block(k_of, v_of, bias_fn):
        for h in range(nh):
            vb = v_of(h)
            for c in range(2):
                cs = slice((2 * h + c) * dh, (2 * h + c + 1) * dh)
                _softmax_step(q[:, cs], k_of(2 * h + c), vb, bias_fn(h), m_ref, l_ref, acc_ref, 2 * h + c)

    def cache_k_of(hc):
        return ck_ref[0, pl.ds(hc, tk, stride=2 * nh), :].astype(BF16)

    def cache_v_of(h):
        return jnp.concatenate([cvl_ref[0, pl.ds(h, tk, stride=nh), :],
                                cvh_ref[0, pl.ds(h, tk, stride=nh), :]], axis=1).astype(BF16)

    @pl.when(j < nj - 1)
    def _():
        block(cache_k_of, cache_v_of, lambda h: tab_ref[far_bucket, h] * LOG2E)

    @pl.when(j == nj - 1)
    def _():
        block(cache_k_of, cache_v_of, lambda h: blast_ref[h])
        block(lambda hc: kn_ref[:, hc * dh:(hc + 1) * dh], lambda h: vn_ref[:, h * hw:(h + 1) * hw],
              lambda h: bnew_ref[h][:, :ts])
        lam = _lambda(lam_ref[0:1, :], lam_ref[1:2, :], lam_ref[2:3, :], lam_ref[3:4, :], lambda_init)
        for h in range(nh):
            o = acc_ref[2 * h] / l_ref[2 * h] - lam * (acc_ref[2 * h + 1] / l_ref[2 * h + 1])
            o_ref[:, h * hw:(h + 1) * hw] = (_rms(o, gain_ref[...]) * (1.0 - lambda_init)).astype(BF16)


def sample_diff_attention(q, k_new, v_new, cache_k, cache_v, layer, table, lam_params, gain, lambda_init):
    n_layers, batch, past, nhc, dh = cache_k.shape
    d = nhc * dh
    ts = q.shape[0] // batch
    hw = 2 * DA_DH
    nh = d // hw
    tk = min(CFG["sa_tk"], past)
    nkb = past // tk
    q_pos = past + np.arange(ts)
    blast = bias_tiles(table, _bucket_tile(q_pos, past - tk + np.arange(tk)))
    bnew = bias_tiles(table, _bucket_tile(q_pos, past + np.arange(128)))
    if nkb > 1:
        far = _bucket_tile(q_pos, np.arange(past - tk))
        far_bucket = int(far[0, 0])
        assert (far == far_bucket).all()
    else:
        far_bucket = 0
    cache_v_rows = cache_v.reshape(n_layers, batch, past * nh, hw)
    return pl.pallas_call(
        functools.partial(_sample_attn_kernel, tk=tk, far_bucket=far_bucket, lambda_init=lambda_init),
        grid=(batch, nkb),
        in_specs=[pl.BlockSpec(memory_space=pltpu.SMEM),
                  pl.BlockSpec((ts, d), lambda b, j: (b, 0)),
                  pl.BlockSpec((None, 1, tk * nhc, dh), lambda b, j: (layer, b, j, 0)),
                  pl.BlockSpec((None, 1, tk * nh, dh), lambda b, j: (layer, b, j, 0)),
                  pl.BlockSpec((None, 1, tk * nh, dh), lambda b, j: (layer, b, j, 1)),
                  pl.BlockSpec((ts, d), lambda b, j: (b, 0)),
                  pl.BlockSpec((ts, d), lambda b, j: (b, 0)),
                  pl.BlockSpec((nh, ts, tk), lambda b, j: (0, 0, 0)),
                  pl.BlockSpec((nh, ts, 128), lambda b, j: (0, 0, 0)),
                  pl.BlockSpec((4, DA_DH), lambda b, j: (0, 0)),
                  pl.BlockSpec((1, hw), lambda b, j: (0, 0))],
        out_specs=pl.BlockSpec((ts, d), lambda b, j: (b, 0)),
        out_shape=jax.ShapeDtypeStruct((batch * ts, d), BF16),
        scratch_shapes=[pltpu.VMEM((2 * nh, ts, 1), F32), pltpu.VMEM((2 * nh, ts, 1), F32),
                        pltpu.VMEM((2 * nh, ts, hw), F32)],
        compiler_params=_cp(("parallel", "arbitrary")),
        name="sample_diff_attention",
    )(table, q, cache_k.reshape(n_layers, batch, past * nhc, dh), cache_v_rows, cache_v_rows,
      k_new, v_new, blast, bnew, lam_params, gain.reshape(1, hw))


def kernel(x_prompt, x_sample, state_hgrn, state_pool, cache_k, cache_v, norm_mix, norm_ffn, norm_final, hgrn_w_q, hgrn_w_f, hgrn_w_i, hgrn_w_g, hgrn_w_o, hgrn_lb_logits, hgrn_norm_gain, pool_w, pool_scale, attn_w_q, attn_w_k, attn_w_v, attn_w_o, attn_lambda_q1, attn_lambda_k1, attn_lambda_q2, attn_lambda_k2, attn_subln_gain, rel_bias_table, ffn_w_gate, ffn_w_up, ffn_w_down):
    bp, tp, d = x_prompt.shape
    bs, ts, _ = x_sample.shape
    past = cache_k.shape[2]
    depth = norm_mix.shape[0]
    assert bp == 1
    bf = lambda a: a.astype(BF16)
    xs = [x_prompt.reshape(bp * tp, d), x_sample.reshape(bs * ts, d)]
    dims = [(bp, tp), (bs, ts)]
    hg, pool_st, k_out, v_out = [[], []], [[], []], [[], []], [[], []]
    w_g, w_u, w_d = bf(ffn_w_gate), bf(ffn_w_up), bf(ffn_w_down)
    zero_state = jnp.zeros((1, bp, HG_HEADS, HG_DK, HG_DK), F32)

    for i in range(depth):
        m, j = i % N_MIXERS, i // N_MIXERS
        if m == 0:
            w_cat = bf(jnp.concatenate([hgrn_w_q[j], hgrn_w_f[j], hgrn_w_i[j], hgrn_w_g[j]], axis=1))
            w_o = bf(hgrn_w_o[j])
            s0s = [(zero_state, 0), (state_hgrn, j)]
            for r in range(2):
                b, t = dims[r]
                proj = norm_mm(xs[r], norm_mix[i], w_cat)
                og, st = hgrn_recurrence(proj, hgrn_lb_logits, hgrn_norm_gain[j], s0s[r], b, t, j)
                xs[r] = mm_res(og, w_o, xs[r])
                hg[r].append(st)
        elif m == 1:
            w_p = bf(pool_w[j])
            hists = [jnp.zeros((bp, POOL_HIST, d), F32), state_pool[j]]
            for r in range(2):
                b, t = dims[r]
                xs[r], st = pool_mixer(xs[r], norm_mix[i], hists[r], w_p, pool_scale[j], b, t, (0, past)[r])
                pool_st[r].append(st[:, 1:, :])
        else:
            lambda_init = 0.8 - 0.6 * math.exp(-0.3 * i)
            w_qkv = bf(jnp.concatenate([attn_w_q[j], attn_w_k[j], attn_w_v[j]], axis=1))
            w_o = bf(attn_w_o[j])
            lam_params = jnp.stack([attn_lambda_q1[j], attn_lambda_k1[j], attn_lambda_q2[j], attn_lambda_k2[j]])
            for r in range(2):
                b, t = dims[r]
                q, k32, k16, v32, v16 = qkv_proj(xs[r], norm_mix[i], w_qkv, DA_DH ** -0.5 * LOG2E,
                                                 min(CFG["fa_tk"], t) if r == 0 else None)
                if r == 0:
                    o = flash_diff_attention(q, k16, v16, rel_bias_table, lam_params, attn_subln_gain[j],
                                             lambda_init)
                else:
                    o = sample_diff_attention(q, k16, v16, cache_k, cache_v, j, rel_bias_table, lam_params,
                                              attn_subln_gain[j], lambda_init)
                xs[r] = mm_res(o, w_o, xs[r])
                k_out[r].append(k32.reshape(b, t, 2 * DA_HEADS, DA_DH))
                v_out[r].append(v32.reshape(b, t, DA_HEADS, 2 * DA_DH))
        g_final = norm_final if i == depth - 1 else None
        for r in range(2):
            xs[r] = ffn(xs[r], norm_ffn[i], w_g, w_u, w_d, i, g_final)

    return (xs[0].reshape(bp, tp, d), xs[1].reshape(bs, ts, d),
            jnp.stack(hg[0]), jnp.stack(hg[1]), jnp.stack(pool_st[0]), jnp.stack(pool_st[1]),
            jnp.stack(k_out[0]), jnp.stack(v_out[0]), jnp.stack(k_out[1]), jnp.stack(v_out[1]))
```

```python
import functools
import math

import numpy as np
import jax
import jax.numpy as jnp
from jax import lax
from jax.experimental import pallas as pl
from jax.experimental.pallas import tpu as pltpu

F32 = jnp.float32
BF16 = jnp.bfloat16

EPS = 1e-6
LOG2E = math.log2(math.e)
F_MIN = 1e-6
MASK_VALUE = -1e30
HG_HEADS = 16
HG_DK = 128
POOL_WINDOWS = (2, 4, 8, 16)
POOL_HIST = 15
DA_HEADS = 8
DA_DH = 128
ATT_CHUNK = 64
REL_BUCKETS = 32
REL_MAX_DIST = 128
N_MIXERS = 3

CFG = dict(
    mm_tm=1024, mm_tn=1024,
    qkv_tn=512,
    ffn_tm=1024, ffn_tf=256,
    pool_tm=512,
    hg_chunk=256, hg_rows=512, hg_heads=8, hg_unroll=2,
    fa_tq=512, fa_tk=512,
    sa_tk=1024,
    vmem=56 * 1024 * 1024,
)


def _cp(sem):
    return pltpu.CompilerParams(dimension_semantics=sem, vmem_limit_bytes=CFG["vmem"])


def _rms(x, g):
    return x * lax.rsqrt(jnp.mean(x * x, axis=-1, keepdims=True) + EPS) * g


def _dot(a, b):
    return jnp.dot(a, b, preferred_element_type=F32)


def _dot_nt(a, b):
    return lax.dot_general(a, b, (((1,), (1,)), ((), ())), preferred_element_type=F32)


def _norm_mm_kernel(x_ref, g_ref, w_ref, o_ref, xn_ref):
    @pl.when(pl.program_id(1) == 0)
    def _():
        xn_ref[...] = _rms(x_ref[...], g_ref[...]).astype(BF16)

    o_ref[...] = _dot(xn_ref[...], w_ref[...])


def norm_mm(x, g, w):
    m, d = x.shape
    n = w.shape[1]
    tm, tn = min(CFG["mm_tm"], m), min(CFG["mm_tn"], n)
    return pl.pallas_call(
        _norm_mm_kernel,
        grid=(m // tm, n // tn),
        in_specs=[pl.BlockSpec((tm, d), lambda i, j: (i, 0)),
                  pl.BlockSpec((1, d), lambda i, j: (0, 0)),
                  pl.BlockSpec((d, tn), lambda i, j: (0, j))],
        out_specs=pl.BlockSpec((tm, tn), lambda i, j: (i, j)),
        out_shape=jax.ShapeDtypeStruct((m, n), F32),
        scratch_shapes=[pltpu.VMEM((tm, d), BF16)],
        compiler_params=_cp(("parallel", "arbitrary")),
        name="norm_mm",
    )(x, g.reshape(1, d), w)


def _qkv_kernel(x_ref, g_ref, w_ref, q_ref, k32_ref, k16_ref, v32_ref, v16_ref, xn_ref, *, nq, q_scale, v_transposed):
    j = pl.program_id(1)

    @pl.when(j == 0)
    def _():
        xn_ref[...] = _rms(x_ref[...], g_ref[...]).astype(BF16)

    y = _dot(xn_ref[...], w_ref[...])

    @pl.when(j < nq)
    def _():
        q_ref[...] = (y * q_scale).astype(BF16)

    @pl.when((j >= nq) & (j < 2 * nq))
    def _():
        k32_ref[...] = y
        k16_ref[...] = y.astype(BF16)

    @pl.when(j >= 2 * nq)
    def _():
        v32_ref[...] = y
        if v_transposed:
            tkv = v16_ref.shape[2]
            for s in range(v16_ref.shape[0]):
                v16_ref[s] = y[s * tkv:(s + 1) * tkv, :].T.astype(BF16)
        else:
            v16_ref[...] = y.astype(BF16)


def qkv_proj(x, g, w, q_scale, v_transposed_rows):
    m, d = x.shape
    n = w.shape[1] // 3
    tm, tn = min(CFG["mm_tm"], m), min(CFG["qkv_tn"], n)
    nq = n // tn
    v_transposed = v_transposed_rows is not None
    spec = lambda off: pl.BlockSpec((tm, tn), lambda i, j: (i, jnp.clip(j - off, 0, nq - 1)))
    if v_transposed:
        tkv = v_transposed_rows
        v16_spec = pl.BlockSpec((tm // tkv, tn, tkv), lambda i, j: (i, jnp.clip(j - 2 * nq, 0, nq - 1), 0))
        v16_shape = jax.ShapeDtypeStruct((m // tkv, n, tkv), BF16)
    else:
        v16_spec, v16_shape = spec(2 * nq), jax.ShapeDtypeStruct((m, n), BF16)
    return pl.pallas_call(
        functools.partial(_qkv_kernel, nq=nq, q_scale=q_scale, v_transposed=v_transposed),
        grid=(m // tm, 3 * nq),
        in_specs=[pl.BlockSpec((tm, d), lambda i, j: (i, 0)),
                  pl.BlockSpec((1, d), lambda i, j: (0, 0)),
                  pl.BlockSpec((d, tn), lambda i, j: (0, j))],
        out_specs=[spec(0), spec(nq), spec(nq), spec(2 * nq), v16_spec],
        out_shape=[jax.ShapeDtypeStruct((m, n), BF16), jax.ShapeDtypeStruct((m, n), F32),
                   jax.ShapeDtypeStruct((m, n), BF16), jax.ShapeDtypeStruct((m, n), F32), v16_shape],
        scratch_shapes=[pltpu.VMEM((tm, d), BF16)],
        compiler_params=_cp(("parallel", "arbitrary")),
        name="qkv_proj",
    )(x, g.reshape(1, d), w)


def _mm_res_kernel(a_ref, w_ref, r_ref, o_ref):
    o_ref[...] = r_ref[...] + _dot(a_ref[...], w_ref[...])


def mm_res(a, w, r):
    m, k = a.shape
    n = w.shape[1]
    tm, tn = min(CFG["mm_tm"], m), min(CFG["mm_tn"], n)
    return pl.pallas_call(
        _mm_res_kernel,
        grid=(m // tm, n // tn),
        in_specs=[pl.BlockSpec((tm, k), lambda i, j: (i, 0)),
                  pl.BlockSpec((k, tn), lambda i, j: (0, j)),
                  pl.BlockSpec((tm, tn), lambda i, j: (i, j))],
        out_specs=pl.BlockSpec((tm, tn), lambda i, j: (i, j)),
        out_shape=jax.ShapeDtypeStruct((m, n), F32),
        compiler_params=_cp(("parallel", "arbitrary")),
        name="mm_res",
    )(a, w, r)


def _ffn_kernel(x_ref, g_ref, wg_ref, wu_ref, wd_ref, gf_ref, o_ref, xn_ref, *, final_norm):
    j = pl.program_id(1)

    @pl.when(j == 0)
    def _():
        x = x_ref[...]
        xn_ref[...] = _rms(x, g_ref[...]).astype(BF16)
        o_ref[...] = x

    xn = xn_ref[...]
    a = _dot(xn, wg_ref[...])
    b = _dot(xn, wu_ref[...])
    h = (a * jax.nn.sigmoid(a) * b).astype(BF16)
    o_ref[...] += _dot(h, wd_ref[...])

    if final_norm:
        @pl.when(j == pl.num_programs(1) - 1)
        def _():
            o_ref[...] = _rms(o_ref[...], gf_ref[...])


def ffn(x, g, wg, wu, wd, layer, g_final=None):
    m, d = x.shape
    f = wg.shape[2]
    tm = min(CFG["ffn_tm"], m)
    tf = min(CFG["ffn_tf"] * min(CFG["ffn_tm"] // tm, 2), f)
    final_norm = g_final is not None
    gf = (g_final if final_norm else g).reshape(1, d)
    return pl.pallas_call(
        functools.partial(_ffn_kernel, final_norm=final_norm),
        grid=(m // tm, f // tf),
        in_specs=[pl.BlockSpec((tm, d), lambda i, j: (i, 0)),
                  pl.BlockSpec((1, d), lambda i, j: (0, 0)),
                  pl.BlockSpec((None, d, tf), lambda i, j: (layer, 0, j)),
                  pl.BlockSpec((None, d, tf), lambda i, j: (layer, 0, j)),
                  pl.BlockSpec((None, tf, d), lambda i, j: (layer, j, 0)),
                  pl.BlockSpec((1, d), lambda i, j: (0, 0))],
        out_specs=pl.BlockSpec((tm, d), lambda i, j: (i, 0)),
        out_shape=jax.ShapeDtypeStruct((m, d), F32),
        scratch_shapes=[pltpu.VMEM((tm, d), BF16)],
        compiler_params=_cp(("parallel", "arbitrary")),
        name="ffn",
    )(x, g.reshape(1, d), wg, wu, wd, gf)


def _hgrn_levels(c):
    lv, h = [], c // 2
    while h >= 8:
        lv.append(h)
        h //= 2
    return lv


def _bcast_rows(g, rows, rep):
    return jnp.concatenate([jnp.broadcast_to(g[r:r + 1, :], (rep, g.shape[1])) for r in rows], axis=0)


def _hgrn_kernel(q_ref, f_ref, v_ref, gt_ref, lbl_ref, gain_ref, s0_ref, l_ref, og_ref, st_ref, stt_ref,
                 *, c, n_sub, hb, layer):
    cb = pl.program_id(2)
    dk = HG_DK

    @pl.when(cb == 0)
    def _():
        for h in range(hb):
            stt_ref[h] = s0_ref[0, h].T

    lg = lbl_ref[...]
    e = jnp.exp(lg - jnp.max(lg, axis=0, keepdims=True))
    p = e / jnp.sum(e, axis=0, keepdims=True)
    lb = p[0:1, :]
    for i in range(1, layer + 1):
        lb = lb + p[i:i + 1, :]
    lb = jnp.maximum(lb - p[0:1, :], 0.0)

    ti = lax.broadcasted_iota(jnp.int32, (c, c), 0)
    si = lax.broadcasted_iota(jnp.int32, (c, c), 1)
    levels = _hgrn_levels(c)
    masks = []
    for hh in levels:
        sh = int(math.log2(2 * hh))
        masks.append(((ti >> sh) == (si >> sh)) & ((ti & (2 * hh - 1)) >= hh) & ((si & (2 * hh - 1)) < hh))
    mask_loc = ((ti >> 3) == (si >> 3)) & (si <= ti)
    lmat = l_ref[...]
    gain = gain_ref[...]

    def chunk(ci, carry):
        r0 = pl.multiple_of(ci * c, c)
        fp = f_ref[pl.ds(r0, c), :]
        f = lb + (1.0 - lb) * jax.nn.sigmoid(fp)
        g = jnp.log2(jnp.maximum(f, F_MIN))
        kk = 1.0 - f
        g_hi = g.astype(BF16)
        r1 = g - g_hi.astype(F32)
        g_mid = r1.astype(BF16)
        g_lo = (r1 - g_mid.astype(F32)).astype(BF16)
        gc = _dot(lmat, g_hi) + _dot(lmat, g_mid) + _dot(lmat, g_lo)
        qa = q_ref[pl.ds(r0, c), :]
        va = v_ref[pl.ds(r0, c), :]
        ga = gt_ref[pl.ds(r0, c), :]
        for h in range(hb):
            sl = slice(h * dk, (h + 1) * dk)
            gh, qh, kh = gc[:, sl], qa[:, sl], kk[:, sl]
            vh = va[:, sl].astype(BF16)
            stt = stt_ref[h]
            inter = _dot_nt((qh * jnp.exp2(gh)).astype(BF16), stt.astype(BF16))
            g_last = gh[c - 1:c, :]
            k_dec = (kh * jnp.exp2(g_last - gh)).astype(BF16)
            stt_ref[h] = stt * jnp.exp2(g_last) + lax.dot_general(
                vh, k_dec, (((0,), (0,)), ((), ())), preferred_element_type=F32)
            q16, k16 = qh.astype(BF16), kh.astype(BF16)
            g_loc = _bcast_rows(gh, [8 * b + 3 for b in range(c // 8)], 8)
            sc = jnp.where(mask_loc,
                           _dot_nt(q16 * jnp.exp2(gh - g_loc).astype(BF16),
                                   k16 * jnp.exp2(g_loc - gh).astype(BF16)), 0.0)
            for hh, mk in zip(levels, masks):
                g_mid_rows = _bcast_rows(gh, [b * 2 * hh + hh - 1 for b in range(c // (2 * hh))], 2 * hh)
                dist = lax.bitcast_convert_type(gh - g_mid_rows, jnp.uint32) | jnp.uint32(0x80000000)
                fac = jnp.exp2(lax.bitcast_convert_type(dist, F32)).astype(BF16)
                sc = jnp.where(mk, _dot_nt(q16 * fac, k16 * fac), sc)
            o = inter + _dot(sc.astype(BF16), vh)
            on = _rms(o, gain)
            gate = ga[:, sl]
            og_ref[pl.ds(r0, c), sl] = (on * (gate * jax.nn.sigmoid(gate))).astype(BF16)
        return carry

    lax.fori_loop(0, n_sub, chunk, 0, unroll=min(CFG["hg_unroll"], n_sub))

    @pl.when(cb == pl.num_programs(2) - 1)
    def _():
        for h in range(hb):
            st_ref[0, h] = stt_ref[h].T


def hgrn_recurrence(proj, lb_logits, gain, s0, batch, t, layer):
    s0, s0_layer = s0
    d = HG_HEADS * HG_DK
    c = min(CFG["hg_chunk"], t)
    rows = min(CFG["hg_rows"], t)
    hb = CFG["hg_heads"] if t > c else HG_HEADS
    w = hb * HG_DK
    ncb = t // rows
    n_layers = lb_logits.shape[0]
    lmat = jnp.asarray(np.tril(np.ones((c, c), np.float32)), BF16)

    def col(sec):
        return lambda b, hg, cb: (b * ncb + cb, sec * (d // w) + hg)

    og, st = pl.pallas_call(
        functools.partial(_hgrn_kernel, c=c, n_sub=rows // c, hb=hb, layer=layer),
        grid=(batch, HG_HEADS // hb, ncb),
        in_specs=[pl.BlockSpec((rows, w), col(0)),
                  pl.BlockSpec((rows, w), col(1)),
                  pl.BlockSpec((rows, w), col(2)),
                  pl.BlockSpec((rows, w), col(3)),
                  pl.BlockSpec((n_layers, w), lambda b, hg, cb: (0, hg)),
                  pl.BlockSpec((1, HG_DK), lambda b, hg, cb: (0, 0)),
                  pl.BlockSpec((None, 1, hb, HG_DK, HG_DK), lambda b, hg, cb: (s0_layer, b, hg, 0, 0)),
                  pl.BlockSpec((c, c), lambda b, hg, cb: (0, 0))],
        out_specs=[pl.BlockSpec((rows, w), lambda b, hg, cb: (b * ncb + cb, hg)),
                   pl.BlockSpec((1, hb, HG_DK, HG_DK), lambda b, hg, cb: (b, hg, 0, 0))],
        out_shape=[jax.ShapeDtypeStruct((batch * t, d), BF16),
                   jax.ShapeDtypeStruct((batch, HG_HEADS, HG_DK, HG_DK), F32)],
        scratch_shapes=[pltpu.VMEM((hb, HG_DK, HG_DK), F32)],
        compiler_params=_cp(("parallel", "parallel", "arbitrary")),
        name="hgrn_recurrence",
    )(proj, proj, proj, proj, lb_logits, gain.reshape(1, HG_DK), s0, lmat)
    return og, st


def _pool_kernel(x_ref, g_ref, hist_ref, w_ref, sc_ref, o_ref, st_ref, buf_ref, *, tm, pos0):
    tb = pl.program_id(1)
    x = x_ref[...]
    xn = _rms(x, g_ref[...])
    hp = POOL_HIST + 1

    @pl.when(tb == 0)
    def _():
        buf_ref[0:hp, :] = hist_ref[0]

    @pl.when(tb > 0)
    def _():
        buf_ref[0:hp, :] = buf_ref[tm:tm + hp, :]

    buf_ref[hp:hp + tm, :] = xn
    st_ref[0] = buf_ref[tm:tm + hp, :]
    pos = pos0 + tb * tm + lax.broadcasted_iota(jnp.int32, (tm, 1), 0)
    gc = x.shape[1] // len(POOL_WINDOWS)
    for gi, win in enumerate(POOL_WINDOWS):
        cs = slice(gi * gc, (gi + 1) * gc)
        ws = xn[:, cs]
        for sft in range(1, win):
            ws = ws + buf_ref[hp - sft:hp - sft + tm, cs]
        cnt = jnp.minimum(win, pos + 1).astype(F32)
        pooled = (ws / cnt - xn[:, cs]).astype(BF16)
        o_ref[:, cs] = x[:, cs] + _dot(pooled, w_ref[gi]) * sc_ref[:, cs]


def pool_mixer(x, g, hist, w, scale, batch, t, pos0):
    d = x.shape[1]
    tm = min(CFG["pool_tm"], t)
    nb = t // tm
    hp = POOL_HIST + 1
    hist16 = jnp.concatenate([jnp.zeros((batch, 1, d), F32), hist], axis=1)
    ng, gc = w.shape[0], w.shape[1]
    return pl.pallas_call(
        functools.partial(_pool_kernel, tm=tm, pos0=pos0),
        grid=(batch, nb),
        in_specs=[pl.BlockSpec((tm, d), lambda b, i: (b * nb + i, 0)),
                  pl.BlockSpec((1, d), lambda b, i: (0, 0)),
                  pl.BlockSpec((1, hp, d), lambda b, i: (b, 0, 0)),
                  pl.BlockSpec((ng, gc, gc), lambda b, i: (0, 0, 0)),
                  pl.BlockSpec((1, d), lambda b, i: (0, 0))],
        out_specs=[pl.BlockSpec((tm, d), lambda b, i: (b * nb + i, 0)),
                   pl.BlockSpec((1, hp, d), lambda b, i: (b, 0, 0))],
        out_shape=[jax.ShapeDtypeStruct(x.shape, F32),
                   jax.ShapeDtypeStruct((batch, hp, d), F32)],
        scratch_shapes=[pltpu.VMEM((tm + hp, d), F32)],
        compiler_params=_cp(("parallel", "arbitrary")),
        name="pool_mixer",
    )(x, g.reshape(1, d), hist16, w, scale.reshape(1, d))


def _rel_bucket_np(rel):
    nb = REL_BUCKETS // 2
    max_exact = nb // 2
    n = np.abs(rel)
    large = max_exact + (np.log(np.maximum(n, max_exact).astype(np.float64) / max_exact)
                         / math.log(REL_MAX_DIST / max_exact) * (nb - max_exact)).astype(np.int64)
    large = np.minimum(large, nb - 1)
    return (np.where(rel > 0, nb, 0) + np.where(n < max_exact, n, large)).astype(np.int32)


def _bucket_tile(q_pos, k_pos):
    q_pos, k_pos = np.asarray(q_pos)[:, None], np.asarray(k_pos)[None, :]
    b = _rel_bucket_np(k_pos - q_pos)
    return np.where((k_pos // ATT_CHUNK) <= (q_pos // ATT_CHUNK), b, -1).astype(np.int32)


def _bias_kernel(tab_ref, bk_ref, o_ref, *, present):
    h = pl.program_id(0)
    rows = bk_ref.shape[0] // len(present)
    for k, values in enumerate(present):
        bk = bk_ref[k * rows:(k + 1) * rows, :]
        bias = jnp.full(bk.shape, MASK_VALUE, F32)
        for b in values:
            bias = jnp.where(bk == b, tab_ref[b, h] * LOG2E, bias)
        o_ref[0, k * rows:(k + 1) * rows, :] = bias


def bias_tiles(table, bucket, bands=1):
    r, c = bucket.shape
    nh = table.shape[1]
    present = tuple(tuple(int(b) for b in np.unique(band) if b >= 0) for band in np.split(bucket, bands, axis=0))
    return pl.pallas_call(
        functools.partial(_bias_kernel, present=present),
        grid=(nh,),
        in_specs=[pl.BlockSpec(memory_space=pltpu.SMEM),
                  pl.BlockSpec((r, c), lambda h: (0, 0))],
        out_specs=pl.BlockSpec((1, r, c), lambda h: (h, 0, 0)),
        out_shape=jax.ShapeDtypeStruct((nh, r, c), F32),
        compiler_params=_cp(("arbitrary",)),
        name="bias_tiles",
    )(table, jnp.asarray(bucket))


def _lambda(lq1, lk1, lq2, lk2, lambda_init):
    return (jnp.exp(jnp.sum(lq1 * lk1, axis=-1, keepdims=True))
            - jnp.exp(jnp.sum(lq2 * lk2, axis=-1, keepdims=True)) + lambda_init)


def _softmax_step(q, k, v, bias, m_ref, l_ref, acc_ref, idx):
    s = _dot_nt(q, k) + bias
    m_prev = m_ref[idx]
    m_new = jnp.maximum(m_prev, jnp.max(s, axis=-1, keepdims=True))
    p = jnp.exp2(s - m_new)
    alpha = jnp.exp2(m_prev - m_new)
    l_ref[idx] = alpha * l_ref[idx] + jnp.sum(p, axis=-1, keepdims=True)
    acc_ref[idx] = alpha * acc_ref[idx] + _dot(p.astype(BF16), v)
    m_ref[idx] = m_new


def _flash_kernel(tab_ref, q_ref, k_ref, vt_ref, bias_ref, lam_ref, gain_ref, o_ref,
                  m_ref, l_ref, acc_ref, sa_ref, sb_ref, pa_ref, pb_ref, ala_ref, alb_ref,
                  *, tq, tk, far_bucket, lambda_init):
    i = pl.program_id(1)
    dh = DA_DH
    m_ref[...] = jnp.full(m_ref.shape, -jnp.inf, F32)
    l_ref[...] = jnp.zeros(l_ref.shape, F32)
    acc_ref[...] = jnp.zeros(acc_ref.shape, F32)
    q = q_ref[...]

    nkb = vt_ref.shape[0]

    def scores(j, s_ref):
        jc = jnp.minimum(j, nkb - 1)
        kb = k_ref[pl.ds(pl.multiple_of(jc * tk, tk), tk), :]
        for c in range(2):
            s_ref[c] = _dot_nt(kb[:, c * dh:(c + 1) * dh], q[:, c * dh:(c + 1) * dh])

    r = bias_ref.shape[1] - 3
    c_far = tab_ref[far_bucket, pl.program_id(0)] * LOG2E

    def softmax(j, s_ref, p_ref, al_ref, far_only):
        cst = c_far if far_only else 0.0
        for c in range(2):
            s = s_ref[c]
            if not far_only:
                s = s + bias_ref[0, jnp.clip(j - (r * i - 2), 0, r + 2)]
            m_prev = m_ref[c]
            m_new = jnp.maximum(m_prev, jnp.max(s, axis=0, keepdims=True) + cst)
            p = jnp.exp2(s - (m_new - cst))
            alpha = jnp.exp2(m_prev - m_new)
            l_ref[c] = alpha * l_ref[c] + jnp.sum(p, axis=0, keepdims=True)
            m_ref[c] = m_new
            al_ref[c] = alpha
            p_ref[c] = p.astype(BF16)

    def values(j, p_ref, al_ref):
        vt = vt_ref[jnp.minimum(j, nkb - 1)]
        for c in range(2):
            acc_ref[c] = al_ref[c] * acc_ref[c] + _dot(vt, p_ref[c])

    scores(0, sa_ref)
    scores(1, sb_ref)
    softmax(0, sa_ref, pa_ref, ala_ref, False)

    def pair(t, carry, far_only):
        j = 2 * t
        values(j, pa_ref, ala_ref)
        scores(j + 2, sa_ref)
        softmax(j + 1, sb_ref, pb_ref, alb_ref, far_only)
        values(j + 1, pb_ref, alb_ref)
        scores(j + 3, sb_ref)
        softmax(j + 2, sa_ref, pa_ref, ala_ref, far_only)
        return carry

    n_far = jnp.maximum(r * i - 2, 0) // 2
    lax.fori_loop(0, n_far, functools.partial(pair, far_only=True), 0)
    lax.fori_loop(n_far, (r * i + r + 1) // 2, functools.partial(pair, far_only=False), 0)

    lam = _lambda(lam_ref[0:1, :], lam_ref[1:2, :], lam_ref[2:3, :], lam_ref[3:4, :], lambda_init)
    o = acc_ref[0] / l_ref[0] - lam * (acc_ref[1] / l_ref[1])
    on = o * lax.rsqrt(jnp.mean(o * o, axis=0, keepdims=True) + EPS) * (1.0 - lambda_init)
    o_ref[...] = (on.T * gain_ref[...]).astype(BF16)


def flash_diff_attention(q, k, vt, table, lam_params, gain, lambda_init):
    t, d = q.shape
    tq, tk = min(CFG["fa_tq"], t), min(CFG["fa_tk"], t)
    assert tq % tk == 0 and tk % ATT_CHUNK == 0 and vt.shape == (t // tk, d, tk)
    r = tq // tk
    hw = 2 * DA_DH
    nh = d // hw
    far = _rel_bucket_np(-np.arange(tk + 1, max(t, tk + 2)))
    far_bucket = int(far[0])
    assert (far == far_bucket).all()
    near = _bucket_tile(tk + np.arange(tq), np.arange((r + 1) * tk)).T
    bucket = np.concatenate([np.full((tk, tq), far_bucket, np.int32), near,
                             np.full((tk, tq), -1, np.int32)], axis=0)
    nb = r + 3
    bank = bias_tiles(table, bucket, bands=nb).reshape(nh, nb, tk, tq)
    once = pl.Buffered(1)
    return pl.pallas_call(
        functools.partial(_flash_kernel, tq=tq, tk=tk, far_bucket=far_bucket, lambda_init=lambda_init),
        grid=(nh, t // tq),
        in_specs=[pl.BlockSpec(memory_space=pltpu.SMEM),
                  pl.BlockSpec((tq, hw), lambda h, i: (i, h)),
                  pl.BlockSpec((t, hw), lambda h, i: (0, h), pipeline_mode=once),
                  pl.BlockSpec((t // tk, hw, tk), lambda h, i: (0, h, 0), pipeline_mode=once),
                  pl.BlockSpec((1, nb, tk, tq), lambda h, i: (h, 0, 0, 0), pipeline_mode=once),
                  pl.BlockSpec((4, DA_DH), lambda h, i: (0, 0)),
                  pl.BlockSpec((1, hw), lambda h, i: (0, 0))],
        out_specs=pl.BlockSpec((tq, hw), lambda h, i: (i, h)),
        out_shape=jax.ShapeDtypeStruct((t, d), BF16),
        scratch_shapes=[pltpu.VMEM((2, 1, tq), F32), pltpu.VMEM((2, 1, tq), F32),
                        pltpu.VMEM((2, hw, tq), F32),
                        pltpu.VMEM((2, tk, tq), F32), pltpu.VMEM((2, tk, tq), F32),
                        pltpu.VMEM((2, tk, tq), BF16), pltpu.VMEM((2, tk, tq), BF16),
                        pltpu.VMEM((2, 1, tq), F32), pltpu.VMEM((2, 1, tq), F32)],
        compiler_params=_cp(("parallel", "arbitrary")),
        name="flash_diff_attention",
    )(table, q, k, vt, bank, lam_params, gain.reshape(1, hw))


def _sample_attn_kernel(tab_ref, q_ref, ck_ref, cvl_ref, cvh_ref, kn_ref, vn_ref, blast_ref, bnew_ref, lam_ref,
                        gain_ref, o_ref, m_ref, l_ref, acc_ref, *, tk, far_bucket, lambda_init):
    j = pl.program_id(1)
    nj = pl.num_programs(1)
    dh = DA_DH
    hw = 2 * dh
    nh = q_ref.shape[1] // hw
    ts = q_ref.shape[0]

    @pl.when(j == 0)
    def _():
        m_ref[...] = jnp.full(m_ref.shape, -jnp.inf, F32)
        l_ref[...] = jnp.zeros(l_ref.shape, F32)
        acc_ref[...] = jnp.zeros(acc_ref.shape, F32)

    q = q_ref[...]

    def block(k_of, v_of, bias_fn):
        for h in range(nh):
            vb = v_of(h)
            for c in range(2):
                cs = slice((2 * h + c) * dh, (2 * h + c + 1) * dh)
                _softmax_step(q[:, cs], k_of(2 * h + c), vb, bias_fn(h), m_ref, l_ref, acc_ref, 2 * h + c)

    def cache_k_of(hc):
        return ck_ref[0, pl.ds(hc, tk, stride=2 * nh), :].astype(BF16)

    def cache_v_of(h):
        return jnp.concatenate([cvl_ref[0, pl.ds(h, tk, stride=nh), :],
                                cvh_ref[0, pl.ds(h, tk, stride=nh), :]], axis=1).astype(BF16)

    @pl.when(j < nj - 1)
    def _():
        block(cache_k_of, cache_v_of, lambda h: tab_ref[far_bucket, h] * LOG2E)

    @pl.when(j == nj - 1)
    def _():
        block(cache_k_of, cache_v_of, lambda h: blast_ref[h])
        block(lambda hc: kn_ref[:, hc * dh:(hc + 1) * dh], lambda h: vn_ref[:, h * hw:(h + 1) * hw],
              lambda h: bnew_ref[h][:, :ts])
        lam = _lambda(lam_ref[0:1, :], lam_ref[1:2, :], lam_ref[2:3, :], lam_ref[3:4, :], lambda_init)
        for h in range(nh):
            o = acc_ref[2 * h] / l_ref[2 * h] - lam * (acc_ref[2 * h + 1] / l_ref[2 * h + 1])
            o_ref[:, h * hw:(h + 1) * hw] = (_rms(o, gain_ref[...]) * (1.0 - lambda_init)).astype(BF16)


def sample_diff_attention(q, k_new, v_new, cache_k, cache_v, layer, table, lam_params, gain, lambda_init):
    n_layers, batch, past, nhc, dh = cache_k.shape
    d = nhc * dh
    ts = q.shape[0] // batch
    hw = 2 * DA_DH
    nh = d // hw
    tk = min(CFG["sa_tk"], past)
    nkb = past // tk
    q_pos = past + np.arange(ts)
    blast = bias_tiles(table, _bucket_tile(q_pos, past - tk + np.arange(tk)))
    bnew = bias_tiles(table, _bucket_tile(q_pos, past + np.arange(128)))
    if nkb > 1:
        far = _bucket_tile(q_pos, np.arange(past - tk))
        far_bucket = int(far[0, 0])
        assert (far == far_bucket).all()
    else:
        far_bucket = 0
    cache_v_rows = cache_v.reshape(n_layers, batch, past * nh, hw)
    return pl.pallas_call(
        functools.partial(_sample_attn_kernel, tk=tk, far_bucket=far_bucket, lambda_init=lambda_init),
        grid=(batch, nkb),
        in_specs=[pl.BlockSpec(memory_space=pltpu.SMEM),
                  pl.BlockSpec((ts, d), lambda b, j: (b, 0)),
                  pl.BlockSpec((None, 1, tk * nhc, dh), lambda b, j: (layer, b, j, 0)),
                  pl.BlockSpec((None, 1, tk * nh, dh), lambda b, j: (layer, b, j, 0)),
                  pl.BlockSpec((None, 1, tk * nh, dh), lambda b, j: (layer, b, j, 1)),
                  pl.BlockSpec((ts, d), lambda b, j: (b, 0)),
                  pl.BlockSpec((ts, d), lambda b, j: (b, 0)),
                  pl.BlockSpec((nh, ts, tk), lambda b, j: (0, 0, 0)),
                  pl.BlockSpec((nh, ts, 128), lambda b, j: (0, 0, 0)),
                  pl.BlockSpec((4, DA_DH), lambda b, j: (0, 0)),
                  pl.BlockSpec((1, hw), lambda b, j: (0, 0))],
        out_specs=pl.BlockSpec((ts, d), lambda b, j: (b, 0)),
        out_shape=jax.ShapeDtypeStruct((batch * ts, d), BF16),
        scratch_shapes=[pltpu.VMEM((2 * nh, ts, 1), F32), pltpu.VMEM((2 * nh, ts, 1), F32),
                        pltpu.VMEM((2 * nh, ts, hw), F32)],
        compiler_params=_cp(("parallel", "arbitrary")),
        name="sample_diff_attention",
    )(table, q, cache_k.reshape(n_layers, batch, past * nhc, dh), cache_v_rows, cache_v_rows,
      k_new, v_new, blast, bnew, lam_params, gain.reshape(1, hw))


def kernel(x_prompt, x_sample, state_hgrn, state_pool, cache_k, cache_v, norm_mix, norm_ffn, norm_final, hgrn_w_q, hgrn_w_f, hgrn_w_i, hgrn_w_g, hgrn_w_o, hgrn_lb_logits, hgrn_norm_gain, pool_w, pool_scale, attn_w_q, attn_w_k, attn_w_v, attn_w_o, attn_lambda_q1, attn_lambda_k1, attn_lambda_q2, attn_lambda_k2, attn_subln_gain, rel_bias_table, ffn_w_gate, ffn_w_up, ffn_w_down):
    bp, tp, d = x_prompt.shape
    bs, ts, _ = x_sample.shape
    past = cache_k.shape[2]
    depth = norm_mix.shape[0]
    assert bp == 1
    bf = lambda a: a.astype(BF16)
    xs = [x_prompt.reshape(bp * tp, d), x_sample.reshape(bs * ts, d)]
    dims = [(bp, tp), (bs, ts)]
    hg, pool_st, k_out, v_out = [[], []], [[], []], [[], []], [[], []]
    w_g, w_u, w_d = bf(ffn_w_gate), bf(ffn_w_up), bf(ffn_w_down)
    zero_state = jnp.zeros((1, bp, HG_HEADS, HG_DK, HG_DK), F32)

    for i in range(depth):
        m, j = i % N_MIXERS, i // N_MIXERS
        if m == 0:
            w_cat = bf(jnp.concatenate([hgrn_w_q[j], hgrn_w_f[j], hgrn_w_i[j], hgrn_w_g[j]], axis=1))
            w_o = bf(hgrn_w_o[j])
            s0s = [(zero_state, 0), (state_hgrn, j)]
            for r in range(2):
                b, t = dims[r]
                proj = norm_mm(xs[r], norm_mix[i], w_cat)
                og, st = hgrn_recurrence(proj, hgrn_lb_logits, hgrn_norm_gain[j], s0s[r], b, t, j)
                xs[r] = mm_res(og, w_o, xs[r])
                hg[r].append(st)
        elif m == 1:
            w_p = bf(pool_w[j])
            hists = [jnp.zeros((bp, POOL_HIST, d), F32), state_pool[j]]
            for r in range(2):
                b, t = dims[r]
                xs[r], st = pool_mixer(xs[r], norm_mix[i], hists[r], w_p, pool_scale[j], b, t, (0, past)[r])
                pool_st[r].append(st[:, 1:, :])
        else:
            lambda_init = 0.8 - 0.6 * math.exp(-0.3 * i)
            w_qkv = bf(jnp.concatenate([attn_w_q[j], attn_w_k[j], attn_w_v[j]], axis=1))
            w_o = bf(attn_w_o[j])
            lam_params = jnp.stack([attn_lambda_q1[j], attn_lambda_k1[j], attn_lambda_q2[j], attn_lambda_k2[j]])
            for r in range(2):
                b, t = dims[r]
                q, k32, k16, v32, v16 = qkv_proj(xs[r], norm_mix[i], w_qkv, DA_DH ** -0.5 * LOG2E,
                                                 min(CFG["fa_tk"], t) if r == 0 else None)
                if r == 0:
                    o = flash_diff_attention(q, k16, v16, rel_bias_table, lam_params, attn_subln_gain[j],
                                             lambda_init)
                else:
                    o = sample_diff_attention(q, k16, v16, cache_k, cache_v, j, rel_bias_table, lam_params,
                                              attn_subln_gain[j], lambda_init)
                xs[r] = mm_res(o, w_o, xs[r])
                k_out[r].append(k32.reshape(b, t, 2 * DA_HEADS, DA_DH))
                v_out[r].append(v32.reshape(b, t, DA_HEADS, 2 * DA_DH))
        g_final = norm_final if i == depth - 1 else None
        for r in range(2):
            xs[r] = ffn(xs[r], norm_ffn[i], w_g, w_u, w_d, i, g_final)

    return (xs[0].reshape(bp, tp, d), xs[1].reshape(bs, ts, d),
            jnp.stack(hg[0]), jnp.stack(hg[1]), jnp.stack(pool_st[0]), jnp.stack(pool_st[1]),
            jnp.stack(k_out[0]), jnp.stack(v_out[0]), jnp.stack(k_out[1]), jnp.stack(v_out[1]))
```
